```python
import math
import jax, jax.numpy as jnp
from jax import lax
import numpy as np

D_MODEL = 1024
BATCH = 32
SEQ = 256
DEPTH = 4
DEC_BATCH = 4
DEC_SEQ = 1024
PAST_LEN = 256

GRID_W = 64
N_DIR = 2
DN_HEADS = 4
DN_DK = 128
DN_DV = 128
DN_QK = DN_HEADS * DN_DK
DN_V = DN_HEADS * DN_DV
CHUNK = 64
CONV_W = 4
LRU_W = 1024
LRU_BLOCKS = 8
LRU_BS = LRU_W // LRU_BLOCKS
LRU_C = 8.0
N_EXPERTS = 16
EXPERT_FF = 2048
EC_CAPACITY = 2
EPS = 1e-6
IN_SPLIT = (2 * DN_QK + DN_V, DN_V, N_DIR * DN_HEADS, N_DIR * DN_HEADS, LRU_W, LRU_W, D_MODEL, D_MODEL)
IN_COLS = 2 * DN_QK + 2 * DN_V + 2 * N_DIR * DN_HEADS + 2 * LRU_W + 2 * D_MODEL

kernel_name = "hybrid_gdn_rglru_ec_diffusion_step"


def _rms(x, g):
    xf = x.astype(jnp.float32)
    y = xf * lax.rsqrt(jnp.mean(xf * xf, axis=-1, keepdims=True) + EPS)
    return (y * g.astype(jnp.float32)).astype(x.dtype)


def _l2norm(x):
    return x * lax.rsqrt(jnp.sum(x * x, axis=-1, keepdims=True) + EPS)


def _conv_centred(x, w, b=None):
    t = x.shape[1]
    left = CONV_W // 2
    xp = jnp.pad(x, ((0, 0), (left, CONV_W - 1 - left), (0, 0)))
    y = sum(xp[:, j:j + t] * w[j] for j in range(CONV_W))
    return y if b is None else y + b


def _pos_embed_2d(n_tokens):
    rows = n_tokens // GRID_W
    r = jnp.broadcast_to(jnp.arange(rows, dtype=jnp.float32)[:, None], (rows, GRID_W)).reshape(-1)
    col = jnp.broadcast_to(jnp.arange(GRID_W, dtype=jnp.float32)[None, :], (rows, GRID_W)).reshape(-1)
    quarter = D_MODEL // 4
    freq = jnp.exp(-math.log(10000.0) * jnp.arange(quarter, dtype=jnp.float32) / quarter)
    ar = r[:, None] * freq
    ac = col[:, None] * freq
    return jnp.concatenate([jnp.sin(ar), jnp.cos(ar), jnp.sin(ac), jnp.cos(ac)], axis=-1)


def _gated_delta_chunked(q, k, v, beta, g, s0):
    bsz, t, h, dk = q.shape
    dv = v.shape[-1]
    n = t // CHUNK

    def chunks(a):
        return a.reshape(bsz, n, CHUNK, h, -1).transpose(0, 3, 1, 2, 4)

    q = chunks(q) * (dk ** -0.5)
    k = chunks(k)
    v = chunks(v)
    beta = chunks(beta[..., None])[..., 0]
    decay = jnp.cumsum(chunks(g[..., None])[..., 0], axis=-1)
    lower = jnp.tril(jnp.ones((CHUNK, CHUNK), dtype=bool))
    strict = jnp.tril(jnp.ones((CHUNK, CHUNK), dtype=bool), -1)
    diff = decay[..., :, None] - decay[..., None, :]
    gamma = jnp.where(lower, jnp.exp(jnp.where(lower, diff, 0.0)), 0.0)
    k_beta = k * beta[..., None]
    a = jnp.where(strict, jnp.einsum('bhncd,bhnmd->bhncm', k_beta, k) * gamma, 0.0)
    rhs = jnp.concatenate([v * beta[..., None], k_beta * jnp.exp(decay)[..., None]], axis=-1)
    sol = lax.linalg.triangular_solve(a + jnp.eye(CHUNK, dtype=a.dtype), rhs,
                                      left_side=True, lower=True, unit_diagonal=True)
    u, w = sol[..., :dv], sol[..., dv:]
    attn = jnp.where(lower, jnp.einsum('bhncd,bhnmd->bhncm', q, k) * gamma, 0.0)
    q_dec = q * jnp.exp(decay)[..., None]
    k_dec = k * jnp.exp(decay[..., -1:] - decay)[..., None]
    chunk_decay = jnp.exp(decay[..., -1])

    def first(a_):
        return jnp.moveaxis(a_, 2, 0)

    def step(s, inp):
        u_i, w_i, qd_i, kd_i, at_i, cd_i = inp
        v_new = u_i - jnp.einsum('bhck,bhkv->bhcv', w_i, s)
        o_i = jnp.einsum('bhck,bhkv->bhcv', qd_i, s) + jnp.einsum('bhcm,bhmv->bhcv', at_i, v_new)
        s = s * cd_i[..., None, None] + jnp.einsum('bhck,bhcv->bhkv', kd_i, v_new)
        return s, o_i

    s_last, o = lax.scan(step, s0, (first(u), first(w), first(q_dec), first(k_dec),
                                    first(attn), first(chunk_decay)))
    o = o.transpose(1, 0, 3, 2, 4).reshape(bsz, t, h, dv)
    return o, s_last


def _gdn_branch(qkv_in, z, b_in, a_in, conv_w, a_log, dt_bias, norm_g, w_out, s0):
    bsz, t, _ = qkv_in.shape
    qkv = jax.nn.silu(_conv_centred(qkv_in, conv_w)).astype(jnp.float32)
    q = _l2norm(qkv[..., :DN_QK].reshape(bsz, t, DN_HEADS, DN_DK))
    k = _l2norm(qkv[..., DN_QK:2 * DN_QK].reshape(bsz, t, DN_HEADS, DN_DK))
    v = qkv[..., 2 * DN_QK:].reshape(bsz, t, DN_HEADS, DN_DV)
    beta = jax.nn.sigmoid(b_in.astype(jnp.float32)).reshape(bsz, t, N_DIR, DN_HEADS)
    g = -jnp.exp(a_log.astype(jnp.float32)) * jax.nn.softplus(
        a_in.astype(jnp.float32).reshape(bsz, t, N_DIR, DN_HEADS) + dt_bias.astype(jnp.float32))
    s0 = s0.astype(jnp.float32)
    o_f, s_f = _gated_delta_chunked(q, k, v, beta[:, :, 0], g[:, :, 0], s0[:, 0])
    rev = lambda a_: jnp.flip(a_, axis=1)
    o_b, s_b = _gated_delta_chunked(rev(q), rev(k), rev(v), rev(beta[:, :, 1]), rev(g[:, :, 1]), s0[:, 1])
    o = o_f + rev(o_b)
    o = o * lax.rsqrt(jnp.mean(o * o, axis=-1, keepdims=True) + EPS) * norm_g.astype(jnp.float32)
    o = o * jax.nn.silu(z.astype(jnp.float32).reshape(bsz, t, DN_HEADS, DN_DV))
    y = o.reshape(bsz, t, DN_V).astype(qkv_in.dtype) @ w_out
    return y, jnp.stack([s_f, s_b], axis=1)


def _rglru_scan(u, wa, ba, wx, bx, lam, h0):
    bsz, t, wd = u.shape
    ub = u.reshape(bsz, t, LRU_BLOCKS, LRU_BS)
    r = jax.nn.sigmoid(jnp.einsum('btni,nij->btnj', ub, wa.astype(jnp.float32)).reshape(bsz, t, wd) + ba)
    i = jax.nn.sigmoid(jnp.einsum('btni,nij->btnj', ub, wx.astype(jnp.float32)).reshape(bsz, t, wd) + bx)
    log_a = LRU_C * r * jax.nn.log_sigmoid(lam.astype(jnp.float32))
    a = jnp.exp(log_a)
    b = jnp.sqrt(-jnp.expm1(2.0 * log_a)) * (i * u)
    b = b.at[:, 0].add(a[:, 0] * h0)

    def combine(c1, c2):
        return c1[0] * c2[0], c2[0] * c1[1] + c2[1]

    _, h = lax.associative_scan(combine, (a, b), axis=1)
    return h, h[:, -1]


def _rglru_branch(x_in, y_in, conv_w, conv_b, wa, ba, wx, bx, lam, w_out, h0):
    u = _conv_centred(x_in, conv_w, conv_b).astype(jnp.float32)
    h0 = h0.astype(jnp.float32)
    ba = ba.astype(jnp.float32)
    bx = bx.astype(jnp.float32)
    h_f, last_f = _rglru_scan(u, wa[0], ba[0], wx[0], bx[0], lam[0], h0[:, 0])
    h_b, last_b = _rglru_scan(jnp.flip(u, axis=1), wa[1], ba[1], wx[1], bx[1], lam[1], h0[:, 1])
    rec = h_f + jnp.flip(h_b, axis=1)
    y = (jax.nn.gelu(y_in.astype(jnp.float32)) * rec).astype(x_in.dtype) @ w_out
    return y, jnp.stack([last_f, last_b], axis=1)


def _expert_choice_ffn(h, w_router, w_gate, w_up, w_down):
    bsz, t, d = h.shape
    n = bsz * t
    cap = EC_CAPACITY * n // N_EXPERTS
    xt = h.reshape(n, d)
    aff = jax.nn.softmax((xt @ w_router).astype(jnp.float32), axis=-1)
    gval, idx = lax.top_k(aff.T, cap)
    xe = xt[idx]
    hid = jax.nn.silu(jnp.einsum('ecd,edf->ecf', xe, w_gate)) * jnp.einsum('ecd,edf->ecf', xe, w_up)
    ye = jnp.einsum('ecf,efd->ecd', hid, w_down) * gval[..., None].astype(xt.dtype)
    out = jnp.zeros_like(xt).at[idx.reshape(-1)].add(ye.reshape(-1, d))
    return out.reshape(bsz, t, d)


def _run_stack(x, cond, sd0, sl0, norm1_g, w_mod, b_mod, w_in, conv_qkv, dn_a_log, dn_dt_bias,
               dn_norm_g, w_dn_out, conv_lru_w, conv_lru_b, lru_wa, lru_ba, lru_wx, lru_bx,
               lru_lambda, w_lru_out, w_o, norm2_g, w_router, w_gate, w_up, w_down, final_g):
    split_at = [int(s) for s in np.cumsum(IN_SPLIT)[:-1]]
    sd_out, sl_out = [], []
    for l in range(DEPTH):
        mod = (jax.nn.silu(cond) @ w_mod[l] + b_mod[l])[:, None, :]
        sh1, sc1, g1, sh2, sc2, g2 = jnp.split(mod, 6, axis=-1)
        hn = _rms(x, norm1_g[l]) * (1 + sc1) + sh1
        cols = hn @ w_in[l]
        qkv_in, z, b_in, a_in, lx, ly, ga, gb = jnp.split(cols, split_at, axis=-1)
        y_a, sd = _gdn_branch(qkv_in, z, b_in, a_in, conv_qkv[l], dn_a_log[l], dn_dt_bias[l],
                              dn_norm_g[l], w_dn_out[l], sd0[:, l])
        y_b, sl = _rglru_branch(lx, ly, conv_lru_w[l], conv_lru_b[l], lru_wa[l], lru_ba[l],
                                lru_wx[l], lru_bx[l], lru_lambda[l], w_lru_out[l], sl0[:, l])
        mix = (jax.nn.sigmoid(ga) * y_a + jax.nn.sigmoid(gb) * y_b) @ w_o[l]
        x = x + g1 * mix
        hn = _rms(x, norm2_g[l]) * (1 + sc2) + sh2
        x = x + g2 * _expert_choice_ffn(hn, w_router[l], w_gate[l], w_up[l], w_down[l])
        sd_out.append(sd.astype(x.dtype))
        sl_out.append(sl.astype(x.dtype))
    return _rms(x, final_g), jnp.stack(sd_out, axis=1), jnp.stack(sl_out, axis=1)


def setup_inputs(seed: int = 0) -> dict:
    key = jax.random.key(seed)
    ks = jax.random.split(key, 32)
    f32 = jnp.float32
    nrm = lambda k_, shape, s: jax.random.normal(k_, shape, f32) * s
    gain = lambda k_, shape: 1.0 + 0.01 * jax.random.normal(k_, shape, f32)
    dt = jnp.exp(jax.random.uniform(ks[10], (DEPTH, N_DIR, DN_HEADS), f32, math.log(1e-3), math.log(1e-1)))
    a_lru = jax.random.uniform(ks[19], (DEPTH, N_DIR, LRU_W), f32, 0.9, 0.999)
    return {
        "x_prompt": nrm(ks[0], (BATCH, SEQ, D_MODEL), 1.0),
        "x_sample": nrm(ks[1], (DEC_BATCH, DEC_SEQ, D_MODEL), 1.0),
        "state_delta": nrm(ks[2], (DEC_BATCH, DEPTH, N_DIR, DN_HEADS, DN_DK, DN_DV), 0.1),
        "state_lru": nrm(ks[3], (DEC_BATCH, DEPTH, N_DIR, LRU_W), 0.5),
        "c": nrm(ks[4], (DEC_BATCH, D_MODEL), 1.0),
        "c_ctx": nrm(ks[5], (D_MODEL,), 1.0),
        "norm1_g": gain(ks[6], (DEPTH, D_MODEL)),
        "w_mod": nrm(ks[7], (DEPTH, D_MODEL, 6 * D_MODEL), 0.5 * D_MODEL ** -0.5),
        "b_mod": nrm(ks[8], (DEPTH, 6 * D_MODEL), 0.01),
        "w_in": nrm(ks[9], (DEPTH, D_MODEL, IN_COLS), D_MODEL ** -0.5),
        "conv_qkv": nrm(ks[11], (DEPTH, CONV_W, 2 * DN_QK + DN_V), CONV_W ** -0.5),
        "dn_a_log": jnp.log(jax.random.uniform(ks[12], (DEPTH, N_DIR, DN_HEADS), f32, 1.0, 16.0)),
        "dn_dt_bias": dt + jnp.log(-jnp.expm1(-dt)),
        "dn_norm_g": gain(ks[13], (DEPTH, DN_DV)),
        "w_dn_out": nrm(ks[14], (DEPTH, DN_V, D_MODEL), DN_V ** -0.5),
        "conv_lru_w": nrm(ks[15], (DEPTH, CONV_W, LRU_W), CONV_W ** -0.5),
        "conv_lru_b": nrm(ks[16], (DEPTH, LRU_W), 0.01),
        "lru_wa": nrm(ks[17], (DEPTH, N_DIR, LRU_BLOCKS, LRU_BS, LRU_BS), LRU_BS ** -0.5),
        "lru_ba": nrm(ks[18], (DEPTH, N_DIR, LRU_W), 0.01),
        "lru_wx": nrm(ks[20], (DEPTH, N_DIR, LRU_BLOCKS, LRU_BS, LRU_BS), LRU_BS ** -0.5),
        "lru_bx": nrm(ks[21], (DEPTH, N_DIR, LRU_W), 0.01),
        "lru_lambda": jnp.log(a_lru) - jnp.log1p(-a_lru),
        "w_lru_out": nrm(ks[22], (DEPTH, LRU_W, D_MODEL), LRU_W ** -0.5),
        "w_o": nrm(ks[23], (DEPTH, D_MODEL, D_MODEL), D_MODEL ** -0.5),
        "norm2_g": gain(ks[24], (DEPTH, D_MODEL)),
        "w_router": nrm(ks[25], (DEPTH, D_MODEL, N_EXPERTS), D_MODEL ** -0.5),
        "w_gate": nrm(ks[26], (DEPTH, N_EXPERTS, D_MODEL, EXPERT_FF), D_MODEL ** -0.5),
        "w_up": nrm(ks[27], (DEPTH, N_EXPERTS, D_MODEL, EXPERT_FF), D_MODEL ** -0.5),
        "w_down": nrm(ks[28], (DEPTH, N_EXPERTS, EXPERT_FF, D_MODEL), EXPERT_FF ** -0.5),
        "final_g": gain(ks[29], (D_MODEL,)),
    }


def reference(x_prompt, x_sample, state_delta, state_lru, c, c_ctx, norm1_g, w_mod, b_mod, w_in,
              conv_qkv, dn_a_log, dn_dt_bias, dn_norm_g, w_dn_out, conv_lru_w, conv_lru_b, lru_wa,
              lru_ba, lru_wx, lru_bx, lru_lambda, w_lru_out, w_o, norm2_g, w_router, w_gate, w_up,
              w_down, final_g):
    weights = (norm1_g, w_mod, b_mod, w_in, conv_qkv, dn_a_log, dn_dt_bias, dn_norm_g, w_dn_out,
               conv_lru_w, conv_lru_b, lru_wa, lru_ba, lru_wx, lru_bx, lru_lambda, w_lru_out, w_o,
               norm2_g, w_router, w_gate, w_up, w_down, final_g)
    bp = x_prompt.shape[0]
    sd0 = jnp.zeros((bp, DEPTH, N_DIR, DN_HEADS, DN_DK, DN_DV), jnp.float32)
    sl0 = jnp.zeros((bp, DEPTH, N_DIR, LRU_W), jnp.float32)
    y_prompt, new_state_delta, new_state_lru = _run_stack(x_prompt, c_ctx[None, :], sd0, sl0, *weights)
    xs = x_sample + _pos_embed_2d(x_sample.shape[1]).astype(x_sample.dtype)[None]
    y_sample, _, _ = _run_stack(xs, c, state_delta, state_lru, *weights)
    return (y_prompt, y_sample, new_state_delta, new_state_lru)
```

```python
import functools
import math

import jax
import jax.numpy as jnp
from jax import lax
from jax.experimental import pallas as pl
from jax.experimental.pallas import tpu as pltpu

F32 = jnp.float32
BF16 = jnp.bfloat16

EPS = 1e-6
CHUNK = 64
SUB = 8
CONV_LEFT = 2
CONV_W = 4
LRU_C = 8.0
N_DIR = 2
GRID_W = 64
LANES = 128
SUBLANES = 8
VMEM_LIMIT = 56 * 1024 * 1024


def _cparams(sem):
    return pltpu.CompilerParams(dimension_semantics=sem, vmem_limit_bytes=VMEM_LIMIT)


def _bdot(a, b):
    return jnp.dot(a.astype(BF16), b.astype(BF16), preferred_element_type=F32)


def _mod_kernel(c_ref, w_ref, b_ref, o_ref):
    c = c_ref[...]
    c = c * jax.nn.sigmoid(c)
    o_ref[...] = _bdot(c, w_ref[...]) + b_ref[...]


def _modulation(cond8, w_mod, b_mod):
    depth, d, n6 = w_mod.shape
    tn = 1536
    return pl.pallas_call(
        _mod_kernel,
        grid=(depth, n6 // tn),
        in_specs=[
            pl.BlockSpec((8, d), lambda l, j: (0, 0)),
            pl.BlockSpec((None, d, tn), lambda l, j: (l, 0, j)),
            pl.BlockSpec((None, 1, tn), lambda l, j: (l, 0, j)),
        ],
        out_specs=pl.BlockSpec((None, 8, tn), lambda l, j: (l, 0, j)),
        out_shape=jax.ShapeDtypeStruct((depth, 8, n6), F32),
        compiler_params=_cparams(("parallel", "parallel")),
        name="modulation",
    )(cond8, w_mod, b_mod.reshape(depth, 1, n6))


def _in_proj_kernel(x_ref, g_ref, sh_ref, sc_ref, w_ref, wba_ref, o_ref, oba_ref, hn_scr):
    @pl.when(pl.program_id(1) == 0)
    def _():
        x = x_ref[...]
        y = x * lax.rsqrt(jnp.mean(x * x, axis=-1, keepdims=True) + EPS) * g_ref[...]
        hn = (y * (1.0 + sc_ref[...]) + sh_ref[...]).astype(BF16)
        hn_scr[...] = hn
        oba_ref[...] = jnp.dot(hn, wba_ref[...], preferred_element_type=F32)

    o_ref[...] = jnp.dot(hn_scr[...], w_ref[...], preferred_element_type=F32)


def _in_proj(x, norm_g, mod4, w_main, w_ba, row_of_tile, tm):
    n, d = x.shape
    ncols = w_main.shape[1]
    tn = 1536
    return pl.pallas_call(
        _in_proj_kernel,
        grid=(n // tm, ncols // tn),
        in_specs=[
            pl.BlockSpec((tm, d), lambda i, j: (i, 0)),
            pl.BlockSpec((1, d), lambda i, j: (0, 0)),
            pl.BlockSpec((None, None, 1, d), lambda i, j: (row_of_tile(i), 0, 0, 0)),
            pl.BlockSpec((None, None, 1, d), lambda i, j: (row_of_tile(i), 1, 0, 0)),
            pl.BlockSpec((d, tn), lambda i, j: (0, j)),
            pl.BlockSpec((d, LANES), lambda i, j: (0, 0)),
        ],
        out_specs=[
            pl.BlockSpec((tm, tn), lambda i, j: (i, j)),
            pl.BlockSpec((tm, LANES), lambda i, j: (i, 0)),
        ],
        out_shape=[
            jax.ShapeDtypeStruct((n, ncols), F32),
            jax.ShapeDtypeStruct((n, LANES), F32),
        ],
        scratch_shapes=[pltpu.VMEM((tm, d), BF16)],
        compiler_params=_cparams(("parallel", "arbitrary")),
        name="in_proj",
    )(x, norm_g.reshape(1, d), mod4, mod4, w_main, w_ba)


def _conv_rows(x, w, row):
    t = x.shape[0]
    acc = x * w[CONV_LEFT:CONV_LEFT + 1, :]
    for j in range(CONV_W):
        off = j - CONV_LEFT
        if off == 0:
            continue
        xs = pltpu.roll(x, (-off) % t, axis=0)
        valid = (row + off >= 0) & (row + off < t)
        acc = acc + jnp.where(valid, xs, 0.0) * w[j:j + 1, :]
    return acc


def _softplus(x):
    return jnp.maximum(x, 0.0) + jnp.log1p(jnp.exp(-jnp.abs(x)))


def _bmm(a, b):
    return jnp.einsum('nij,njk->nik', a.astype(BF16), b.astype(BF16), preferred_element_type=F32)


def _bmm_nt(a, b, precision=None):
    return jnp.einsum('nid,njd->nij', a, b, preferred_element_type=F32, precision=precision)


def _unit_tri_inverse(a, ii, jj):
    def same(b):
        return (ii // b) == (jj // b)

    eye = (ii == jj).astype(F32)
    d1 = jnp.where(same(SUB), a, 0.0)
    d2 = _bmm(d1, d1)
    d4 = _bmm(d2, d2)
    x = eye - d1
    x = x + _bmm(x, d2)
    x = x + _bmm(x, d4)
    b = SUB
    while b < CHUNK:
        o = jnp.where(same(2 * b) & jnp.logical_not(same(b)), a, 0.0)
        x = x - _bmm(_bmm(x, o), x)
        b *= 2
    return x


def _gdn_kernel(*refs, t, has_s0, dk):
    if has_s0:
        (q_ref, k_ref, v_ref, z_ref, ba_ref, cq_ref, ck_ref, cv_ref, al_ref, dt_ref, ng_ref,
         oh_ref, s0_ref, o_ref, s_ref) = refs
    else:
        (q_ref, k_ref, v_ref, z_ref, ba_ref, cq_ref, ck_ref, cv_ref, al_ref, dt_ref, ng_ref,
         oh_ref, o_ref, s_ref) = refs
        s0_ref = None
    n = t // CHUNK
    row = lax.broadcasted_iota(jnp.int32, (t, LANES), 0)
    pos = row % CHUNK

    def conv_silu(x_ref, w_ref):
        y = _conv_rows(x_ref[...], w_ref[...], row)
        return y * jax.nn.sigmoid(y)

    def l2n(x):
        return x * lax.rsqrt(jnp.sum(x * x, axis=-1, keepdims=True) + EPS)

    q = l2n(conv_silu(q_ref, cq_ref)) * (dk ** -0.5)
    k = l2n(conv_silu(k_ref, ck_ref))
    v = conv_silu(v_ref, cv_ref)
    q3 = q.reshape(n, CHUNK, LANES).astype(BF16)
    k3 = k.reshape(n, CHUNK, LANES)
    v3 = v.reshape(n, CHUNK, LANES)
    k3b = k3.astype(BF16)
    gram = _bmm_nt(k3b, k3b)
    qk = _bmm_nt(q3, k3b)
    q3 = q.reshape(n, CHUNK, LANES)

    ba = ba_ref[...]
    al = al_ref[...]
    dtb = dt_ref[...]
    ii = lax.broadcasted_iota(jnp.int32, (CHUNK, CHUNK), 0)
    jj = lax.broadcasted_iota(jnp.int32, (CHUNK, CHUNK), 1)
    onehot0 = jnp.broadcast_to(oh_ref[...][None], (n, CHUNK, LANES))

    per_dir = []
    for d in range(N_DIR):
        beta = jax.nn.sigmoid(ba[:, d:d + 1])
        g = -jnp.exp(al[:, d:d + 1]) * _softplus(ba[:, 2 + d:3 + d] + dtb[:, d:d + 1])
        beta_b = jnp.broadcast_to(beta, (t, LANES))
        dec = jnp.broadcast_to(g, (t, LANES))
        step = 1
        while step < CHUNK:
            if d == 0:
                sh = pltpu.roll(dec, step, axis=0)
                dec = dec + jnp.where(pos >= step, sh, 0.0)
            else:
                sh = pltpu.roll(dec, t - step, axis=0)
                dec = dec + jnp.where(pos < CHUNK - step, sh, 0.0)
            step *= 2
        dec3 = dec.reshape(n, CHUNK, LANES)
        tot3 = jnp.broadcast_to(dec3[:, CHUNK - 1:CHUNK, :] if d == 0 else dec3[:, 0:1, :],
                                (n, CHUNK, LANES))
        beta3 = beta_b.reshape(n, CHUNK, LANES)
        dec_row = _bmm_nt(onehot0, dec3, precision=lax.Precision.HIGHEST)
        dec_col = dec3[:, :, :CHUNK]
        tri = (ii >= jj) if d == 0 else (ii <= jj)
        strict = (ii > jj) if d == 0 else (ii < jj)
        gamma = jnp.where(tri, jnp.exp(jnp.where(tri, dec_col - dec_row, 0.0)), 0.0)
        a = jnp.where(strict, gram * gamma, 0.0) * beta3[:, :, :CHUNK]
        x = _unit_tri_inverse(a, ii, jj)
        attn = jnp.where(tri, qk * gamma, 0.0)
        edec = jnp.exp(dec3)
        kbd = k3 * (beta3 * edec)
        vb = v3 * beta3
        qd = q3 * edec
        kd = k3 * jnp.exp(tot3 - dec3)
        cd = jnp.exp(tot3)
        per_dir.append((x.astype(BF16), attn.astype(BF16), kbd.astype(BF16), vb, qd.astype(BF16),
                        kd.astype(BF16), cd))

    states = []
    outs = []
    for d in range(N_DIR):
        if has_s0:
            states.append(s0_ref[d])
        else:
            states.append(jnp.zeros((LANES, LANES), F32))
        outs.append([None] * n)
    for it in range(n):
        for d in range(N_DIR):
            c = it if d == 0 else n - 1 - it
            x, attn, kbd, vb, qd, kd, cd = per_dir[d]
            s = states[d]
            sb = s.astype(BF16)
            r = jnp.dot(kbd[c], sb, preferred_element_type=F32)
            vn = jnp.dot(x[c], (vb[c] - r).astype(BF16), preferred_element_type=F32).astype(BF16)
            outs[d][c] = (jnp.dot(qd[c], sb, preferred_element_type=F32)
                          + jnp.dot(attn[c], vn, preferred_element_type=F32))
            states[d] = s * cd[c][0:1, :] + lax.dot_general(kd[c], vn, (((0,), (0,)), ((), ())),
                                                    preferred_element_type=F32)
    for d in range(N_DIR):
        s_ref[d] = states[d]
    o = jnp.concatenate([outs[0][c] + outs[1][c] for c in range(n)], axis=0)
    o = o * lax.rsqrt(jnp.mean(o * o, axis=-1, keepdims=True) + EPS) * ng_ref[...]
    z = z_ref[...]
    o_ref[...] = (o * (z * jax.nn.sigmoid(z))).astype(o_ref.dtype)


def _gdn(cols, bah, conv_qkv, al, dtb, norm_g, s0, row0, nb, t, heads, dk):
    n = cols.shape[0]
    blk0 = row0 // t
    has_s0 = s0 is not None
    col = lambda off: pl.BlockSpec((t, LANES), lambda b, h: (blk0 + b, off + h))
    cw = lambda off: pl.BlockSpec((CONV_W, LANES), lambda b, h: (0, off + h))
    in_specs = [
        col(0), col(heads), col(2 * heads), col(3 * heads),
        pl.BlockSpec((None, t, 4), lambda b, h: (h, blk0 + b, 0)),
        cw(0), cw(heads), cw(2 * heads),
        pl.BlockSpec((None, 1, N_DIR), lambda b, h: (h, 0, 0)),
        pl.BlockSpec((None, 1, N_DIR), lambda b, h: (h, 0, 0)),
        pl.BlockSpec((1, LANES), lambda b, h: (0, 0)),
        pl.BlockSpec((CHUNK, LANES), lambda b, h: (0, 0)),
    ]
    onehot0 = jnp.zeros((CHUNK, LANES), F32).at[:, 0].set(1.0)
    args = [cols, cols, cols, cols, bah, conv_qkv, conv_qkv, conv_qkv, al, dtb, norm_g.reshape(1, LANES),
            onehot0]
    if has_s0:
        in_specs.append(pl.BlockSpec((None, N_DIR, None, dk, LANES), lambda b, h: (b, 0, h, 0, 0)))
        args.append(s0)
    return pl.pallas_call(
        functools.partial(_gdn_kernel, t=t, has_s0=has_s0, dk=dk),
        grid=(nb, heads),
        in_specs=in_specs,
        out_specs=[
            pl.BlockSpec((t, LANES), lambda b, h: (b, h)),
            pl.BlockSpec((None, N_DIR, None, dk, LANES), lambda b, h: (b, 0, h, 0, 0)),
        ],
        out_shape=[
            jax.ShapeDtypeStruct((nb * t, heads * LANES), BF16),
            jax.ShapeDtypeStruct((nb, N_DIR, heads, dk, LANES), F32),
        ],
        compiler_params=_cparams(("parallel", "parallel")),
        name=f"gdn_t{t}",
    )(*args)


def _lru_kernel(*refs, t, has_h0):
    if has_h0:
        (lx_ref, ly_ref, cw_ref, cb_ref, wa_ref, wx_ref, ba_ref, bx_ref, lam_ref, h0_ref,
         y_ref, last_ref, a_scr, b_scr, h_scr) = refs
    else:
        (lx_ref, ly_ref, cw_ref, cb_ref, wa_ref, wx_ref, ba_ref, bx_ref, lam_ref,
         y_ref, last_ref, a_scr, b_scr, h_scr) = refs
        h0_ref = None
    nblk = t // SUBLANES
    row = lax.broadcasted_iota(jnp.int32, (t, LANES), 0)
    sub = lax.broadcasted_iota(jnp.int32, (nblk, SUBLANES, LANES), 1)
    u = _conv_rows(lx_ref[...], cw_ref[...], row) + cb_ref[...]
    ub = u.astype(BF16)
    for d in range(N_DIR):
        r = jax.nn.sigmoid(jnp.dot(ub, wa_ref[d], preferred_element_type=F32) + ba_ref[d:d + 1, :])
        i = jax.nn.sigmoid(jnp.dot(ub, wx_ref[d], preferred_element_type=F32) + bx_ref[d:d + 1, :])
        log_a = (LRU_C * r) * (-_softplus(-lam_ref[d:d + 1, :]))
        a = jnp.exp(log_a)
        b = jnp.sqrt(1.0 - a * a) * (i * u)
        a3 = a.reshape(nblk, SUBLANES, LANES)
        b3 = b.reshape(nblk, SUBLANES, LANES)
        step = 1
        while step < SUBLANES:
            if d == 0:
                a_s = pltpu.roll(a3, step, axis=1)
                b_s = pltpu.roll(b3, step, axis=1)
                m = sub >= step
            else:
                a_s = pltpu.roll(a3, SUBLANES - step, axis=1)
                b_s = pltpu.roll(b3, SUBLANES - step, axis=1)
                m = sub < SUBLANES - step
            b3 = b3 + a3 * jnp.where(m, b_s, 0.0)
            a3 = a3 * jnp.where(m, a_s, 1.0)
            step *= 2
        a_scr[d] = a3
        b_scr[d] = b3

    if has_h0:
        h_init = (jnp.broadcast_to(h0_ref[0:1, :], (SUBLANES, LANES)),
                  jnp.broadcast_to(h0_ref[1:2, :], (SUBLANES, LANES)))
    else:
        h_init = (jnp.zeros((SUBLANES, LANES), F32), jnp.zeros((SUBLANES, LANES), F32))

    def body(it, carry):
        hf, hb = carry
        kf = it
        kb = nblk - 1 - it
        new_f = b_scr[0, kf] + a_scr[0, kf] * hf
        new_b = b_scr[1, kb] + a_scr[1, kb] * hb
        h_scr[0, kf] = new_f
        h_scr[1, kb] = new_b
        hf = jnp.broadcast_to(new_f[SUBLANES - 1:SUBLANES, :], (SUBLANES, LANES))
        hb = jnp.broadcast_to(new_b[0:1, :], (SUBLANES, LANES))
        return hf, hb

    hf, hb = lax.fori_loop(0, nblk, body, h_init)
    last_ref[0:1, :] = hf[0:1, :]
    last_ref[1:2, :] = hb[0:1, :]
    rec = (h_scr[0] + h_scr[1]).reshape(t, LANES)
    y_ref[...] = (jax.nn.gelu(ly_ref[...]) * rec).astype(y_ref.dtype)


def _lru(cols, conv_w, conv_b, wa, wx, ba, bx, lam, h0, row0, nb, t, lx_blk, ly_blk):
    nblocks = wa.shape[1]
    w = nblocks * LANES
    blk0 = row0 // t
    has_h0 = h0 is not None
    in_specs = [
        pl.BlockSpec((t, LANES), lambda b, c: (blk0 + b, lx_blk + c)),
        pl.BlockSpec((t, LANES), lambda b, c: (blk0 + b, ly_blk + c)),
        pl.BlockSpec((CONV_W, LANES), lambda b, c: (0, c)),
        pl.BlockSpec((1, LANES), lambda b, c: (0, c)),
        pl.BlockSpec((N_DIR, None, LANES, LANES), lambda b, c: (0, c, 0, 0)),
        pl.BlockSpec((N_DIR, None, LANES, LANES), lambda b, c: (0, c, 0, 0)),
        pl.BlockSpec((N_DIR, LANES), lambda b, c: (0, c)),
        pl.BlockSpec((N_DIR, LANES), lambda b, c: (0, c)),
        pl.BlockSpec((N_DIR, LANES), lambda b, c: (0, c)),
    ]
    args = [cols, cols, conv_w, conv_b.reshape(1, w), wa, wx, ba, bx, lam]
    if has_h0:
        in_specs.append(pl.BlockSpec((None, N_DIR, LANES), lambda b, c: (b, 0, c)))
        args.append(h0)
    nblk = t // SUBLANES
    return pl.pallas_call(
        functools.partial(_lru_kernel, t=t, has_h0=has_h0),
        grid=(nb, nblocks),
        in_specs=in_specs,
        out_specs=[
            pl.BlockSpec((t, LANES), lambda b, c: (b, c)),
            pl.BlockSpec((None, N_DIR, LANES), lambda b, c: (b, 0, c)),
        ],
        out_shape=[
            jax.ShapeDtypeStruct((nb * t, w), BF16),
            jax.ShapeDtypeStruct((nb, N_DIR, w), F32),
        ],
        scratch_shapes=[pltpu.VMEM((N_DIR, nblk, SUBLANES, LANES), F32)] * 3,
        compiler_params=_cparams(("parallel", "parallel")),
        name=f"lru_t{t}",
    )(*args)


def _merge_kernel(oa_ref, ob_ref, ga_ref, gb_ref, x_ref, g1_ref, sh2_ref, sc2_ref, n2_ref,
                  wdn_ref, wlru_ref, wo_ref, wr_ref, x1_ref, hn_ref, aff_ref, *, n_experts):
    ya = jnp.dot(oa_ref[...], wdn_ref[...], preferred_element_type=F32)
    yb = jnp.dot(ob_ref[...], wlru_ref[...], preferred_element_type=F32)
    mix = jax.nn.sigmoid(ga_ref[...]) * ya + jax.nn.sigmoid(gb_ref[...]) * yb
    mix = jnp.dot(mix.astype(BF16), wo_ref[...], preferred_element_type=F32)
    x1 = x_ref[...] + g1_ref[...] * mix
    x1_ref[...] = x1
    y = x1 * lax.rsqrt(jnp.mean(x1 * x1, axis=-1, keepdims=True) + EPS) * n2_ref[...]
    hn = (y * (1.0 + sc2_ref[...]) + sh2_ref[...]).astype(BF16)
    hn_ref[...] = hn
    logits = jnp.dot(hn, wr_ref[...], preferred_element_type=F32)
    lane = lax.broadcasted_iota(jnp.int32, logits.shape, 1)
    logits = jnp.where(lane < n_experts, logits, -jnp.inf)
    e = jnp.exp(logits - jnp.max(logits, axis=-1, keepdims=True))
    aff_ref[...] = e / jnp.sum(e, axis=-1, keepdims=True)


def _merge(o_gdn, y_lru, cols, x, mod4, norm2_g, w_dn, w_lru, w_o, w_r, row_of_tile, tm,
           ga_blk, gb_blk, n_experts):
    n, d = x.shape
    dv = o_gdn.shape[1]
    w = y_lru.shape[1]
    modspec = lambda k: pl.BlockSpec((None, None, 1, d), lambda i: (row_of_tile(i), k, 0, 0))
    full = lambda a: pl.BlockSpec(a.shape, lambda i: (0,) * a.ndim)
    return pl.pallas_call(
        functools.partial(_merge_kernel, n_experts=n_experts),
        grid=(n // tm,),
        in_specs=[
            pl.BlockSpec((tm, dv), lambda i: (i, 0)),
            pl.BlockSpec((tm, w), lambda i: (i, 0)),
            pl.BlockSpec((tm, d), lambda i: (i, ga_blk)),
            pl.BlockSpec((tm, d), lambda i: (i, gb_blk)),
            pl.BlockSpec((tm, d), lambda i: (i, 0)),
            modspec(2), modspec(3), modspec(4),
            pl.BlockSpec((1, d), lambda i: (0, 0)),
            full(w_dn), full(w_lru), full(w_o), full(w_r),
        ],
        out_specs=[
            pl.BlockSpec((tm, d), lambda i: (i, 0)),
            pl.BlockSpec((tm, d), lambda i: (i, 0)),
            pl.BlockSpec((tm, LANES), lambda i: (i, 0)),
        ],
        out_shape=[
            jax.ShapeDtypeStruct((n, d), F32),
            jax.ShapeDtypeStruct((n, d), BF16),
            jax.ShapeDtypeStruct((n, LANES), F32),
        ],
        compiler_params=_cparams(("parallel",)),
        name="merge",
    )(o_gdn, y_lru, cols, cols, x, mod4, mod4, mod4, norm2_g.reshape(1, d), w_dn, w_lru, w_o, w_r)


def _ffn_kernel(xe_ref, wg_ref, wu_ref, wd_ref, gv_ref, o_ref):
    f = pl.program_id(1)
    x = xe_ref[...]
    g = jnp.dot(x, wg_ref[...].astype(BF16), preferred_element_type=F32)
    u = jnp.dot(x, wu_ref[...].astype(BF16), preferred_element_type=F32)
    hid = ((g * jax.nn.sigmoid(g)) * u).astype(BF16)
    y = jnp.dot(hid, wd_ref[...].astype(BF16), preferred_element_type=F32)

    @pl.when(f == 0)
    def _():
        o_ref[...] = y

    @pl.when(f > 0)
    def _():
        o_ref[...] += y

    @pl.when(f == pl.num_programs(1) - 1)
    def _():
        o_ref[...] = o_ref[...] * gv_ref[...]


def _ffn(xe, gval, w_gate, w_up, w_down, tf):
    e, r, d = xe.shape
    ff = w_gate.shape[2]
    return pl.pallas_call(
        _ffn_kernel,
        grid=(e, ff // tf),
        in_specs=[
            pl.BlockSpec((None, r, d), lambda i, f: (i, 0, 0)),
            pl.BlockSpec((None, d, tf), lambda i, f: (i, 0, f)),
            pl.BlockSpec((None, d, tf), lambda i, f: (i, 0, f)),
            pl.BlockSpec((None, tf, d), lambda i, f: (i, f, 0)),
            pl.BlockSpec((None, r, 1), lambda i, f: (i, 0, 0)),
        ],
        out_specs=pl.BlockSpec((None, r, d), lambda i, f: (i, 0, 0)),
        out_shape=jax.ShapeDtypeStruct((e, r, d), F32),
        compiler_params=_cparams(("parallel", "arbitrary")),
        name="expert_ffn",
    )(xe, w_gate, w_up, w_down, gval)


def _final_norm_kernel(x_ref, g_ref, o_ref):
    x = x_ref[...]
    o_ref[...] = x * lax.rsqrt(jnp.mean(x * x, axis=-1, keepdims=True) + EPS) * g_ref[...]


def _final_norm(x, g, tm):
    n, d = x.shape
    return pl.pallas_call(
        _final_norm_kernel,
        grid=(n // tm,),
        in_specs=[pl.BlockSpec((tm, d), lambda i: (i, 0)), pl.BlockSpec((1, d), lambda i: (0, 0))],
        out_specs=pl.BlockSpec((tm, d), lambda i: (i, 0)),
        out_shape=jax.ShapeDtypeStruct((n, d), F32),
        compiler_params=_cparams(("parallel",)),
        name="final_norm",
    )(x, g.reshape(1, d))


def _pos_embed_2d(n_tokens, d_model):
    rows = n_tokens // GRID_W
    r = jnp.broadcast_to(jnp.arange(rows, dtype=F32)[:, None], (rows, GRID_W)).reshape(-1)
    col = jnp.broadcast_to(jnp.arange(GRID_W, dtype=F32)[None, :], (rows, GRID_W)).reshape(-1)
    quarter = d_model // 4
    freq = jnp.exp(-math.log(10000.0) * jnp.arange(quarter, dtype=F32) / quarter)
    ar = r[:, None] * freq
    ac = col[:, None] * freq
    return jnp.concatenate([jnp.sin(ar), jnp.cos(ar), jnp.sin(ac), jnp.cos(ac)], axis=-1)


def kernel(x_prompt, x_sample, state_delta, state_lru, c, c_ctx, norm1_g, w_mod, b_mod, w_in, conv_qkv, dn_a_log, dn_dt_bias, dn_norm_g, w_dn_out, conv_lru_w, conv_lru_b, lru_wa, lru_ba, lru_wx, lru_bx, lru_lambda, w_lru_out, w_o, norm2_g, w_router, w_gate, w_up, w_down, final_g):
    bp, tp, d = x_prompt.shape
    bs, ts, _ = x_sample.shape
    depth = w_in.shape[0]
    heads, dk, dv = state_delta.shape[3:]
    qk = heads * dk
    vw = heads * dv
    lru_w = state_lru.shape[-1]
    n_experts = w_router.shape[-1]
    n_p, n_s = bp * tp, bs * ts
    n = n_p + n_s
    cap_p = 2 * n_p // n_experts
    cap_s = 2 * n_s // n_experts
    assert dk == LANES and dv == LANES and n_p % ts == 0 and ts % tp == 0

    xs = x_sample + _pos_embed_2d(ts, d)[None]
    x = jnp.concatenate([x_prompt.reshape(n_p, d), xs.reshape(n_s, d)], axis=0)

    cond8 = jnp.zeros((8, d), F32).at[0].set(c_ctx).at[1:1 + bs].set(c)
    mod = _modulation(cond8, w_mod, b_mod)
    mod = mod.reshape(depth, 8, 6, 1, d)

    n_small = 2 * N_DIR * heads
    c0 = 2 * qk + 2 * vw
    w_main = jnp.concatenate([w_in[:, :, :c0], w_in[:, :, c0 + n_small:]], axis=2).astype(BF16)
    w_ba = jnp.pad(w_in[:, :, c0:c0 + n_small], ((0, 0), (0, 0), (0, LANES - n_small))).astype(BF16)
    lx_blk = c0 // LANES
    ly_blk = lx_blk + lru_w // LANES
    ga_blk = (c0 + 2 * lru_w) // d
    gb_blk = ga_blk + 1
    w_dn_b = w_dn_out.astype(BF16)
    w_lru_b = w_lru_out.astype(BF16)
    w_o_b = w_o.astype(BF16)
    w_r_b = jnp.pad(w_router, ((0, 0), (0, 0), (0, LANES - n_experts))).astype(BF16)
    wa_b = lru_wa.astype(BF16)
    wx_b = lru_wx.astype(BF16)
    al_h = jnp.transpose(dn_a_log, (0, 2, 1))[:, :, None, :]
    dt_h = jnp.transpose(dn_dt_bias, (0, 2, 1))[:, :, None, :]

    tm_in = ts
    tiles_p_in = n_p // tm_in
    row_in = lambda i: jnp.where(i < tiles_p_in, 0, i - tiles_p_in + 1)
    tm_mg = 512
    tiles_p_mg = n_p // tm_mg
    per_seq = ts // tm_mg
    row_mg = lambda i: jnp.where(i < tiles_p_mg, 0, (i - tiles_p_mg) // per_seq + 1)

    sd_out, sl_out = [], []
    for l in range(depth):
        cols, ba = _in_proj(x, norm1_g[l], mod[l], w_main[l], w_ba[l], row_in, tm_in)
        bah = ba[:, :n_small].reshape(n, 2, N_DIR, heads)
        bah = jnp.transpose(bah, (3, 0, 1, 2)).reshape(heads, n, 2 * N_DIR)
        o_p, sd_p = _gdn(cols, bah, conv_qkv[l], al_h[l], dt_h[l], dn_norm_g[l], None,
                         0, bp, tp, heads, dk)
        o_s, _ = _gdn(cols, bah, conv_qkv[l], al_h[l], dt_h[l], dn_norm_g[l], state_delta[:, l],
                      n_p, bs, ts, heads, dk)
        y_p, sl_p = _lru(cols, conv_lru_w[l], conv_lru_b[l], wa_b[l], wx_b[l], lru_ba[l], lru_bx[l],
                         lru_lambda[l], None, 0, bp, tp, lx_blk, ly_blk)
        y_s, _ = _lru(cols, conv_lru_w[l], conv_lru_b[l], wa_b[l], wx_b[l], lru_ba[l], lru_bx[l],
                      lru_lambda[l], state_lru[:, l], n_p, bs, ts, lx_blk, ly_blk)
        o_gdn = jnp.concatenate([o_p, o_s], axis=0)
        y_lru = jnp.concatenate([y_p, y_s], axis=0)
        x1, hn2, aff = _merge(o_gdn, y_lru, cols, x, mod[l], norm2_g[l], w_dn_b[l], w_lru_b[l],
                              w_o_b[l], w_r_b[l], row_mg, tm_mg, ga_blk, gb_blk, n_experts)
        aff = aff[:, :n_experts]
        gv_p, idx_p = lax.top_k(aff[:n_p].T, cap_p)
        gv_s, idx_s = lax.top_k(aff[n_p:].T, cap_s)
        idx_s = idx_s + n_p
        idx = jnp.concatenate([idx_p, idx_s], axis=1)
        gv = jnp.concatenate([gv_p, gv_s], axis=1)
        xe = hn2[idx]
        ye = _ffn(xe, gv[..., None], w_gate[l], w_up[l], w_down[l], 512)
        moe = jnp.zeros((n, d), F32).at[idx.reshape(-1)].add(ye.reshape(-1, d))
        g2 = jnp.concatenate([jnp.broadcast_to(mod[l, 0, 5], (n_p, d)),
                              jnp.repeat(mod[l, 1:1 + bs, 5, 0], ts, axis=0)], axis=0)
        x = x1 + g2 * moe
        sd_out.append(sd_p)
        sl_out.append(sl_p)

    y = _final_norm(x, final_g, 1024)
    y_prompt = y[:n_p].reshape(bp, tp, d)
    y_sample = y[n_p:].reshape(bs, ts, d)
    return (y_prompt, y_sample, jnp.stack(sd_out, axis=1), jnp.stack(sl_out, axis=1))
```

```python
import functools
import math

import jax
import jax.numpy as jnp
from jax import lax
from jax.experimental import pallas as pl
from jax.experimental.pallas import tpu as pltpu

F32 = jnp.float32
BF16 = jnp.bfloat16

EPS = 1e-6
CHUNK = 64
SUB = 8
CONV_LEFT = 2
CONV_W = 4
LRU_C = 8.0
N_DIR = 2
GRID_W = 64
LANES = 128
SUBLANES = 8
VMEM_LIMIT = 56 * 1024 * 1024


def _cparams(sem):
    return pltpu.CompilerParams(dimension_semantics=sem, vmem_limit_bytes=VMEM_LIMIT)


def _bdot(a, b):
    return jnp.dot(a.astype(BF16), b.astype(BF16), preferred_element_type=F32)


def _mod_kernel(c_ref, w_ref, b_ref, o_ref):
    c = c_ref[...]
    c = c * jax.nn.sigmoid(c)
    o_ref[...] = _bdot(c, w_ref[...]) + b_ref[...]


def _modulation(cond8, w_mod, b_mod):
    depth, d, n6 = w_mod.shape
    tn = 1536
    return pl.pallas_call(
        _mod_kernel,
        grid=(depth, n6 // tn),
        in_specs=[
            pl.BlockSpec((8, d), lambda l, j: (0, 0)),
            pl.BlockSpec((None, d, tn), lambda l, j: (l, 0, j)),
            pl.BlockSpec((None, 1, tn), lambda l, j: (l, 0, j)),
        ],
        out_specs=pl.BlockSpec((None, 8, tn), lambda l, j: (l, 0, j)),
        out_shape=jax.ShapeDtypeStruct((depth, 8, n6), F32),
        compiler_params=_cparams(("parallel", "parallel")),
        name="modulation",
    )(cond8, w_mod, b_mod.reshape(depth, 1, n6))


def _in_proj_kernel(x_ref, g_ref, sh_ref, sc_ref, w_ref, wba_ref, o_ref, oba_ref, hn_scr):
    @pl.when(pl.program_id(1) == 0)
    def _():
        x = x_ref[...]
        y = x * lax.rsqrt(jnp.mean(x * x, axis=-1, keepdims=True) + EPS) * g_ref[...]
        hn = (y * (1.0 + sc_ref[...]) + sh_ref[...]).astype(BF16)
        hn_scr[...] = hn
        oba_ref[...] = jnp.dot(hn, wba_ref[...], preferred_element_type=F32)

    o_ref[...] = jnp.dot(hn_scr[...], w_ref[...], preferred_element_type=F32)


def _in_proj(x, norm_g, mod4, w_main, w_ba, row_of_tile, tm):
    n, d = x.shape
    ncols = w_main.shape[1]
    tn = 1536
    return pl.pallas_call(
        _in_proj_kernel,
        grid=(n // tm, ncols // tn),
        in_specs=[
            pl.BlockSpec((tm, d), lambda i, j: (i, 0)),
            pl.BlockSpec((1, d), lambda i, j: (0, 0)),
            pl.BlockSpec((None, None, 1, d), lambda i, j: (row_of_tile(i), 0, 0, 0)),
            pl.BlockSpec((None, None, 1, d), lambda i, j: (row_of_tile(i), 1, 0, 0)),
            pl.BlockSpec((d, tn), lambda i, j: (0, j)),
            pl.BlockSpec((d, LANES), lambda i, j: (0, 0)),
        ],
        out_specs=[
            pl.BlockSpec((tm, tn), lambda i, j: (i, j)),
            pl.BlockSpec((tm, LANES), lambda i, j: (i, 0)),
        ],
        out_shape=[
            jax.ShapeDtypeStruct((n, ncols), F32),
            jax.ShapeDtypeStruct((n, LANES), F32),
        ],
        scratch_shapes=[pltpu.VMEM((tm, d), BF16)],
        compiler_params=_cparams(("parallel", "arbitrary")),
        name="in_proj",
    )(x, norm_g.reshape(1, d), mod4, mod4, w_main, w_ba)


def _conv_rows(x, w, row):
    t = x.shape[0]
    acc = x * w[CONV_LEFT:CONV_LEFT + 1, :]
    for j in range(CONV_W):
        off = j - CONV_LEFT
        if off == 0:
            continue
        xs = pltpu.roll(x, (-off) % t, axis=0)
        valid = (row + off >= 0) & (row + off < t)
        acc = acc + jnp.where(valid, xs, 0.0) * w[j:j + 1, :]
    return acc


def _softplus(x):
    return jnp.maximum(x, 0.0) + jnp.log1p(jnp.exp(-jnp.abs(x)))


def _bmm(a, b):
    return jnp.einsum('nij,njk->nik', a.astype(BF16), b.astype(BF16), preferred_element_type=F32)


def _bmm_nt(a, b, precision=None):
    return jnp.einsum('nid,njd->nij', a, b, preferred_element_type=F32, precision=precision)


def _unit_tri_inverse(a, ii, jj):
    def same(b):
        return (ii // b) == (jj // b)

    eye = (ii == jj).astype(F32)
    d1 = jnp.where(same(SUB), a, 0.0)
    d2 = _bmm(d1, d1)
    d4 = _bmm(d2, d2)
    x = eye - d1
    x = x + _bmm(x, d2)
    x = x + _bmm(x, d4)
    b = SUB
    while b < CHUNK:
        o = jnp.where(same(2 * b) & jnp.logical_not(same(b)), a, 0.0)
        x = x - _bmm(_bmm(x, o), x)
        b *= 2
    return x


def _gdn_kernel(*refs, t, has_s0, dk):
    if has_s0:
        (q_ref, k_ref, v_ref, z_ref, ba_ref, cq_ref, ck_ref, cv_ref, al_ref, dt_ref, ng_ref,
         oh_ref, s0_ref, o_ref, s_ref) = refs
    else:
        (q_ref, k_ref, v_ref, z_ref, ba_ref, cq_ref, ck_ref, cv_ref, al_ref, dt_ref, ng_ref,
         oh_ref, o_ref, s_ref) = refs
        s0_ref = None
    n = t // CHUNK
    row = lax.broadcasted_iota(jnp.int32, (t, LANES), 0)
    pos = row % CHUNK

    def conv_silu(x_ref, w_ref):
        y = _conv_rows(x_ref[...], w_ref[...], row)
        return y * jax.nn.sigmoid(y)

    def l2n(x):
        return x * lax.rsqrt(jnp.sum(x * x, axis=-1, keepdims=True) + EPS)

    q = l2n(conv_silu(q_ref, cq_ref)) * (dk ** -0.5)
    k = l2n(conv_silu(k_ref, ck_ref))
    v = conv_silu(v_ref, cv_ref)
    q3 = q.reshape(n, CHUNK, LANES).astype(BF16)
    k3 = k.reshape(n, CHUNK, LANES)
    v3 = v.reshape(n, CHUNK, LANES)
    k3b = k3.astype(BF16)
    gram = _bmm_nt(k3b, k3b)
    qk = _bmm_nt(q3, k3b)
    q3 = q.reshape(n, CHUNK, LANES)

    ba = ba_ref[...]
    al = al_ref[...]
    dtb = dt_ref[...]
    ii = lax.broadcasted_iota(jnp.int32, (CHUNK, CHUNK), 0)
    jj = lax.broadcasted_iota(jnp.int32, (CHUNK, CHUNK), 1)
    onehot0 = jnp.broadcast_to(oh_ref[...][None], (n, CHUNK, LANES))

    per_dir = []
    for d in range(N_DIR):
        beta = jax.nn.sigmoid(ba[:, d:d + 1])
        g = -jnp.exp(al[:, d:d + 1]) * _softplus(ba[:, 2 + d:3 + d] + dtb[:, d:d + 1])
        beta_b = jnp.broadcast_to(beta, (t, LANES))
        dec = jnp.broadcast_to(g, (t, LANES))
        step = 1
        while step < CHUNK:
            if d == 0:
                sh = pltpu.roll(dec, step, axis=0)
                dec = dec + jnp.where(pos >= step, sh, 0.0)
            else:
                sh = pltpu.roll(dec, t - step, axis=0)
                dec = dec + jnp.where(pos < CHUNK - step, sh, 0.0)
            step *= 2
        dec3 = dec.reshape(n, CHUNK, LANES)
        tot3 = jnp.broadcast_to(dec3[:, CHUNK - 1:CHUNK, :] if d == 0 else dec3[:, 0:1, :],
                                (n, CHUNK, LANES))
        beta3 = beta_b.reshape(n, CHUNK, LANES)
        dec_row = _bmm_nt(onehot0, dec3, precision=lax.Precision.HIGHEST)
        dec_col = dec3[:, :, :CHUNK]
        tri = (ii >= jj) if d == 0 else (ii <= jj)
        strict = (ii > jj) if d == 0 else (ii < jj)
        gamma = jnp.where(tri, jnp.exp(jnp.where(tri, dec_col - dec_row, 0.0)), 0.0)
        a = jnp.where(strict, gram * gamma, 0.0) * beta3[:, :, :CHUNK]
        x = _unit_tri_inverse(a, ii, jj)
        attn = jnp.where(tri, qk * gamma, 0.0)
        edec = jnp.exp(dec3)
        kbd = k3 * (beta3 * edec)
        vb = v3 * beta3
        qd = q3 * edec
        kd = k3 * jnp.exp(tot3 - dec3)
        cd = jnp.exp(tot3)
        per_dir.append((x.astype(BF16), attn.astype(BF16), kbd.astype(BF16), vb, qd.astype(BF16),
                        kd.astype(BF16), cd))

    states = []
    outs = []
    for d in range(N_DIR):
        if has_s0:
            states.append(s0_ref[d])
        else:
            states.append(jnp.zeros((LANES, LANES), F32))
        outs.append([None] * n)
    for it in range(n):
        for d in range(N_DIR):
            c = it if d == 0 else n - 1 - it
            x, attn, kbd, vb, qd, kd, cd = per_dir[d]
            s = states[d]
            sb = s.astype(BF16)
            r = jnp.dot(kbd[c], sb, preferred_element_type=F32)
            vn = jnp.dot(x[c], (vb[c] - r).astype(BF16), preferred_element_type=F32).astype(BF16)
            outs[d][c] = (jnp.dot(qd[c], sb, preferred_element_type=F32)
                          + jnp.dot(attn[c], vn, preferred_element_type=F32))
            states[d] = s * cd[c][0:1, :] + lax.dot_general(kd[c], vn, (((0,), (0,)), ((), ())),
                                                    preferred_element_type=F32)
    for d in range(N_DIR):
        s_ref[d] = states[d]
    o = jnp.concatenate([outs[0][c] + outs[1][c] for c in range(n)], axis=0)
    o = o * lax.rsqrt(jnp.mean(o * o, axis=-1, keepdims=True) + EPS) * ng_ref[...]
    z = z_ref[...]
    o_ref[...] = (o * (z * jax.nn.sigmoid(z))).astype(o_ref.dtype)


def _gdn(cols, bah, conv_qkv, al, dtb, norm_g, s0, row0, nb, t, heads, dk):
    n = cols.shape[0]
    blk0 = row0 // t
    has_s0 = s0 is not None
    col = lambda off: pl.BlockSpec((t, LANES), lambda b, h: (blk0 + b, off + h))
    cw = lambda off: pl.BlockSpec((CONV_W, LANES), lambda b, h: (0, off + h))
    in_specs = [
        col(0), col(heads), col(2 * heads), col(3 * heads),
        pl.BlockSpec((None, t, 4), lambda b, h: (h, blk0 + b, 0)),
        cw(0), cw(heads), cw(2 * heads),
        pl.BlockSpec((None, 1, N_DIR), lambda b, h: (h, 0, 0)),
        pl.BlockSpec((None, 1, N_DIR), lambda b, h: (h, 0, 0)),
        pl.BlockSpec((1, LANES), lambda b, h: (0, 0)),
        pl.BlockSpec((CHUNK, LANES), lambda b, h: (0, 0)),
    ]
    onehot0 = jnp.zeros((CHUNK, LANES), F32).at[:, 0].set(1.0)
    args = [cols, cols, cols, cols, bah, conv_qkv, conv_qkv, conv_qkv, al, dtb, norm_g.reshape(1, LANES),
            onehot0]
    if has_s0:
        in_specs.append(pl.BlockSpec((None, N_DIR, None, dk, LANES), lambda b, h: (b, 0, h, 0, 0)))
        args.append(s0)
    return pl.pallas_call(
        functools.partial(_gdn_kernel, t=t, has_s0=has_s0, dk=dk),
        grid=(nb, heads),
        in_specs=in_specs,
        out_specs=[
            pl.BlockSpec((t, LANES), lambda b, h: (b, h)),
            pl.BlockSpec((None, N_DIR, None, dk, LANES), lambda b, h: (b, 0, h, 0, 0)),
        ],
        out_shape=[
            jax.ShapeDtypeStruct((nb * t, heads * LANES), BF16),
            jax.ShapeDtypeStruct((nb, N_DIR, heads, dk, LANES), F32),
        ],
        compiler_params=_cparams(("parallel", "parallel")),
        name=f"gdn_t{t}",
    )(*args)


def _lru_kernel(*refs, t, has_h0):
    if has_h0:
        (lx_ref, ly_ref, cw_ref, cb_ref, wa_ref, wx_ref, ba_ref, bx_ref, lam_ref, h0_ref,
         y_ref, last_ref, a_scr, b_scr, h_scr) = refs
    else:
        (lx_ref, ly_ref, cw_ref, cb_ref, wa_ref, wx_ref, ba_ref, bx_ref, lam_ref,
         y_ref, last_ref, a_scr, b_scr, h_scr) = refs
        h0_ref = None
    nblk = t // SUBLANES
    row = lax.broadcasted_iota(jnp.int32, (t, LANES), 0)
    sub = lax.broadcasted_iota(jnp.int32, (nblk, SUBLANES, LANES), 1)
    u = _conv_rows(lx_ref[...], cw_ref[...], row) + cb_ref[...]
    ub = u.astype(BF16)
    for d in range(N_DIR):
        r = jax.nn.sigmoid(jnp.dot(ub, wa_ref[d], preferred_element_type=F32) + ba_ref[d:d + 1, :])
        i = jax.nn.sigmoid(jnp.dot(ub, wx_ref[d], preferred_element_type=F32) + bx_ref[d:d + 1, :])
        log_a = (LRU_C * r) * (-_softplus(-lam_ref[d:d + 1, :]))
        a = jnp.exp(log_a)
        b = jnp.sqrt(1.0 - a * a) * (i * u)
        a3 = a.reshape(nblk, SUBLANES, LANES)
        b3 = b.reshape(nblk, SUBLANES, LANES)
        step = 1
        while step < SUBLANES:
            if d == 0:
                a_s = pltpu.roll(a3, step, axis=1)
                b_s = pltpu.roll(b3, step, axis=1)
                m = sub >= step
            else:
                a_s = pltpu.roll(a3, SUBLANES - step, axis=1)
                b_s = pltpu.roll(b3, SUBLANES - step, axis=1)
                m = sub < SUBLANES - step
            b3 = b3 + a3 * jnp.where(m, b_s, 0.0)
            a3 = a3 * jnp.where(m, a_s, 1.0)
            step *= 2
        a_scr[d] = a3
        b_scr[d] = b3

    if has_h0:
        h_init = (jnp.broadcast_to(h0_ref[0:1, :], (SUBLANES, LANES)),
                  jnp.broadcast_to(h0_ref[1:2, :], (SUBLANES, LANES)))
    else:
        h_init = (jnp.zeros((SUBLANES, LANES), F32), jnp.zeros((SUBLANES, LANES), F32))

    def body(it, carry):
        hf, hb = carry
        kf = it
        kb = nblk - 1 - it
        new_f = b_scr[0, kf] + a_scr[0, kf] * hf
        new_b = b_scr[1, kb] + a_scr[1, kb] * hb
        h_scr[0, kf] = new_f
        h_scr[1, kb] = new_b
        hf = jnp.broadcast_to(new_f[SUBLANES - 1:SUBLANES, :], (SUBLANES, LANES))
        hb = jnp.broadcast_to(new_b[0:1, :], (SUBLANES, LANES))
        return hf, hb

    hf, hb = lax.fori_loop(0, nblk, body, h_init)
    last_ref[0:1, :] = hf[0:1, :]
    last_ref[1:2, :] = hb[0:1, :]
    rec = (h_scr[0] + h_scr[1]).reshape(t, LANES)
    y_ref[...] = (jax.nn.gelu(ly_ref[...]) * rec).astype(y_ref.dtype)


def _lru(cols, conv_w, conv_b, wa, wx, ba, bx, lam, h0, row0, nb, t, lx_blk, ly_blk):
    nblocks = wa.shape[1]
    w = nblocks * LANES
    blk0 = row0 // t
    has_h0 = h0 is not None
    in_specs = [
        pl.BlockSpec((t, LANES), lambda b, c: (blk0 + b, lx_blk + c)),
        pl.BlockSpec((t, LANES), lambda b, c: (blk0 + b, ly_blk + c)),
        pl.BlockSpec((CONV_W, LANES), lambda b, c: (0, c)),
        pl.BlockSpec((1, LANES), lambda b, c: (0, c)),
        pl.BlockSpec((N_DIR, None, LANES, LANES), lambda b, c: (0, c, 0, 0)),
        pl.BlockSpec((N_DIR, None, LANES, LANES), lambda b, c: (0, c, 0, 0)),
        pl.BlockSpec((N_DIR, LANES), lambda b, c: (0, c)),
        pl.BlockSpec((N_DIR, LANES), lambda b, c: (0, c)),
        pl.BlockSpec((N_DIR, LANES), lambda b, c: (0, c)),
    ]
    args = [cols, cols, conv_w, conv_b.reshape(1, w), wa, wx, ba, bx, lam]
    if has_h0:
        in_specs.append(pl.BlockSpec((None, N_DIR, LANES), lambda b, c: (b, 0, c)))
        args.append(h0)
    nblk = t // SUBLANES
    return pl.pallas_call(
        functools.partial(_lru_kernel, t=t, has_h0=has_h0),
        grid=(nb, nblocks),
        in_specs=in_specs,
        out_specs=[
            pl.BlockSpec((t, LANES), lambda b, c: (b, c)),
            pl.BlockSpec((None, N_DIR, LANES), lambda b, c: (b, 0, c)),
        ],
        out_shape=[
            jax.ShapeDtypeStruct((nb * t, w), BF16),
            jax.ShapeDtypeStruct((nb, N_DIR, w), F32),
        ],
        scratch_shapes=[pltpu.VMEM((N_DIR, nblk, SUBLANES, LANES), F32)] * 3,
        compiler_params=_cparams(("parallel", "parallel")),
        name=f"lru_t{t}",
    )(*args)


def _merge_kernel(oa_ref, ob_ref, ga_ref, gb_ref, x_ref, g1_ref, sh2_ref, sc2_ref, n2_ref,
                  wdn_ref, wlru_ref, wo_ref, wr_ref, x1_ref, hn_ref, aff_ref, afft_ref, *, n_experts):
    ya = jnp.dot(oa_ref[...], wdn_ref[...], preferred_element_type=F32)
    yb = jnp.dot(ob_ref[...], wlru_ref[...], preferred_element_type=F32)
    mix = jax.nn.sigmoid(ga_ref[...]) * ya + jax.nn.sigmoid(gb_ref[...]) * yb
    mix = jnp.dot(mix.astype(BF16), wo_ref[...], preferred_element_type=F32)
    x1 = x_ref[...] + g1_ref[...] * mix
    x1_ref[...] = x1
    y = x1 * lax.rsqrt(jnp.mean(x1 * x1, axis=-1, keepdims=True) + EPS) * n2_ref[...]
    hn = y * (1.0 + sc2_ref[...]) + sh2_ref[...]
    hn_ref[...] = hn
    logits = jnp.dot(hn.astype(BF16), wr_ref[...], preferred_element_type=F32)
    lane = lax.broadcasted_iota(jnp.int32, logits.shape, 1)
    logits = jnp.where(lane < n_experts, logits, -jnp.inf)
    e = jnp.exp(logits - jnp.max(logits, axis=-1, keepdims=True))
    aff = e / jnp.sum(e, axis=-1, keepdims=True)
    aff_ref[...] = aff
    afft_ref[...] = aff.T[:n_experts, :]


def _merge(o_gdn, y_lru, cols, x, mod4, norm2_g, w_dn, w_lru, w_o, w_r, row_of_tile, tm,
           ga_blk, gb_blk, n_experts):
    n, d = x.shape
    dv = o_gdn.shape[1]
    w = y_lru.shape[1]
    modspec = lambda k: pl.BlockSpec((None, None, 1, d), lambda i: (row_of_tile(i), k, 0, 0))
    full = lambda a: pl.BlockSpec(a.shape, lambda i: (0,) * a.ndim)
    return pl.pallas_call(
        functools.partial(_merge_kernel, n_experts=n_experts),
        grid=(n // tm,),
        in_specs=[
            pl.BlockSpec((tm, dv), lambda i: (i, 0)),
            pl.BlockSpec((tm, w), lambda i: (i, 0)),
            pl.BlockSpec((tm, d), lambda i: (i, ga_blk)),
            pl.BlockSpec((tm, d), lambda i: (i, gb_blk)),
            pl.BlockSpec((tm, d), lambda i: (i, 0)),
            modspec(2), modspec(3), modspec(4),
            pl.BlockSpec((1, d), lambda i: (0, 0)),
            full(w_dn), full(w_lru), full(w_o), full(w_r),
        ],
        out_specs=[
            pl.BlockSpec((tm, d), lambda i: (i, 0)),
            pl.BlockSpec((tm, d), lambda i: (i, 0)),
            pl.BlockSpec((tm, LANES), lambda i: (i, 0)),
            pl.BlockSpec((n_experts, tm), lambda i: (0, i)),
        ],
        out_shape=[
            jax.ShapeDtypeStruct((n, d), F32),
            jax.ShapeDtypeStruct((n, d), F32),
            jax.ShapeDtypeStruct((n, LANES), F32),
            jax.ShapeDtypeStruct((n_experts, n), F32),
        ],
        compiler_params=_cparams(("parallel",)),
        name="merge",
    )(o_gdn, y_lru, cols, cols, x, mod4, mod4, mod4, norm2_g.reshape(1, d), w_dn, w_lru, w_o, w_r)


def _ffn_kernel(xe_ref, wg_ref, wu_ref, wd_ref, o_ref, xb_scr):
    f = pl.program_id(1)

    @pl.when(f == 0)
    def _():
        xb_scr[...] = xe_ref[...].astype(BF16)

    x = xb_scr[...]
    g = jnp.dot(x, wg_ref[...].astype(BF16), preferred_element_type=F32)
    u = jnp.dot(x, wu_ref[...].astype(BF16), preferred_element_type=F32)
    hid = ((g * jax.nn.sigmoid(g)) * u).astype(BF16)
    y = jnp.dot(hid, wd_ref[...].astype(BF16), preferred_element_type=F32)

    @pl.when(f == 0)
    def _():
        o_ref[...] = y

    @pl.when(f > 0)
    def _():
        o_ref[...] += y


def _ffn(xe, w_gate, w_up, w_down, tf):
    e, r, d = xe.shape
    ff = w_gate.shape[2]
    return pl.pallas_call(
        _ffn_kernel,
        grid=(e, ff // tf),
        in_specs=[
            pl.BlockSpec((None, r, d), lambda i, f: (i, 0, 0)),
            pl.BlockSpec((None, d, tf), lambda i, f: (i, 0, f)),
            pl.BlockSpec((None, d, tf), lambda i, f: (i, 0, f)),
            pl.BlockSpec((None, tf, d), lambda i, f: (i, f, 0)),
        ],
        out_specs=pl.BlockSpec((None, r, d), lambda i, f: (i, 0, 0)),
        out_shape=jax.ShapeDtypeStruct((e, r, d), F32),
        scratch_shapes=[pltpu.VMEM((r, d), BF16)],
        compiler_params=_cparams(("parallel", "arbitrary")),
        name="expert_ffn",
    )(xe, w_gate, w_up, w_down)


ROUTE_TB = 128
ROUTE_ROWS = 24


def _thr_kernel(afft_ref, o_ref, *, groups):
    n_e = afft_ref.shape[0]
    sub = lax.broadcasted_iota(jnp.int32, (n_e, LANES), 0)
    lane = lax.broadcasted_iota(jnp.int32, (n_e, LANES), 1)
    rows = []
    for lo, hi, cap in groups:
        bits = lax.bitcast_convert_type(afft_ref[:, lo:hi], jnp.int32)

        def body(i, cur, bits=bits, cap=cap):
            cand = cur | lax.shift_left(jnp.int32(1), 30 - i)
            cnt = jnp.sum((bits >= cand).astype(F32), axis=1, keepdims=True)
            return jnp.where(cnt >= cap, cand, cur)

        thr_bits = lax.fori_loop(0, 31, body, jnp.zeros((n_e, 1), jnp.int32))
        cnt_gt = jnp.sum((bits > thr_bits).astype(F32), axis=1, keepdims=True)
        need = cap - cnt_gt
        thr = lax.bitcast_convert_type(thr_bits, F32)
        for col in (thr, need):
            m = jnp.where(sub == lane, jnp.broadcast_to(col, (n_e, LANES)), 0.0)
            rows.append(jnp.sum(m, axis=0, keepdims=True))
    rows.append(jnp.zeros((SUBLANES - len(rows), LANES), F32))
    o_ref[...] = jnp.concatenate(rows, axis=0)


def _thresholds(afft, groups):
    n_e, n = afft.shape
    return pl.pallas_call(
        functools.partial(_thr_kernel, groups=groups),
        grid=(1,),
        in_specs=[pl.BlockSpec((n_e, n), lambda i: (0, 0))],
        out_specs=pl.BlockSpec((SUBLANES, LANES), lambda i: (0, 0)),
        out_shape=jax.ShapeDtypeStruct((SUBLANES, LANES), F32),
        compiler_params=_cparams(("arbitrary",)),
        name="route_thresholds",
    )(afft)


def _route_kernel(aff_ref, tn_ref, ls_ref, hn_hbm, wsel_ref, rl_ref, xe_hbm,
                  run_eq, run_sel, stage, rec_smem, sem_s, sem_g, *, nblk_p, cap_p, n_e):
    tb = ROUTE_TB
    j = pl.program_id(0)

    @pl.when(j == 0)
    def _():
        run_eq[...] = jnp.zeros_like(run_eq)
        run_sel[...] = jnp.zeros_like(run_sel)
        stage[...] = jnp.zeros_like(stage)

    @pl.when(j == nblk_p)
    def _():
        run_eq[...] = jnp.zeros_like(run_eq)
        run_sel[...] = jnp.full_like(run_sel, float(cap_p))

    g = (j >= nblk_p).astype(jnp.int32)
    thr = tn_ref[pl.ds(2 * g, 1), :]
    need = tn_ref[pl.ds(2 * g + 1, 1), :]
    a = aff_ref[...]
    lane = lax.broadcasted_iota(jnp.int32, (tb, LANES), 1)
    valid = lane < n_e
    gt = (a > thr) & valid
    eq = (a == thr) & valid
    eqf = eq.astype(F32)
    ls = ls_ref[...]
    eq_rank = run_eq[...] + jnp.dot(ls, eqf.astype(BF16), preferred_element_type=F32)
    sel = gt | (eq & (eq_rank < need))
    self_ = sel.astype(F32)
    pos = jnp.dot(ls, self_.astype(BF16), preferred_element_type=F32)
    cnt = jnp.sum(self_, axis=0, keepdims=True)
    off = run_sel[...]
    run_eq[...] = run_eq[...] + jnp.sum(eqf, axis=0, keepdims=True)
    run_sel[...] = off + cnt
    wsel_ref[...] = jnp.where(sel, a, 0.0)

    tok = (lax.broadcasted_iota(jnp.int32, (tb, LANES), 0) + j * tb).astype(F32)
    lanef = lane.astype(F32)
    for e in range(n_e):
        pe = jnp.broadcast_to(pos[:, e:e + 1], (tb, LANES))
        se = jnp.broadcast_to(self_[:, e:e + 1], (tb, LANES))
        hit = (pe == lanef) & (se > 0.0)
        stage[e:e + 1, :] = jnp.sum(jnp.where(hit, tok, 0.0), axis=0, keepdims=True).astype(jnp.int32)
    stage[n_e:n_e + 1, :] = cnt.astype(jnp.int32)
    stage[n_e + 1:n_e + 2, :] = off.astype(jnp.int32)
    rl_ref[...] = stage[...]

    to_smem = pltpu.make_async_copy(stage, rec_smem, sem_s)
    to_smem.start()
    to_smem.wait()

    def row_copy(e, q):
        t = rec_smem[e, q]
        o = rec_smem[n_e + 1, e]
        return pltpu.make_async_copy(hn_hbm.at[pl.ds(t, 1)], xe_hbm.at[e, pl.ds(o + q, 1)], sem_g)

    for e in range(n_e):
        c = rec_smem[n_e, e]

        def issue(q, carry, e=e):
            row_copy(e, q).start()
            return carry

        lax.fori_loop(0, c, issue, 0)
    for e in range(n_e):
        c = rec_smem[n_e, e]

        def drain(q, carry, e=e):
            row_copy(e, q).wait()
            return carry

        lax.fori_loop(0, c, drain, 0)


def _route(aff, thr_need, hn, n_p, cap_p, cap_s, n_e):
    n, d = hn.shape
    tb = ROUTE_TB
    nblk = n // tb
    ls = jnp.tril(jnp.ones((tb, tb), F32), -1).astype(BF16)
    return pl.pallas_call(
        functools.partial(_route_kernel, nblk_p=n_p // tb, cap_p=cap_p, n_e=n_e),
        grid=(nblk,),
        in_specs=[
            pl.BlockSpec((tb, LANES), lambda j: (j, 0)),
            pl.BlockSpec((SUBLANES, LANES), lambda j: (0, 0)),
            pl.BlockSpec((tb, tb), lambda j: (0, 0)),
            pl.BlockSpec(memory_space=pl.ANY),
        ],
        out_specs=[
            pl.BlockSpec((tb, LANES), lambda j: (j, 0)),
            pl.BlockSpec((None, ROUTE_ROWS, LANES), lambda j: (j, 0, 0)),
            pl.BlockSpec(memory_space=pl.ANY),
        ],
        out_shape=[
            jax.ShapeDtypeStruct((n, LANES), F32),
            jax.ShapeDtypeStruct((nblk, ROUTE_ROWS, LANES), jnp.int32),
            jax.ShapeDtypeStruct((n_e, cap_p + cap_s, d), F32),
        ],
        scratch_shapes=[
            pltpu.VMEM((1, LANES), F32),
            pltpu.VMEM((1, LANES), F32),
            pltpu.VMEM((ROUTE_ROWS, LANES), jnp.int32),
            pltpu.SMEM((ROUTE_ROWS, LANES), jnp.int32),
            pltpu.SemaphoreType.DMA(()),
            pltpu.SemaphoreType.DMA(()),
        ],
        compiler_params=_cparams(("arbitrary",)),
        name="route_gather",
    )(aff, thr_need, ls, hn)


def _combine_kernel(rl_hbm, y_hbm, wsel_ref, x1_ref, g2_ref, fg_ref, o_ref,
                    z, rec_smem, sem_s, sem_g, *, n_e, final):
    tb = ROUTE_TB
    j = pl.program_id(0)

    @pl.when(j == 0)
    def _():
        z[...] = jnp.zeros_like(z)

    to_smem = pltpu.make_async_copy(rl_hbm.at[j], rec_smem, sem_s)
    to_smem.start()
    to_smem.wait()

    def row_copy(e, q):
        t = rec_smem[e, q] - j * tb
        o = rec_smem[n_e + 1, e]
        return pltpu.make_async_copy(y_hbm.at[e, pl.ds(o + q, 1)], z.at[e, pl.ds(t, 1)], sem_g)

    for e in range(n_e):
        c = rec_smem[n_e, e]

        def issue(q, carry, e=e):
            row_copy(e, q).start()
            return carry

        lax.fori_loop(0, c, issue, 0)
    for e in range(n_e):
        c = rec_smem[n_e, e]

        def drain(q, carry, e=e):
            row_copy(e, q).wait()
            return carry

        lax.fori_loop(0, c, drain, 0)

    w = wsel_ref[...]
    acc = jnp.zeros(x1_ref.shape, F32)
    for e in range(n_e):
        we = w[:, e:e + 1]
        acc = acc + jnp.where(we > 0.0, we * z[e], 0.0)
    x2 = x1_ref[...] + g2_ref[...] * acc
    if final:
        x2 = x2 * lax.rsqrt(jnp.mean(x2 * x2, axis=-1, keepdims=True) + EPS) * fg_ref[...]
    o_ref[...] = x2


def _combine(rl, ye, wsel, x1, mod4, final_g, row_of_tile, n_e, final):
    n, d = x1.shape
    tb = ROUTE_TB
    return pl.pallas_call(
        functools.partial(_combine_kernel, n_e=n_e, final=final),
        grid=(n // tb,),
        in_specs=[
            pl.BlockSpec(memory_space=pl.ANY),
            pl.BlockSpec(memory_space=pl.ANY),
            pl.BlockSpec((tb, LANES), lambda j: (j, 0)),
            pl.BlockSpec((tb, d), lambda j: (j, 0)),
            pl.BlockSpec((None, None, 1, d), lambda j: (row_of_tile(j), 5, 0, 0)),
            pl.BlockSpec((1, d), lambda j: (0, 0)),
        ],
        out_specs=pl.BlockSpec((tb, d), lambda j: (j, 0)),
        out_shape=jax.ShapeDtypeStruct((n, d), F32),
        scratch_shapes=[
            pltpu.VMEM((n_e, tb, d), F32),
            pltpu.SMEM((ROUTE_ROWS, LANES), jnp.int32),
            pltpu.SemaphoreType.DMA(()),
            pltpu.SemaphoreType.DMA(()),
        ],
        compiler_params=_cparams(("arbitrary",)),
        name="combine",
    )(rl, ye, wsel, x1, mod4, final_g.reshape(1, d))


def _final_norm_kernel(x_ref, g_ref, o_ref):
    x = x_ref[...]
    o_ref[...] = x * lax.rsqrt(jnp.mean(x * x, axis=-1, keepdims=True) + EPS) * g_ref[...]


def _final_norm(x, g, tm):
    n, d = x.shape
    return pl.pallas_call(
        _final_norm_kernel,
        grid=(n // tm,),
        in_specs=[pl.BlockSpec((tm, d), lambda i: (i, 0)), pl.BlockSpec((1, d), lambda i: (0, 0))],
        out_specs=pl.BlockSpec((tm, d), lambda i: (i, 0)),
        out_shape=jax.ShapeDtypeStruct((n, d), F32),
        compiler_params=_cparams(("parallel",)),
        name="final_norm",
    )(x, g.reshape(1, d))


def _pos_embed_2d(n_tokens, d_model):
    rows = n_tokens // GRID_W
    r = jnp.broadcast_to(jnp.arange(rows, dtype=F32)[:, None], (rows, GRID_W)).reshape(-1)
    col = jnp.broadcast_to(jnp.arange(GRID_W, dtype=F32)[None, :], (rows, GRID_W)).reshape(-1)
    quarter = d_model // 4
    freq = jnp.exp(-math.log(10000.0) * jnp.arange(quarter, dtype=F32) / quarter)
    ar = r[:, None] * freq
    ac = col[:, None] * freq
    return jnp.concatenate([jnp.sin(ar), jnp.cos(ar), jnp.sin(ac), jnp.cos(ac)], axis=-1)


def kernel(x_prompt, x_sample, state_delta, state_lru, c, c_ctx, norm1_g, w_mod, b_mod, w_in, conv_qkv, dn_a_log, dn_dt_bias, dn_norm_g, w_dn_out, conv_lru_w, conv_lru_b, lru_wa, lru_ba, lru_wx, lru_bx, lru_lambda, w_lru_out, w_o, norm2_g, w_router, w_gate, w_up, w_down, final_g):
    bp, tp, d = x_prompt.shape
    bs, ts, _ = x_sample.shape
    depth = w_in.shape[0]
    heads, dk, dv = state_delta.shape[3:]
    qk = heads * dk
    vw = heads * dv
    lru_w = state_lru.shape[-1]
    n_experts = w_router.shape[-1]
    n_p, n_s = bp * tp, bs * ts
    n = n_p + n_s
    cap_p = 2 * n_p // n_experts
    cap_s = 2 * n_s // n_experts
    assert dk == LANES and dv == LANES and n_p % ts == 0 and ts % tp == 0

    xs = x_sample + _pos_embed_2d(ts, d)[None]
    x = jnp.concatenate([x_prompt.reshape(n_p, d), xs.reshape(n_s, d)], axis=0)

    cond8 = jnp.zeros((8, d), F32).at[0].set(c_ctx).at[1:1 + bs].set(c)
    mod = _modulation(cond8, w_mod, b_mod)
    mod = mod.reshape(depth, 8, 6, 1, d)

    n_small = 2 * N_DIR * heads
    c0 = 2 * qk + 2 * vw
    w_main = jnp.concatenate([w_in[:, :, :c0], w_in[:, :, c0 + n_small:]], axis=2).astype(BF16)
    w_ba = jnp.pad(w_in[:, :, c0:c0 + n_small], ((0, 0), (0, 0), (0, LANES - n_small))).astype(BF16)
    lx_blk = c0 // LANES
    ly_blk = lx_blk + lru_w // LANES
    ga_blk = (c0 + 2 * lru_w) // d
    gb_blk = ga_blk + 1
    w_dn_b = w_dn_out.astype(BF16)
    w_lru_b = w_lru_out.astype(BF16)
    w_o_b = w_o.astype(BF16)
    w_r_b = jnp.pad(w_router, ((0, 0), (0, 0), (0, LANES - n_experts))).astype(BF16)
    wa_b = lru_wa.astype(BF16)
    wx_b = lru_wx.astype(BF16)
    al_h = jnp.transpose(dn_a_log, (0, 2, 1))[:, :, None, :]
    dt_h = jnp.transpose(dn_dt_bias, (0, 2, 1))[:, :, None, :]

    tm_in = ts
    tiles_p_in = n_p // tm_in
    row_in = lambda i: jnp.where(i < tiles_p_in, 0, i - tiles_p_in + 1)
    tm_mg = 512
    tiles_p_mg = n_p // tm_mg
    per_seq = ts // tm_mg
    row_mg = lambda i: jnp.where(i < tiles_p_mg, 0, (i - tiles_p_mg) // per_seq + 1)
    tiles_p_cb = n_p // ROUTE_TB
    per_seq_cb = ts // ROUTE_TB
    row_cb = lambda i: jnp.where(i < tiles_p_cb, 0, (i - tiles_p_cb) // per_seq_cb + 1)
    assert n_p % ROUTE_TB == 0 and ts % ROUTE_TB == 0 and n_experts + 2 <= ROUTE_ROWS

    sd_out, sl_out = [], []
    for l in range(depth):
        cols, ba = _in_proj(x, norm1_g[l], mod[l], w_main[l], w_ba[l], row_in, tm_in)
        bah = ba[:, :n_small].reshape(n, 2, N_DIR, heads)
        bah = jnp.transpose(bah, (3, 0, 1, 2)).reshape(heads, n, 2 * N_DIR)
        o_p, sd_p = _gdn(cols, bah, conv_qkv[l], al_h[l], dt_h[l], dn_norm_g[l], None,
                         0, bp, tp, heads, dk)
        o_s, _ = _gdn(cols, bah, conv_qkv[l], al_h[l], dt_h[l], dn_norm_g[l], state_delta[:, l],
                      n_p, bs, ts, heads, dk)
        y_p, sl_p = _lru(cols, conv_lru_w[l], conv_lru_b[l], wa_b[l], wx_b[l], lru_ba[l], lru_bx[l],
                         lru_lambda[l], None, 0, bp, tp, lx_blk, ly_blk)
        y_s, _ = _lru(cols, conv_lru_w[l], conv_lru_b[l], wa_b[l], wx_b[l], lru_ba[l], lru_bx[l],
                      lru_lambda[l], state_lru[:, l], n_p, bs, ts, lx_blk, ly_blk)
        o_gdn = jnp.concatenate([o_p, o_s], axis=0)
        y_lru = jnp.concatenate([y_p, y_s], axis=0)
        x1, hn2, aff, afft = _merge(o_gdn, y_lru, cols, x, mod[l], norm2_g[l], w_dn_b[l], w_lru_b[l],
                                    w_o_b[l], w_r_b[l], row_mg, tm_mg, ga_blk, gb_blk, n_experts)
        thr_need = _thresholds(afft, ((0, n_p, cap_p), (n_p, n, cap_s)))
        wsel, rl, xe = _route(aff, thr_need, hn2, n_p, cap_p, cap_s, n_experts)
        ye = _ffn(xe, w_gate[l], w_up[l], w_down[l], 512)
        x = _combine(rl, ye, wsel, x1, mod[l], final_g, row_cb, n_experts, l == depth - 1)
        sd_out.append(sd_p)
        sl_out.append(sl_p)

    y = x
    y_prompt = y[:n_p].reshape(bp, tp, d)
    y_sample = y[n_p:].reshape(bs, ts, d)
    return (y_prompt, y_sample, jnp.stack(sd_out, axis=1), jnp.stack(sl_out, axis=1))
```

```python
import functools
import math

import jax
import jax.numpy as jnp
from jax import lax
from jax.experimental import pallas as pl
from jax.experimental.pallas import tpu as pltpu

F32 = jnp.float32
BF16 = jnp.bfloat16

EPS = 1e-6
CHUNK = 64
SUB = 8
CONV_LEFT = 2
CONV_W = 4
LRU_C = 8.0
LRU_GROUP = 4
N_DIR = 2
GRID_W = 64
LANES = 128
SUBLANES = 8
VMEM_LIMIT = 56 * 1024 * 1024


def _cparams(sem):
    return pltpu.CompilerParams(dimension_semantics=sem, vmem_limit_bytes=VMEM_LIMIT)


def _bdot(a, b):
    return jnp.dot(a.astype(BF16), b.astype(BF16), preferred_element_type=F32)


def _mod_kernel(c_ref, w_ref, b_ref, o_ref):
    c = c_ref[...]
    c = c * jax.nn.sigmoid(c)
    o_ref[...] = _bdot(c, w_ref[...]) + b_ref[...]


def _modulation(cond8, w_mod, b_mod):
    depth, d, n6 = w_mod.shape
    tn = 1536
    return pl.pallas_call(
        _mod_kernel,
        grid=(depth, n6 // tn),
        in_specs=[
            pl.BlockSpec((8, d), lambda l, j: (0, 0)),
            pl.BlockSpec((None, d, tn), lambda l, j: (l, 0, j)),
            pl.BlockSpec((None, 1, tn), lambda l, j: (l, 0, j)),
        ],
        out_specs=pl.BlockSpec((None, 8, tn), lambda l, j: (l, 0, j)),
        out_shape=jax.ShapeDtypeStruct((depth, 8, n6), F32),
        compiler_params=_cparams(("parallel", "parallel")),
        name="modulation",
    )(cond8, w_mod, b_mod.reshape(depth, 1, n6))


def _w_in_prep_kernel(w_ref, main_ref, ba_ref, *, c0, n_small):
    w = w_ref[...]
    rows = w.shape[0]
    main_ref[...] = jnp.concatenate([w[:, :c0], w[:, c0 + n_small:]], axis=1).astype(BF16)
    ba_ref[...] = jnp.concatenate(
        [w[:, c0:c0 + n_small], jnp.zeros((rows, LANES - n_small), F32)], axis=1).astype(BF16)


def _w_in_prep(w_in, c0, n_small):
    depth, d, ncol = w_in.shape
    tk = 256
    return pl.pallas_call(
        functools.partial(_w_in_prep_kernel, c0=c0, n_small=n_small),
        grid=(depth, d // tk),
        in_specs=[pl.BlockSpec((None, tk, ncol), lambda l, i: (l, i, 0))],
        out_specs=[
            pl.BlockSpec((None, tk, ncol - n_small), lambda l, i: (l, i, 0)),
            pl.BlockSpec((None, tk, LANES), lambda l, i: (l, i, 0)),
        ],
        out_shape=[
            jax.ShapeDtypeStruct((depth, d, ncol - n_small), BF16),
            jax.ShapeDtypeStruct((depth, d, LANES), BF16),
        ],
        compiler_params=_cparams(("parallel", "parallel")),
        name="w_in_prep",
    )(w_in)


def _in_proj_kernel(x_ref, g_ref, sh_ref, sc_ref, w_ref, wba_ref, o_ref, oba_ref, hn_scr):
    @pl.when(pl.program_id(1) == 0)
    def _():
        x = x_ref[...]
        y = x * lax.rsqrt(jnp.mean(x * x, axis=-1, keepdims=True) + EPS) * g_ref[...]
        hn = (y * (1.0 + sc_ref[...]) + sh_ref[...]).astype(BF16)
        hn_scr[...] = hn
        oba_ref[...] = jnp.dot(hn, wba_ref[...], preferred_element_type=F32)

    o_ref[...] = jnp.dot(hn_scr[...], w_ref[...], preferred_element_type=F32)


def _in_proj(x, norm_g, mod4, w_main, w_ba, row_of_tile, tm):
    n, d = x.shape
    ncols = w_main.shape[1]
    tn = 1536
    return pl.pallas_call(
        _in_proj_kernel,
        grid=(n // tm, ncols // tn),
        in_specs=[
            pl.BlockSpec((tm, d), lambda i, j: (i, 0)),
            pl.BlockSpec((1, d), lambda i, j: (0, 0)),
            pl.BlockSpec((None, None, 1, d), lambda i, j: (row_of_tile(i), 0, 0, 0)),
            pl.BlockSpec((None, None, 1, d), lambda i, j: (row_of_tile(i), 1, 0, 0)),
            pl.BlockSpec((d, tn), lambda i, j: (0, j)),
            pl.BlockSpec((d, LANES), lambda i, j: (0, 0)),
        ],
        out_specs=[
            pl.BlockSpec((tm, tn), lambda i, j: (i, j)),
            pl.BlockSpec((tm, LANES), lambda i, j: (i, 0)),
        ],
        out_shape=[
            jax.ShapeDtypeStruct((n, ncols), F32),
            jax.ShapeDtypeStruct((n, LANES), F32),
        ],
        scratch_shapes=[pltpu.VMEM((tm, d), BF16)],
        compiler_params=_cparams(("parallel", "arbitrary")),
        name="in_proj",
    )(x, norm_g.reshape(1, d), mod4, mod4, w_main, w_ba)


def _conv_rows(x, w, row):
    t = x.shape[0]
    acc = x * w[CONV_LEFT:CONV_LEFT + 1, :]
    for j in range(CONV_W):
        off = j - CONV_LEFT
        if off == 0:
            continue
        xs = pltpu.roll(x, (-off) % t, axis=0)
        valid = (row + off >= 0) & (row + off < t)
        acc = acc + jnp.where(valid, xs, 0.0) * w[j:j + 1, :]
    return acc


def _softplus(x):
    return jnp.maximum(x, 0.0) + jnp.log1p(jnp.exp(-jnp.abs(x)))


def _bmm(a, b):
    return jnp.einsum('nij,njk->nik', a.astype(BF16), b.astype(BF16), preferred_element_type=F32)


def _bmm_nt(a, b, precision=None):
    return jnp.einsum('nid,njd->nij', a, b, preferred_element_type=F32, precision=precision)


def _unit_tri_inverse(a, ii, jj):
    def same(b):
        return (ii // b) == (jj // b)

    eye = (ii == jj).astype(F32)
    d1 = jnp.where(same(SUB), a, 0.0)
    d2 = _bmm(d1, d1)
    d4 = _bmm(d2, d2)
    x = eye - d1
    x = x + _bmm(x, d2)
    x = x + _bmm(x, d4)
    b = SUB
    while b < CHUNK:
        o = jnp.where(same(2 * b) & jnp.logical_not(same(b)), a, 0.0)
        x = x - _bmm(_bmm(x, o), x)
        b *= 2
    return x


def _gdn_kernel(*refs, t, has_s0, dk):
    if has_s0:
        (q_ref, k_ref, v_ref, z_ref, ba_ref, cq_ref, ck_ref, cv_ref, al_ref, dt_ref, ng_ref,
         oh_ref, s0_ref, o_ref, s_ref) = refs
    else:
        (q_ref, k_ref, v_ref, z_ref, ba_ref, cq_ref, ck_ref, cv_ref, al_ref, dt_ref, ng_ref,
         oh_ref, o_ref, s_ref) = refs
        s0_ref = None
    n = t // CHUNK
    row = lax.broadcasted_iota(jnp.int32, (t, LANES), 0)
    pos = row % CHUNK

    def conv_silu(x_ref, w_ref):
        y = _conv_rows(x_ref[...], w_ref[...], row)
        return y * jax.nn.sigmoid(y)

    def l2n(x):
        return x * lax.rsqrt(jnp.sum(x * x, axis=-1, keepdims=True) + EPS)

    q = l2n(conv_silu(q_ref, cq_ref)) * (dk ** -0.5)
    k = l2n(conv_silu(k_ref, ck_ref))
    v = conv_silu(v_ref, cv_ref)
    q3 = q.reshape(n, CHUNK, LANES).astype(BF16)
    k3 = k.reshape(n, CHUNK, LANES)
    v3 = v.reshape(n, CHUNK, LANES)
    k3b = k3.astype(BF16)
    gram = _bmm_nt(k3b, k3b)
    qk = _bmm_nt(q3, k3b)
    q3 = q.reshape(n, CHUNK, LANES)

    ba = ba_ref[...]
    al = al_ref[...]
    dtb = dt_ref[...]
    ii = lax.broadcasted_iota(jnp.int32, (CHUNK, CHUNK), 0)
    jj = lax.broadcasted_iota(jnp.int32, (CHUNK, CHUNK), 1)
    onehot0 = jnp.broadcast_to(oh_ref[...][None], (n, CHUNK, LANES))

    per_dir = []
    for d in range(N_DIR):
        beta = jax.nn.sigmoid(ba[:, d:d + 1])
        g = -jnp.exp(al[:, d:d + 1]) * _softplus(ba[:, 2 + d:3 + d] + dtb[:, d:d + 1])
        beta_b = jnp.broadcast_to(beta, (t, LANES))
        dec = jnp.broadcast_to(g, (t, LANES))
        step = 1
        while step < CHUNK:
            if d == 0:
                sh = pltpu.roll(dec, step, axis=0)
                dec = dec + jnp.where(pos >= step, sh, 0.0)
            else:
                sh = pltpu.roll(dec, t - step, axis=0)
                dec = dec + jnp.where(pos < CHUNK - step, sh, 0.0)
            step *= 2
        dec3 = dec.reshape(n, CHUNK, LANES)
        tot3 = jnp.broadcast_to(dec3[:, CHUNK - 1:CHUNK, :] if d == 0 else dec3[:, 0:1, :],
                                (n, CHUNK, LANES))
        beta3 = beta_b.reshape(n, CHUNK, LANES)
        dec_row = _bmm_nt(onehot0, dec3, precision=lax.Precision.HIGHEST)
        dec_col = dec3[:, :, :CHUNK]
        tri = (ii >= jj) if d == 0 else (ii <= jj)
        strict = (ii > jj) if d == 0 else (ii < jj)
        gamma = jnp.where(tri, jnp.exp(jnp.where(tri, dec_col - dec_row, 0.0)), 0.0)
        a = jnp.where(strict, gram * gamma, 0.0) * beta3[:, :, :CHUNK]
        x = _unit_tri_inverse(a, ii, jj)
        attn = jnp.where(tri, qk * gamma, 0.0)
        edec = jnp.exp(dec3)
        kbd = k3 * (beta3 * edec)
        vb = v3 * beta3
        qd = q3 * edec
        kd = k3 * jnp.exp(tot3 - dec3)
        cd = jnp.exp(tot3)
        per_dir.append((x.astype(BF16), attn.astype(BF16), kbd.astype(BF16), vb, qd.astype(BF16),
                        kd.astype(BF16), cd))

    states = []
    outs = []
    for d in range(N_DIR):
        if has_s0:
            states.append(s0_ref[d])
        else:
            states.append(jnp.zeros((LANES, LANES), F32))
        outs.append([None] * n)
    for it in range(n):
        for d in range(N_DIR):
            c = it if d == 0 else n - 1 - it
            x, attn, kbd, vb, qd, kd, cd = per_dir[d]
            s = states[d]
            sb = s.astype(BF16)
            r = jnp.dot(kbd[c], sb, preferred_element_type=F32)
            vn = jnp.dot(x[c], (vb[c] - r).astype(BF16), preferred_element_type=F32).astype(BF16)
            outs[d][c] = (jnp.dot(qd[c], sb, preferred_element_type=F32)
                          + jnp.dot(attn[c], vn, preferred_element_type=F32))
            states[d] = s * cd[c][0:1, :] + lax.dot_general(kd[c], vn, (((0,), (0,)), ((), ())),
                                                    preferred_element_type=F32)
    for d in range(N_DIR):
        s_ref[d] = states[d]
    o = jnp.concatenate([outs[0][c] + outs[1][c] for c in range(n)], axis=0)
    o = o * lax.rsqrt(jnp.mean(o * o, axis=-1, keepdims=True) + EPS) * ng_ref[...]
    z = z_ref[...]
    o_ref[...] = (o * (z * jax.nn.sigmoid(z))).astype(o_ref.dtype)


def _gdn(cols, bah, conv_qkv, al, dtb, norm_g, s0, row0, nb, t, heads, dk):
    n = cols.shape[0]
    blk0 = row0 // t
    has_s0 = s0 is not None
    col = lambda off: pl.BlockSpec((t, LANES), lambda b, h: (blk0 + b, off + h))
    cw = lambda off: pl.BlockSpec((CONV_W, LANES), lambda b, h: (0, off + h))
    in_specs = [
        col(0), col(heads), col(2 * heads), col(3 * heads),
        pl.BlockSpec((None, t, 4), lambda b, h: (h, blk0 + b, 0)),
        cw(0), cw(heads), cw(2 * heads),
        pl.BlockSpec((None, 1, N_DIR), lambda b, h: (h, 0, 0)),
        pl.BlockSpec((None, 1, N_DIR), lambda b, h: (h, 0, 0)),
        pl.BlockSpec((1, LANES), lambda b, h: (0, 0)),
        pl.BlockSpec((CHUNK, LANES), lambda b, h: (0, 0)),
    ]
    onehot0 = jnp.zeros((CHUNK, LANES), F32).at[:, 0].set(1.0)
    args = [cols, cols, cols, cols, bah, conv_qkv, conv_qkv, conv_qkv, al, dtb, norm_g.reshape(1, LANES),
            onehot0]
    if has_s0:
        in_specs.append(pl.BlockSpec((None, N_DIR, None, dk, LANES), lambda b, h: (b, 0, h, 0, 0)))
        args.append(s0)
    return pl.pallas_call(
        functools.partial(_gdn_kernel, t=t, has_s0=has_s0, dk=dk),
        grid=(nb, heads),
        in_specs=in_specs,
        out_specs=[
            pl.BlockSpec((t, LANES), lambda b, h: (b, h)),
            pl.BlockSpec((None, N_DIR, None, dk, LANES), lambda b, h: (b, 0, h, 0, 0)),
        ],
        out_shape=[
            jax.ShapeDtypeStruct((nb * t, heads * LANES), BF16),
            jax.ShapeDtypeStruct((nb, N_DIR, heads, dk, LANES), F32),
        ],
        compiler_params=_cparams(("parallel", "parallel")),
        name=f"gdn_t{t}",
    )(*args)


def _lru_kernel(*refs, t, has_h0):
    if has_h0:
        (lx_ref, ly_ref, cw_ref, cb_ref, wa_ref, wx_ref, ba_ref, bx_ref, lam_ref, h0_ref,
         y_ref, last_ref, a_scr, b_scr, h_scr) = refs
    else:
        (lx_ref, ly_ref, cw_ref, cb_ref, wa_ref, wx_ref, ba_ref, bx_ref, lam_ref,
         y_ref, last_ref, a_scr, b_scr, h_scr) = refs
        h0_ref = None
    nblk = t // SUBLANES
    wl = lx_ref.shape[1]
    row = lax.broadcasted_iota(jnp.int32, (t, wl), 0)
    sub = lax.broadcasted_iota(jnp.int32, (nblk, SUBLANES, wl), 1)
    u = _conv_rows(lx_ref[...], cw_ref[...], row) + cb_ref[...]
    ub = u.astype(BF16)

    def block_diag(w_ref, d):
        return jnp.concatenate(
            [jnp.dot(ub[:, c * LANES:(c + 1) * LANES], w_ref[d, c], preferred_element_type=F32)
             for c in range(wl // LANES)], axis=1)

    for d in range(N_DIR):
        r = jax.nn.sigmoid(block_diag(wa_ref, d) + ba_ref[d:d + 1, :])
        i = jax.nn.sigmoid(block_diag(wx_ref, d) + bx_ref[d:d + 1, :])
        log_a = (LRU_C * r) * (-_softplus(-lam_ref[d:d + 1, :]))
        a = jnp.exp(log_a)
        b = jnp.sqrt(1.0 - a * a) * (i * u)
        a3 = a.reshape(nblk, SUBLANES, wl)
        b3 = b.reshape(nblk, SUBLANES, wl)
        step = 1
        while step < SUBLANES:
            if d == 0:
                a_s = pltpu.roll(a3, step, axis=1)
                b_s = pltpu.roll(b3, step, axis=1)
                m = sub >= step
            else:
                a_s = pltpu.roll(a3, SUBLANES - step, axis=1)
                b_s = pltpu.roll(b3, SUBLANES - step, axis=1)
                m = sub < SUBLANES - step
            b3 = b3 + a3 * jnp.where(m, b_s, 0.0)
            a3 = a3 * jnp.where(m, a_s, 1.0)
            step *= 2
        a_scr[d] = a3
        b_scr[d] = b3

    if has_h0:
        h_init = (jnp.broadcast_to(h0_ref[0:1, :], (SUBLANES, wl)),
                  jnp.broadcast_to(h0_ref[1:2, :], (SUBLANES, wl)))
    else:
        h_init = (jnp.zeros((SUBLANES, wl), F32), jnp.zeros((SUBLANES, wl), F32))

    def body(it, carry):
        hf, hb = carry
        kf = it
        kb = nblk - 1 - it
        new_f = b_scr[0, kf] + a_scr[0, kf] * hf
        new_b = b_scr[1, kb] + a_scr[1, kb] * hb
        h_scr[0, kf] = new_f
        h_scr[1, kb] = new_b
        hf = jnp.broadcast_to(new_f[SUBLANES - 1:SUBLANES, :], (SUBLANES, wl))
        hb = jnp.broadcast_to(new_b[0:1, :], (SUBLANES, wl))
        return hf, hb

    hf, hb = lax.fori_loop(0, nblk, body, h_init, unroll=4)
    last_ref[0:1, :] = hf[0:1, :]
    last_ref[1:2, :] = hb[0:1, :]
    rec = (h_scr[0] + h_scr[1]).reshape(t, wl)
    y_ref[...] = (jax.nn.gelu(ly_ref[...]) * rec).astype(y_ref.dtype)


def _lru(cols, conv_w, conv_b, wa, wx, ba, bx, lam, h0, row0, nb, t, lx_blk, ly_blk):
    nblocks = wa.shape[1]
    w = nblocks * LANES
    cg = LRU_GROUP
    wl = cg * LANES
    blk0 = row0 // t
    has_h0 = h0 is not None
    in_specs = [
        pl.BlockSpec((t, wl), lambda b, c: (blk0 + b, lx_blk // cg + c)),
        pl.BlockSpec((t, wl), lambda b, c: (blk0 + b, ly_blk // cg + c)),
        pl.BlockSpec((CONV_W, wl), lambda b, c: (0, c)),
        pl.BlockSpec((1, wl), lambda b, c: (0, c)),
        pl.BlockSpec((N_DIR, cg, LANES, LANES), lambda b, c: (0, c, 0, 0)),
        pl.BlockSpec((N_DIR, cg, LANES, LANES), lambda b, c: (0, c, 0, 0)),
        pl.BlockSpec((N_DIR, wl), lambda b, c: (0, c)),
        pl.BlockSpec((N_DIR, wl), lambda b, c: (0, c)),
        pl.BlockSpec((N_DIR, wl), lambda b, c: (0, c)),
    ]
    args = [cols, cols, conv_w, conv_b.reshape(1, w), wa, wx, ba, bx, lam]
    if has_h0:
        in_specs.append(pl.BlockSpec((None, N_DIR, wl), lambda b, c: (b, 0, c)))
        args.append(h0)
    nblk = t // SUBLANES
    assert lx_blk % cg == 0 and ly_blk % cg == 0 and nblocks % cg == 0
    return pl.pallas_call(
        functools.partial(_lru_kernel, t=t, has_h0=has_h0),
        grid=(nb, nblocks // cg),
        in_specs=in_specs,
        out_specs=[
            pl.BlockSpec((t, wl), lambda b, c: (b, c)),
            pl.BlockSpec((None, N_DIR, wl), lambda b, c: (b, 0, c)),
        ],
        out_shape=[
            jax.ShapeDtypeStruct((nb * t, w), BF16),
            jax.ShapeDtypeStruct((nb, N_DIR, w), F32),
        ],
        scratch_shapes=[pltpu.VMEM((N_DIR, nblk, SUBLANES, wl), F32)] * 3,
        compiler_params=_cparams(("parallel", "parallel")),
        name=f"lru_t{t}",
    )(*args)


def _merge_kernel(oa_ref, ob_ref, ga_ref, gb_ref, x_ref, g1_ref, sh2_ref, sc2_ref, n2_ref,
                  wdn_ref, wlru_ref, wo_ref, wr_ref, x1_ref, hn_ref, aff_ref, afft_ref, *, n_experts):
    ya = jnp.dot(oa_ref[...], wdn_ref[...], preferred_element_type=F32)
    yb = jnp.dot(ob_ref[...], wlru_ref[...], preferred_element_type=F32)
    mix = jax.nn.sigmoid(ga_ref[...]) * ya + jax.nn.sigmoid(gb_ref[...]) * yb
    mix = jnp.dot(mix.astype(BF16), wo_ref[...], preferred_element_type=F32)
    x1 = x_ref[...] + g1_ref[...] * mix
    x1_ref[...] = x1
    y = x1 * lax.rsqrt(jnp.mean(x1 * x1, axis=-1, keepdims=True) + EPS) * n2_ref[...]
    hn = y * (1.0 + sc2_ref[...]) + sh2_ref[...]
    hn_ref[...] = hn
    logits = jnp.dot(hn.astype(BF16), wr_ref[...], preferred_element_type=F32)
    lane = lax.broadcasted_iota(jnp.int32, logits.shape, 1)
    logits = jnp.where(lane < n_experts, logits, -jnp.inf)
    e = jnp.exp(logits - jnp.max(logits, axis=-1, keepdims=True))
    aff = e / jnp.sum(e, axis=-1, keepdims=True)
    aff_ref[...] = aff
    afft_ref[...] = aff.T[:n_experts, :]


def _merge(o_gdn, y_lru, cols, x, mod4, norm2_g, w_dn, w_lru, w_o, w_r, row_of_tile, tm,
           ga_blk, gb_blk, n_experts):
    n, d = x.shape
    dv = o_gdn.shape[1]
    w = y_lru.shape[1]
    modspec = lambda k: pl.BlockSpec((None, None, 1, d), lambda i: (row_of_tile(i), k, 0, 0))
    full = lambda a: pl.BlockSpec(a.shape, lambda i: (0,) * a.ndim)
    return pl.pallas_call(
        functools.partial(_merge_kernel, n_experts=n_experts),
        grid=(n // tm,),
        in_specs=[
            pl.BlockSpec((tm, dv), lambda i: (i, 0)),
            pl.BlockSpec((tm, w), lambda i: (i, 0)),
            pl.BlockSpec((tm, d), lambda i: (i, ga_blk)),
            pl.BlockSpec((tm, d), lambda i: (i, gb_blk)),
            pl.BlockSpec((tm, d), lambda i: (i, 0)),
            modspec(2), modspec(3), modspec(4),
            pl.BlockSpec((1, d), lambda i: (0, 0)),
            full(w_dn), full(w_lru), full(w_o), full(w_r),
        ],
        out_specs=[
            pl.BlockSpec((tm, d), lambda i: (i, 0)),
            pl.BlockSpec((tm, d), lambda i: (i, 0)),
            pl.BlockSpec((tm, LANES), lambda i: (i, 0)),
            pl.BlockSpec((n_experts, tm), lambda i: (0, i)),
        ],
        out_shape=[
            jax.ShapeDtypeStruct((n, d), F32),
            jax.ShapeDtypeStruct((n, d), F32),
            jax.ShapeDtypeStruct((n, LANES), F32),
            jax.ShapeDtypeStruct((n_experts, n), F32),
        ],
        compiler_params=_cparams(("parallel",)),
        name="merge",
    )(o_gdn, y_lru, cols, cols, x, mod4, mod4, mod4, norm2_g.reshape(1, d), w_dn, w_lru, w_o, w_r)


def _ffn_kernel(xp_ref, xs_ref, wg_ref, wu_ref, wd_ref, gv_ref, o_ref, xb_scr):
    f = pl.program_id(1)
    cap_p = xp_ref.shape[0]

    @pl.when(f == 0)
    def _():
        xb_scr[:cap_p, :] = xp_ref[...].astype(BF16)
        xb_scr[cap_p:, :] = xs_ref[...].astype(BF16)

    x = xb_scr[...]
    g = jnp.dot(x, wg_ref[...].astype(BF16), preferred_element_type=F32)
    u = jnp.dot(x, wu_ref[...].astype(BF16), preferred_element_type=F32)
    hid = ((g * jax.nn.sigmoid(g)) * u).astype(BF16)
    y = jnp.dot(hid, wd_ref[...].astype(BF16), preferred_element_type=F32)

    @pl.when(f == 0)
    def _():
        o_ref[...] = y

    @pl.when(f > 0)
    def _():
        o_ref[...] += y

    @pl.when(f == pl.num_programs(1) - 1)
    def _():
        o_ref[...] = o_ref[...] * gv_ref[...]


def _ffn(xe_p, xe_s, gval, w_gate, w_up, w_down, tf):
    e, cap_p, d = xe_p.shape
    cap_s = xe_s.shape[1]
    r = cap_p + cap_s
    ff = w_gate.shape[2]
    return pl.pallas_call(
        _ffn_kernel,
        grid=(e, ff // tf),
        in_specs=[
            pl.BlockSpec((None, cap_p, d), lambda i, f: (i, 0, 0)),
            pl.BlockSpec((None, cap_s, d), lambda i, f: (i, 0, 0)),
            pl.BlockSpec((None, d, tf), lambda i, f: (i, 0, f)),
            pl.BlockSpec((None, d, tf), lambda i, f: (i, 0, f)),
            pl.BlockSpec((None, tf, d), lambda i, f: (i, f, 0)),
            pl.BlockSpec((None, r, 1), lambda i, f: (i, 0, 0)),
        ],
        out_specs=pl.BlockSpec((None, r, d), lambda i, f: (i, 0, 0)),
        out_shape=jax.ShapeDtypeStruct((e, r, d), F32),
        scratch_shapes=[pltpu.VMEM((r, d), BF16)],
        compiler_params=_cparams(("parallel", "arbitrary")),
        name="expert_ffn",
    )(xe_p, xe_s, w_gate, w_up, w_down, gval)


ROUTE_TB = 128
ROUTE_ROWS = 24


def _thr_kernel(afft_ref, o_ref, *, groups):
    n_e = afft_ref.shape[0]
    sub = lax.broadcasted_iota(jnp.int32, (n_e, LANES), 0)
    lane = lax.broadcasted_iota(jnp.int32, (n_e, LANES), 1)
    rows = []
    for lo, hi, cap in groups:
        a = afft_ref[:, lo:hi]
        above_all = 4.0

        def count_ge(v, a=a):
            return jnp.sum((a >= v).astype(F32), axis=1, keepdims=True)

        def count_gt(v, a=a):
            return jnp.sum((a > v).astype(F32), axis=1, keepdims=True)

        def bisect(i, lh, cap=cap, count_ge=count_ge):
            lo_v, hi_v = lh
            mid = 0.5 * (lo_v + hi_v)
            ok = count_ge(mid) >= cap
            return jnp.where(ok, mid, lo_v), jnp.where(ok, hi_v, mid)

        lo_v, _ = lax.fori_loop(0, 48, bisect, (jnp.zeros((n_e, 1), F32), jnp.full((n_e, 1), 2.0, F32)))
        thr = jnp.min(jnp.where(a >= lo_v, a, above_all), axis=1, keepdims=True)

        def not_done(thr, cap=cap, count_gt=count_gt):
            return jnp.max(count_gt(thr)) >= cap

        def step_up(thr, a=a, cap=cap, count_gt=count_gt):
            nxt = jnp.min(jnp.where(a > thr, a, above_all), axis=1, keepdims=True)
            return jnp.where(count_gt(thr) >= cap, nxt, thr)

        thr = lax.while_loop(not_done, step_up, thr)
        need = cap - count_gt(thr)
        for col in (thr, need):
            m = jnp.where(sub == lane, jnp.broadcast_to(col, (n_e, LANES)), 0.0)
            rows.append(jnp.sum(m, axis=0, keepdims=True))
    rows.append(jnp.zeros((SUBLANES - len(rows), LANES), F32))
    o_ref[...] = jnp.concatenate(rows, axis=0)


def _thresholds(afft, groups):
    n_e, n = afft.shape
    return pl.pallas_call(
        functools.partial(_thr_kernel, groups=groups),
        grid=(1,),
        in_specs=[pl.BlockSpec((n_e, n), lambda i: (0, 0))],
        out_specs=pl.BlockSpec((SUBLANES, LANES), lambda i: (0, 0)),
        out_shape=jax.ShapeDtypeStruct((SUBLANES, LANES), F32),
        compiler_params=_cparams(("arbitrary",)),
        name="route_thresholds",
    )(afft)


def _route_kernel(aff_ref, tn_ref, ls_ref, slot_ref, off_ref, cnt_ref, idx_ref, gv_ref,
                  run_eq, run_sel, stage_i, stage_f, rec_i, rec_f, sem_i, sem_f, *, nblk_p, cap_p, n_e):
    tb = ROUTE_TB
    j = pl.program_id(0)

    @pl.when(j == 0)
    def _():
        run_eq[...] = jnp.zeros_like(run_eq)
        run_sel[...] = jnp.zeros_like(run_sel)
        stage_i[...] = jnp.zeros_like(stage_i)

    @pl.when(j == nblk_p)
    def _():
        run_eq[...] = jnp.zeros_like(run_eq)
        run_sel[...] = jnp.full_like(run_sel, float(cap_p))

    g = (j >= nblk_p).astype(jnp.int32)
    thr = tn_ref[pl.ds(2 * g, 1), :]
    need = tn_ref[pl.ds(2 * g + 1, 1), :]
    a = aff_ref[...]
    lane = lax.broadcasted_iota(jnp.int32, (tb, LANES), 1)
    valid = lane < n_e
    gt = (a > thr) & valid
    eq = (a == thr) & valid
    eqf = eq.astype(F32)
    ls = ls_ref[...]
    eq_rank = run_eq[...] + jnp.dot(ls, eqf.astype(BF16), preferred_element_type=F32)
    sel = gt | (eq & (eq_rank < need))
    self_ = sel.astype(F32)
    pos = jnp.dot(ls, self_.astype(BF16), preferred_element_type=F32)
    cnt = jnp.sum(self_, axis=0, keepdims=True)
    off = run_sel[...]
    run_eq[...] = run_eq[...] + jnp.sum(eqf, axis=0, keepdims=True)
    run_sel[...] = off + cnt
    slot_ref[...] = jnp.where(sel, off + pos, -1.0)
    off_ref[...] = off.astype(jnp.int32)
    cnt_ref[...] = cnt.astype(jnp.int32)

    tok = (lax.broadcasted_iota(jnp.int32, (tb, LANES), 0) + j * tb).astype(F32)
    lanef = lane.astype(F32)
    for e in range(n_e):
        pe = jnp.broadcast_to(pos[:, e:e + 1], (tb, LANES))
        se = jnp.broadcast_to(self_[:, e:e + 1], (tb, LANES))
        ae = jnp.broadcast_to(a[:, e:e + 1], (tb, LANES))
        hit = (pe == lanef) & (se > 0.0)
        stage_i[e:e + 1, :] = jnp.sum(jnp.where(hit, tok, 0.0), axis=0, keepdims=True).astype(jnp.int32)
        stage_f[e:e + 1, :] = jnp.sum(jnp.where(hit, ae, 0.0), axis=0, keepdims=True)
    stage_i[n_e:n_e + 1, :] = cnt.astype(jnp.int32)
    stage_i[n_e + 1:n_e + 2, :] = off.astype(jnp.int32)

    copy_i = pltpu.make_async_copy(stage_i, rec_i, sem_i)
    copy_f = pltpu.make_async_copy(stage_f, rec_f, sem_f)
    copy_i.start()
    copy_f.start()
    copy_i.wait()
    copy_f.wait()

    for e in range(n_e):
        c = rec_i[n_e, e]
        o = rec_i[n_e + 1, e]

        def append(q, carry, e=e, o=o):
            idx_ref[e, o + q] = rec_i[e, q]
            gv_ref[e, o + q] = rec_f[e, q]
            return carry

        lax.fori_loop(0, c, append, 0)


def _route(aff, thr_need, n_p, cap_p, cap_s, n_e):
    n = aff.shape[0]
    tb = ROUTE_TB
    nblk = n // tb
    r = cap_p + cap_s
    ls = jnp.tril(jnp.ones((tb, tb), F32), -1).astype(BF16)
    blk_row = pl.BlockSpec((None, 1, LANES), lambda j: (j, 0, 0))
    return pl.pallas_call(
        functools.partial(_route_kernel, nblk_p=n_p // tb, cap_p=cap_p, n_e=n_e),
        grid=(nblk,),
        in_specs=[
            pl.BlockSpec((tb, LANES), lambda j: (j, 0)),
            pl.BlockSpec((SUBLANES, LANES), lambda j: (0, 0)),
            pl.BlockSpec((tb, tb), lambda j: (0, 0)),
        ],
        out_specs=[
            pl.BlockSpec((tb, LANES), lambda j: (j, 0)),
            blk_row,
            blk_row,
            pl.BlockSpec(memory_space=pltpu.SMEM),
            pl.BlockSpec(memory_space=pltpu.SMEM),
        ],
        out_shape=[
            jax.ShapeDtypeStruct((n, LANES), F32),
            jax.ShapeDtypeStruct((nblk, 1, LANES), jnp.int32),
            jax.ShapeDtypeStruct((nblk, 1, LANES), jnp.int32),
            jax.ShapeDtypeStruct((n_e, r), jnp.int32),
            jax.ShapeDtypeStruct((n_e, r), F32),
        ],
        scratch_shapes=[
            pltpu.VMEM((1, LANES), F32),
            pltpu.VMEM((1, LANES), F32),
            pltpu.VMEM((ROUTE_ROWS, LANES), jnp.int32),
            pltpu.VMEM((n_e, LANES), F32),
            pltpu.SMEM((ROUTE_ROWS, LANES), jnp.int32),
            pltpu.SMEM((n_e, LANES), F32),
            pltpu.SemaphoreType.DMA(()),
            pltpu.SemaphoreType.DMA(()),
        ],
        compiler_params=_cparams(("arbitrary",)),
        name="route_slots",
    )(aff, thr_need, ls)


def _gather_kernel(idx_ref, hn_hbm, o_ref, hn_scr, sem, *, row0, slot0, cap):
    e = pl.program_id(0)

    @pl.when(e == 0)
    def _():
        rows = hn_scr.shape[0]
        load = pltpu.make_async_copy(hn_hbm.at[pl.ds(row0, rows)], hn_scr, sem)
        load.start()
        load.wait()

    def body(p, carry):
        t = idx_ref[e, slot0 + p] - row0
        o_ref[pl.ds(p, 1), :] = hn_scr[pl.ds(t, 1), :]
        return carry

    lax.fori_loop(0, cap, body, 0, unroll=8)


def _gather(idx, hn, row0, rows, slot0, cap):
    n_e = idx.shape[0]
    d = hn.shape[1]
    grid_spec = pltpu.PrefetchScalarGridSpec(
        num_scalar_prefetch=1,
        grid=(n_e,),
        in_specs=[pl.BlockSpec(memory_space=pl.ANY)],
        out_specs=pl.BlockSpec((None, cap, d), lambda e, idx_ref: (e, 0, 0)),
        scratch_shapes=[pltpu.VMEM((rows, d), F32), pltpu.SemaphoreType.DMA(())],
    )
    return pl.pallas_call(
        functools.partial(_gather_kernel, row0=row0, slot0=slot0, cap=cap),
        grid_spec=grid_spec,
        out_shape=jax.ShapeDtypeStruct((n_e, cap, d), F32),
        compiler_params=_cparams(("arbitrary",)),
        name=f"gather_rows{rows}",
    )(idx, hn)


COMB_CH = 32
COMB_GRP = 4
COMB_MAXCH = 80


def _combine_kernel(off_ref, cnt_ref, y_hbm, slot_ref, x1_ref, g2_ref, fg_ref, o_ref,
                    buf, acc, desc, sems, *, n_e, final):
    tb = ROUTE_TB
    j = pl.program_id(0)
    nblk = pl.num_programs(0)
    r_total = y_hbm.shape[1]
    par = j % 2

    def issue(jj, p):
        s = jnp.int32(0)
        for e in range(n_e):
            o = off_ref[jj, e]
            c = cnt_ref[jj, e]
            st8 = (o // SUBLANES) * SUBLANES
            nch = jnp.where(c > 0, (o - st8 + c + COMB_CH - 1) // COMB_CH, 0)

            def one(k, s, e=e, st8=st8):
                lo_row = st8 + k * COMB_CH
                base = jnp.minimum(lo_row, r_total - COMB_CH)
                pltpu.make_async_copy(y_hbm.at[e, pl.ds(base, COMB_CH)], buf.at[p, s], sems.at[p]).start()
                desc[p, 0, s] = e
                desc[p, 1, s] = lo_row
                desc[p, 2, s] = base
                return s + 1

            s = lax.fori_loop(0, nch, one, s)
        desc[p, 3, 0] = s

    @pl.when(j == 0)
    def _():
        buf[...] = jnp.zeros_like(buf)

        def clear(i, carry):
            for p in range(2):
                for row in range(4):
                    desc[p, row, i] = 0
            return carry

        lax.fori_loop(0, LANES, clear, 0)
        issue(j, par)

    @pl.when(j + 1 < nblk)
    def _():
        issue(j + 1, 1 - par)

    n_ch = desc[par, 3, 0]

    def drain(k, carry):
        pltpu.make_async_copy(y_hbm.at[0, pl.ds(0, COMB_CH)], buf.at[par, k], sems.at[par]).wait()
        return carry

    lax.fori_loop(0, n_ch, drain, 0)

    slot = slot_ref[...]
    lane = lax.broadcasted_iota(jnp.int32, (tb, LANES), 1)
    lanef = lane.astype(F32)
    acc[...] = jnp.zeros_like(acc)

    def group(gi, carry):
        hit = jnp.zeros((tb, LANES), jnp.bool_)
        for c4 in range(COMB_GRP):
            s = gi * COMB_GRP + c4
            e = desc[par, 0, s]
            lo_row = desc[par, 1, s].astype(F32)
            base = desc[par, 2, s].astype(F32)
            col = jnp.sum(jnp.where(lane == e, slot, 0.0), axis=1, keepdims=True)
            colb = jnp.broadcast_to(col, (tb, LANES))
            in_chunk = (lane >= c4 * COMB_CH) & (lane < (c4 + 1) * COMB_CH)
            match = (colb == base + (lanef - float(c4 * COMB_CH))) & (colb >= lo_row)
            hit = hit | (in_chunk & match & (s < n_ch))
        eb = jnp.where(hit, 1.0, 0.0).astype(BF16)
        yg = buf[par, pl.ds(gi * COMB_GRP, COMB_GRP)].reshape(COMB_GRP * COMB_CH, -1)
        y1 = yg.astype(BF16)
        r1 = yg - y1.astype(F32)
        y2 = r1.astype(BF16)
        y3 = (r1 - y2.astype(F32)).astype(BF16)
        acc[...] += (jnp.dot(eb, y1, preferred_element_type=F32)
                     + jnp.dot(eb, y2, preferred_element_type=F32)
                     + jnp.dot(eb, y3, preferred_element_type=F32))
        return carry

    lax.fori_loop(0, (n_ch + COMB_GRP - 1) // COMB_GRP, group, 0)

    x2 = x1_ref[...] + g2_ref[...] * acc[...]
    if final:
        x2 = x2 * lax.rsqrt(jnp.mean(x2 * x2, axis=-1, keepdims=True) + EPS) * fg_ref[...]
    o_ref[...] = x2


def _combine(off, cnt, ye, slot, x1, mod4, final_g, row_of_tile, n_e, final):
    n, d = x1.shape
    tb = ROUTE_TB
    assert COMB_GRP * COMB_CH == LANES and COMB_MAXCH % COMB_GRP == 0
    assert COMB_MAXCH >= n_e * -(-(tb + SUBLANES - 1) // COMB_CH)
    grid_spec = pltpu.PrefetchScalarGridSpec(
        num_scalar_prefetch=2,
        grid=(n // tb,),
        in_specs=[
            pl.BlockSpec(memory_space=pl.ANY),
            pl.BlockSpec((tb, LANES), lambda j, o_, c_: (j, 0)),
            pl.BlockSpec((tb, d), lambda j, o_, c_: (j, 0)),
            pl.BlockSpec((None, None, 1, d), lambda j, o_, c_: (row_of_tile(j), 5, 0, 0)),
            pl.BlockSpec((1, d), lambda j, o_, c_: (0, 0)),
        ],
        out_specs=pl.BlockSpec((tb, d), lambda j, o_, c_: (j, 0)),
        scratch_shapes=[
            pltpu.VMEM((2, COMB_MAXCH, COMB_CH, d), F32),
            pltpu.VMEM((tb, d), F32),
            pltpu.SMEM((2, 4, LANES), jnp.int32),
            pltpu.SemaphoreType.DMA((2,)),
        ],
    )
    return pl.pallas_call(
        functools.partial(_combine_kernel, n_e=n_e, final=final),
        grid_spec=grid_spec,
        out_shape=jax.ShapeDtypeStruct((n, d), F32),
        compiler_params=_cparams(("arbitrary",)),
        name="combine",
    )(off, cnt, ye, slot, x1, mod4, final_g.reshape(1, d))


def _pos_embed_2d(n_tokens, d_model):
    rows = n_tokens // GRID_W
    r = jnp.broadcast_to(jnp.arange(rows, dtype=F32)[:, None], (rows, GRID_W)).reshape(-1)
    col = jnp.broadcast_to(jnp.arange(GRID_W, dtype=F32)[None, :], (rows, GRID_W)).reshape(-1)
    quarter = d_model // 4
    freq = jnp.exp(-math.log(10000.0) * jnp.arange(quarter, dtype=F32) / quarter)
    ar = r[:, None] * freq
    ac = col[:, None] * freq
    return jnp.concatenate([jnp.sin(ar), jnp.cos(ar), jnp.sin(ac), jnp.cos(ac)], axis=-1)


def kernel(x_prompt, x_sample, state_delta, state_lru, c, c_ctx, norm1_g, w_mod, b_mod, w_in, conv_qkv, dn_a_log, dn_dt_bias, dn_norm_g, w_dn_out, conv_lru_w, conv_lru_b, lru_wa, lru_ba, lru_wx, lru_bx, lru_lambda, w_lru_out, w_o, norm2_g, w_router, w_gate, w_up, w_down, final_g):
    bp, tp, d = x_prompt.shape
    bs, ts, _ = x_sample.shape
    depth = w_in.shape[0]
    heads, dk, dv = state_delta.shape[3:]
    qk = heads * dk
    vw = heads * dv
    lru_w = state_lru.shape[-1]
    n_experts = w_router.shape[-1]
    n_p, n_s = bp * tp, bs * ts
    n = n_p + n_s
    cap_p = 2 * n_p // n_experts
    cap_s = 2 * n_s // n_experts
    assert dk == LANES and dv == LANES and n_p % ts == 0 and ts % tp == 0

    xs = x_sample + _pos_embed_2d(ts, d)[None]
    x = jnp.concatenate([x_prompt.reshape(n_p, d), xs.reshape(n_s, d)], axis=0)

    cond8 = jnp.zeros((8, d), F32).at[0].set(c_ctx).at[1:1 + bs].set(c)
    mod = _modulation(cond8, w_mod, b_mod)
    mod = mod.reshape(depth, 8, 6, 1, d)

    n_small = 2 * N_DIR * heads
    c0 = 2 * qk + 2 * vw
    w_main, w_ba = _w_in_prep(w_in, c0, n_small)
    lx_blk = c0 // LANES
    ly_blk = lx_blk + lru_w // LANES
    ga_blk = (c0 + 2 * lru_w) // d
    gb_blk = ga_blk + 1
    w_dn_b = w_dn_out.astype(BF16)
    w_lru_b = w_lru_out.astype(BF16)
    w_o_b = w_o.astype(BF16)
    w_r_b = jnp.pad(w_router, ((0, 0), (0, 0), (0, LANES - n_experts))).astype(BF16)
    wa_b = lru_wa.astype(BF16)
    wx_b = lru_wx.astype(BF16)
    al_h = jnp.transpose(dn_a_log, (0, 2, 1))[:, :, None, :]
    dt_h = jnp.transpose(dn_dt_bias, (0, 2, 1))[:, :, None, :]

    tm_in = ts
    tiles_p_in = n_p // tm_in
    row_in = lambda i: jnp.where(i < tiles_p_in, 0, i - tiles_p_in + 1)
    tm_mg = 512
    tiles_p_mg = n_p // tm_mg
    per_seq = ts // tm_mg
    row_mg = lambda i: jnp.where(i < tiles_p_mg, 0, (i - tiles_p_mg) // per_seq + 1)
    tiles_p_cb = n_p // ROUTE_TB
    per_seq_cb = ts // ROUTE_TB
    row_cb = lambda i: jnp.where(i < tiles_p_cb, 0, (i - tiles_p_cb) // per_seq_cb + 1)
    assert n_p % ROUTE_TB == 0 and ts % ROUTE_TB == 0 and n_experts + 2 <= ROUTE_ROWS

    sd_out, sl_out = [], []
    for l in range(depth):
        cols, ba = _in_proj(x, norm1_g[l], mod[l], w_main[l], w_ba[l], row_in, tm_in)
        bah = ba[:, :n_small].reshape(n, 2, N_DIR, heads)
        bah = jnp.transpose(bah, (3, 0, 1, 2)).reshape(heads, n, 2 * N_DIR)
        o_p, sd_p = _gdn(cols, bah, conv_qkv[l], al_h[l], dt_h[l], dn_norm_g[l], None,
                         0, bp, tp, heads, dk)
        o_s, _ = _gdn(cols, bah, conv_qkv[l], al_h[l], dt_h[l], dn_norm_g[l], state_delta[:, l],
                      n_p, bs, ts, heads, dk)
        y_p, sl_p = _lru(cols, conv_lru_w[l], conv_lru_b[l], wa_b[l], wx_b[l], lru_ba[l], lru_bx[l],
                         lru_lambda[l], None, 0, bp, tp, lx_blk, ly_blk)
        y_s, _ = _lru(cols, conv_lru_w[l], conv_lru_b[l], wa_b[l], wx_b[l], lru_ba[l], lru_bx[l],
                      lru_lambda[l], state_lru[:, l], n_p, bs, ts, lx_blk, ly_blk)
        o_gdn = jnp.concatenate([o_p, o_s], axis=0)
        y_lru = jnp.concatenate([y_p, y_s], axis=0)
        x1, hn2, aff, afft = _merge(o_gdn, y_lru, cols, x, mod[l], norm2_g[l], w_dn_b[l], w_lru_b[l],
                                    w_o_b[l], w_r_b[l], row_mg, tm_mg, ga_blk, gb_blk, n_experts)
        thr_need = _thresholds(afft, ((0, n_p, cap_p), (n_p, n, cap_s)))
        slot, off, cnt, idx, gv = _route(aff, thr_need, n_p, cap_p, cap_s, n_experts)
        xe_p = _gather(idx, hn2, 0, n_p, 0, cap_p)
        xe_s = _gather(idx, hn2, n_p, n_s, cap_p, cap_s)
        ye = _ffn(xe_p, xe_s, gv[..., None], w_gate[l], w_up[l], w_down[l], 512)
        nblk = n // ROUTE_TB
        x = _combine(off.reshape(nblk, LANES), cnt.reshape(nblk, LANES), ye, slot, x1, mod[l], final_g,
                     row_cb, n_experts, l == depth - 1)
        sd_out.append(sd_p)
        sl_out.append(sl_p)

    y = x
    y_prompt = y[:n_p].reshape(bp, tp, d)
    y_sample = y[n_p:].reshape(bs, ts, d)
    return (y_prompt, y_sample, jnp.stack(sd_out, axis=1), jnp.stack(sl_out, axis=1))
```

```python
import functools
import math

import jax
import jax.numpy as jnp
from jax import lax
from jax.experimental import pallas as pl
from jax.experimental.pallas import tpu as pltpu

F32 = jnp.float32
BF16 = jnp.bfloat16

EPS = 1e-6
CHUNK = 64
SUB = 8
CONV_LEFT = 2
CONV_W = 4
LRU_C = 8.0
LRU_GROUP = 4
N_DIR = 2
GRID_W = 64
LANES = 128
SUBLANES = 8
VMEM_LIMIT = 56 * 1024 * 1024


def _cparams(sem):
    return pltpu.CompilerParams(dimension_semantics=sem, vmem_limit_bytes=VMEM_LIMIT)


def _bdot(a, b):
    return jnp.dot(a.astype(BF16), b.astype(BF16), preferred_element_type=F32)


def _mod_kernel(c_ref, w_ref, b_ref, o_ref):
    c = c_ref[...]
    c = c * jax.nn.sigmoid(c)
    o_ref[...] = _bdot(c, w_ref[...]) + b_ref[...]


def _modulation(cond8, w_mod, b_mod):
    depth, d, n6 = w_mod.shape
    tn = 1536
    return pl.pallas_call(
        _mod_kernel,
        grid=(depth, n6 // tn),
        in_specs=[
            pl.BlockSpec((8, d), lambda l, j: (0, 0)),
            pl.BlockSpec((None, d, tn), lambda l, j: (l, 0, j)),
            pl.BlockSpec((None, 1, tn), lambda l, j: (l, 0, j)),
        ],
        out_specs=pl.BlockSpec((None, 8, tn), lambda l, j: (l, 0, j)),
        out_shape=jax.ShapeDtypeStruct((depth, 8, n6), F32),
        compiler_params=_cparams(("parallel", "parallel")),
        name="modulation",
    )(cond8, w_mod, b_mod.reshape(depth, 1, n6))


def _w_in_prep_kernel(w_ref, main_ref, ba_ref, *, c0, n_small):
    w = w_ref[...]
    rows = w.shape[0]
    main_ref[...] = jnp.concatenate([w[:, :c0], w[:, c0 + n_small:]], axis=1).astype(BF16)
    ba_ref[...] = jnp.concatenate(
        [w[:, c0:c0 + n_small], jnp.zeros((rows, LANES - n_small), F32)], axis=1).astype(BF16)


def _w_in_prep(w_in, c0, n_small):
    depth, d, ncol = w_in.shape
    tk = 256
    return pl.pallas_call(
        functools.partial(_w_in_prep_kernel, c0=c0, n_small=n_small),
        grid=(depth, d // tk),
        in_specs=[pl.BlockSpec((None, tk, ncol), lambda l, i: (l, i, 0))],
        out_specs=[
            pl.BlockSpec((None, tk, ncol - n_small), lambda l, i: (l, i, 0)),
            pl.BlockSpec((None, tk, LANES), lambda l, i: (l, i, 0)),
        ],
        out_shape=[
            jax.ShapeDtypeStruct((depth, d, ncol - n_small), BF16),
            jax.ShapeDtypeStruct((depth, d, LANES), BF16),
        ],
        compiler_params=_cparams(("parallel", "parallel")),
        name="w_in_prep",
    )(w_in)


def _in_proj_kernel(x_ref, g_ref, sh_ref, sc_ref, w_ref, wba_ref, o_ref, oba_ref, hn_scr):
    @pl.when(pl.program_id(1) == 0)
    def _():
        x = x_ref[...]
        y = x * lax.rsqrt(jnp.mean(x * x, axis=-1, keepdims=True) + EPS) * g_ref[...]
        hn = (y * (1.0 + sc_ref[...]) + sh_ref[...]).astype(BF16)
        hn_scr[...] = hn
        oba_ref[...] = jnp.dot(hn, wba_ref[...], preferred_element_type=F32)

    o_ref[...] = jnp.dot(hn_scr[...], w_ref[...], preferred_element_type=F32)


def _in_proj(l, x, norm_g, mod5, w_main, w_ba, row_of_tile, tm):
    n, d = x.shape
    ncols = w_main.shape[2]
    tn = 1536
    return pl.pallas_call(
        _in_proj_kernel,
        grid=(n // tm, ncols // tn),
        in_specs=[
            pl.BlockSpec((tm, d), lambda i, j: (i, 0)),
            pl.BlockSpec((None, 1, d), lambda i, j: (l, 0, 0)),
            pl.BlockSpec((None, None, None, 1, d), lambda i, j: (l, row_of_tile(i), 0, 0, 0)),
            pl.BlockSpec((None, None, None, 1, d), lambda i, j: (l, row_of_tile(i), 1, 0, 0)),
            pl.BlockSpec((None, d, tn), lambda i, j: (l, 0, j)),
            pl.BlockSpec((None, d, LANES), lambda i, j: (l, 0, 0)),
        ],
        out_specs=[
            pl.BlockSpec((tm, tn), lambda i, j: (i, j)),
            pl.BlockSpec((tm, LANES), lambda i, j: (i, 0)),
        ],
        out_shape=[
            jax.ShapeDtypeStruct((n, ncols), F32),
            jax.ShapeDtypeStruct((n, LANES), F32),
        ],
        scratch_shapes=[pltpu.VMEM((tm, d), BF16)],
        compiler_params=_cparams(("parallel", "arbitrary")),
        name="in_proj",
    )(x, norm_g, mod5, mod5, w_main, w_ba)


def _conv_rows(x, w, row):
    t = x.shape[0]
    acc = x * w[CONV_LEFT:CONV_LEFT + 1, :]
    for j in range(CONV_W):
        off = j - CONV_LEFT
        if off == 0:
            continue
        xs = pltpu.roll(x, (-off) % t, axis=0)
        valid = (row + off >= 0) & (row + off < t)
        acc = acc + jnp.where(valid, xs, 0.0) * w[j:j + 1, :]
    return acc


def _softplus(x):
    return jnp.maximum(x, 0.0) + jnp.log1p(jnp.exp(-jnp.abs(x)))


def _bmm(a, b):
    return jnp.einsum('nij,njk->nik', a.astype(BF16), b.astype(BF16), preferred_element_type=F32)


def _bmm_nt(a, b, precision=None):
    return jnp.einsum('nid,njd->nij', a, b, preferred_element_type=F32, precision=precision)


def _unit_tri_inverse(a, ii, jj):
    def same(b):
        return (ii // b) == (jj // b)

    eye = (ii == jj).astype(F32)
    d1 = jnp.where(same(SUB), a, 0.0)
    d2 = _bmm(d1, d1)
    d4 = _bmm(d2, d2)
    x = eye - d1
    x = x + _bmm(x, d2)
    x = x + _bmm(x, d4)
    b = SUB
    while b < CHUNK:
        o = jnp.where(same(2 * b) & jnp.logical_not(same(b)), a, 0.0)
        x = x - _bmm(_bmm(x, o), x)
        b *= 2
    return x


def _gdn_kernel(*refs, t, has_s0, dk):
    if has_s0:
        (q_ref, k_ref, v_ref, z_ref, ba_ref, cq_ref, ck_ref, cv_ref, al_ref, dt_ref, ng_ref,
         oh_ref, s0_ref, o_ref, s_ref) = refs
    else:
        (q_ref, k_ref, v_ref, z_ref, ba_ref, cq_ref, ck_ref, cv_ref, al_ref, dt_ref, ng_ref,
         oh_ref, o_ref, s_ref) = refs
        s0_ref = None
    n = t // CHUNK
    row = lax.broadcasted_iota(jnp.int32, (t, LANES), 0)
    pos = row % CHUNK

    def conv_silu(x_ref, w_ref):
        y = _conv_rows(x_ref[...], w_ref[...], row)
        return y * jax.nn.sigmoid(y)

    def l2n(x):
        return x * lax.rsqrt(jnp.sum(x * x, axis=-1, keepdims=True) + EPS)

    q = l2n(conv_silu(q_ref, cq_ref)) * (dk ** -0.5)
    k = l2n(conv_silu(k_ref, ck_ref))
    v = conv_silu(v_ref, cv_ref)
    q3 = q.reshape(n, CHUNK, LANES).astype(BF16)
    k3 = k.reshape(n, CHUNK, LANES)
    v3 = v.reshape(n, CHUNK, LANES)
    k3b = k3.astype(BF16)
    gram = _bmm_nt(k3b, k3b)
    qk = _bmm_nt(q3, k3b)
    q3 = q.reshape(n, CHUNK, LANES)

    ba = ba_ref[...]
    al = al_ref[...]
    dtb = dt_ref[...]
    ii = lax.broadcasted_iota(jnp.int32, (CHUNK, CHUNK), 0)
    jj = lax.broadcasted_iota(jnp.int32, (CHUNK, CHUNK), 1)
    onehot0 = jnp.broadcast_to(oh_ref[...][None], (n, CHUNK, LANES))

    per_dir = []
    for d in range(N_DIR):
        beta = jax.nn.sigmoid(ba[:, d:d + 1])
        g = -jnp.exp(al[:, d:d + 1]) * _softplus(ba[:, 2 + d:3 + d] + dtb[:, d:d + 1])
        beta_b = jnp.broadcast_to(beta, (t, LANES))
        dec = jnp.broadcast_to(g, (t, LANES))
        step = 1
        while step < CHUNK:
            if d == 0:
                sh = pltpu.roll(dec, step, axis=0)
                dec = dec + jnp.where(pos >= step, sh, 0.0)
            else:
                sh = pltpu.roll(dec, t - step, axis=0)
                dec = dec + jnp.where(pos < CHUNK - step, sh, 0.0)
            step *= 2
        dec3 = dec.reshape(n, CHUNK, LANES)
        tot3 = jnp.broadcast_to(dec3[:, CHUNK - 1:CHUNK, :] if d == 0 else dec3[:, 0:1, :],
                                (n, CHUNK, LANES))
        beta3 = beta_b.reshape(n, CHUNK, LANES)
        dec_row = _bmm_nt(onehot0, dec3, precision=lax.Precision.HIGHEST)
        dec_col = dec3[:, :, :CHUNK]
        tri = (ii >= jj) if d == 0 else (ii <= jj)
        strict = (ii > jj) if d == 0 else (ii < jj)
        gamma = jnp.where(tri, jnp.exp(jnp.where(tri, dec_col - dec_row, 0.0)), 0.0)
        a = jnp.where(strict, gram * gamma, 0.0) * beta3[:, :, :CHUNK]
        x = _unit_tri_inverse(a, ii, jj)
        attn = jnp.where(tri, qk * gamma, 0.0)
        edec = jnp.exp(dec3)
        kbd = k3 * (beta3 * edec)
        vb = v3 * beta3
        qd = q3 * edec
        kd = k3 * jnp.exp(tot3 - dec3)
        cd = jnp.exp(tot3)
        per_dir.append((x.astype(BF16), attn.astype(BF16), kbd.astype(BF16), vb, qd.astype(BF16),
                        kd.astype(BF16), cd))

    states = []
    outs = []
    for d in range(N_DIR):
        if has_s0:
            states.append(s0_ref[d])
        else:
            states.append(jnp.zeros((LANES, LANES), F32))
        outs.append([None] * n)
    for it in range(n):
        for d in range(N_DIR):
            c = it if d == 0 else n - 1 - it
            x, attn, kbd, vb, qd, kd, cd = per_dir[d]
            s = states[d]
            sb = s.astype(BF16)
            r = jnp.dot(kbd[c], sb, preferred_element_type=F32)
            vn = jnp.dot(x[c], (vb[c] - r).astype(BF16), preferred_element_type=F32).astype(BF16)
            outs[d][c] = (jnp.dot(qd[c], sb, preferred_element_type=F32)
                          + jnp.dot(attn[c], vn, preferred_element_type=F32))
            states[d] = s * cd[c][0:1, :] + lax.dot_general(kd[c], vn, (((0,), (0,)), ((), ())),
                                                    preferred_element_type=F32)
    for d in range(N_DIR):
        s_ref[d] = states[d]
    o = jnp.concatenate([outs[0][c] + outs[1][c] for c in range(n)], axis=0)
    o = o * lax.rsqrt(jnp.mean(o * o, axis=-1, keepdims=True) + EPS) * ng_ref[...]
    z = z_ref[...]
    o_ref[...] = (o * (z * jax.nn.sigmoid(z))).astype(o_ref.dtype)


def _gdn(cols, bah, conv_qkv, al, dtb, norm_g, s0, row0, nb, t, heads, dk):
    n = cols.shape[0]
    blk0 = row0 // t
    has_s0 = s0 is not None
    col = lambda off: pl.BlockSpec((t, LANES), lambda b, h: (blk0 + b, off + h))
    cw = lambda off: pl.BlockSpec((CONV_W, LANES), lambda b, h: (0, off + h))
    in_specs = [
        col(0), col(heads), col(2 * heads), col(3 * heads),
        pl.BlockSpec((None, t, 4), lambda b, h: (h, blk0 + b, 0)),
        cw(0), cw(heads), cw(2 * heads),
        pl.BlockSpec((None, 1, N_DIR), lambda b, h: (h, 0, 0)),
        pl.BlockSpec((None, 1, N_DIR), lambda b, h: (h, 0, 0)),
        pl.BlockSpec((1, LANES), lambda b, h: (0, 0)),
        pl.BlockSpec((CHUNK, LANES), lambda b, h: (0, 0)),
    ]
    onehot0 = jnp.zeros((CHUNK, LANES), F32).at[:, 0].set(1.0)
    args = [cols, cols, cols, cols, bah, conv_qkv, conv_qkv, conv_qkv, al, dtb, norm_g.reshape(1, LANES),
            onehot0]
    if has_s0:
        in_specs.append(pl.BlockSpec((None, N_DIR, None, dk, LANES), lambda b, h: (b, 0, h, 0, 0)))
        args.append(s0)
    return pl.pallas_call(
        functools.partial(_gdn_kernel, t=t, has_s0=has_s0, dk=dk),
        grid=(nb, heads),
        in_specs=in_specs,
        out_specs=[
            pl.BlockSpec((t, LANES), lambda b, h: (b, h)),
            pl.BlockSpec((None, N_DIR, None, dk, LANES), lambda b, h: (b, 0, h, 0, 0)),
        ],
        out_shape=[
            jax.ShapeDtypeStruct((nb * t, heads * LANES), BF16),
            jax.ShapeDtypeStruct((nb, N_DIR, heads, dk, LANES), F32),
        ],
        compiler_params=_cparams(("parallel", "parallel")),
        name=f"gdn_t{t}",
    )(*args)


def _gdn_seq_kernel(*refs, t, has_s0, heads, dk):
    if has_s0:
        (q_ref, k_ref, v_ref, z_ref, ba_ref, cq_ref, ck_ref, cv_ref, al_ref, dt_ref, ng_ref,
         oh_ref, s0_ref, o_ref, s_ref, x_scr, at_scr, kq_scr, kd_scr, vb_scr, cd_scr, st_scr, o_scr) = refs
    else:
        (q_ref, k_ref, v_ref, z_ref, ba_ref, cq_ref, ck_ref, cv_ref, al_ref, dt_ref, ng_ref,
         oh_ref, o_ref, s_ref, x_scr, at_scr, kq_scr, kd_scr, vb_scr, cd_scr, st_scr, o_scr) = refs
        s0_ref = None
    n = t // CHUNK
    row = lax.broadcasted_iota(jnp.int32, (t, LANES), 0)
    pos = row % CHUNK
    ii = lax.broadcasted_iota(jnp.int32, (CHUNK, CHUNK), 0)
    jj = lax.broadcasted_iota(jnp.int32, (CHUNK, CHUNK), 1)
    onehot0 = jnp.broadcast_to(oh_ref[...][None], (n, CHUNK, LANES))
    ba = ba_ref[...]
    al = al_ref[...]
    dtb = dt_ref[...]

    def conv_silu(x_ref, w_ref, h):
        y = _conv_rows(x_ref[:, h * LANES:(h + 1) * LANES], w_ref[:, h * LANES:(h + 1) * LANES], row)
        return y * jax.nn.sigmoid(y)

    def l2n(x):
        return x * lax.rsqrt(jnp.sum(x * x, axis=-1, keepdims=True) + EPS)

    for h in range(heads):
        q = l2n(conv_silu(q_ref, cq_ref, h)) * (dk ** -0.5)
        k = l2n(conv_silu(k_ref, ck_ref, h))
        v = conv_silu(v_ref, cv_ref, h)
        q3 = q.reshape(n, CHUNK, LANES)
        k3 = k.reshape(n, CHUNK, LANES)
        v3 = v.reshape(n, CHUNK, LANES)
        k3b = k3.astype(BF16)
        gram = _bmm_nt(k3b, k3b)
        qk = _bmm_nt(q3.astype(BF16), k3b)
        for d in range(N_DIR):
            lb = d * heads + h
            la = 2 * heads + lb
            beta = jax.nn.sigmoid(ba[:, lb:lb + 1])
            g = -jnp.exp(al[:, lb:lb + 1]) * _softplus(ba[:, la:la + 1] + dtb[:, lb:lb + 1])
            beta3 = jnp.broadcast_to(beta, (t, LANES)).reshape(n, CHUNK, LANES)
            dec = jnp.broadcast_to(g, (t, LANES))
            step = 1
            while step < CHUNK:
                if d == 0:
                    sh = pltpu.roll(dec, step, axis=0)
                    dec = dec + jnp.where(pos >= step, sh, 0.0)
                else:
                    sh = pltpu.roll(dec, t - step, axis=0)
                    dec = dec + jnp.where(pos < CHUNK - step, sh, 0.0)
                step *= 2
            dec3 = dec.reshape(n, CHUNK, LANES)
            tot3 = jnp.broadcast_to(dec3[:, CHUNK - 1:CHUNK, :] if d == 0 else dec3[:, 0:1, :],
                                    (n, CHUNK, LANES))
            dec_row = _bmm_nt(onehot0, dec3, precision=lax.Precision.HIGHEST)
            dec_col = dec3[:, :, :CHUNK]
            tri = (ii >= jj) if d == 0 else (ii <= jj)
            strict = (ii > jj) if d == 0 else (ii < jj)
            gamma = jnp.where(tri, jnp.exp(jnp.where(tri, dec_col - dec_row, 0.0)), 0.0)
            a = jnp.where(strict, gram * gamma, 0.0) * beta3[:, :, :CHUNK]
            x_scr[d, h] = _unit_tri_inverse(a, ii, jj).astype(BF16)
            at_scr[d, h] = jnp.where(tri, qk * gamma, 0.0).astype(BF16)
            edec = jnp.exp(dec3)
            kq_scr[d, h, :, :CHUNK, :] = (k3 * (beta3 * edec)).astype(BF16)
            kq_scr[d, h, :, CHUNK:, :] = (q3 * edec).astype(BF16)
            kd_scr[d, h] = (k3 * jnp.exp(tot3 - dec3)).astype(BF16)
            vb_scr[d, h] = v3 * beta3
            cd_scr[d, h] = jnp.exp(tot3[:, :SUBLANES, :])
            if has_s0:
                st_scr[d, h] = s0_ref[d, h]
            else:
                st_scr[d, h] = jnp.zeros((LANES, LANES), F32)

    def scan(it, carry):
        for h in range(heads):
            for d in range(N_DIR):
                c = it if d == 0 else n - 1 - it
                s = st_scr[d, h]
                sb = s.astype(BF16)
                rq = jnp.dot(kq_scr[d, h, c], sb, preferred_element_type=F32)
                vn = jnp.dot(x_scr[d, h, c], (vb_scr[d, h, c] - rq[:CHUNK]).astype(BF16),
                             preferred_element_type=F32).astype(BF16)
                o_scr[d, h, c] = rq[CHUNK:] + jnp.dot(at_scr[d, h, c], vn, preferred_element_type=F32)
                st_scr[d, h] = s * cd_scr[d, h, c][0:1, :] + lax.dot_general(
                    kd_scr[d, h, c], vn, (((0,), (0,)), ((), ())), preferred_element_type=F32)
        return carry

    lax.fori_loop(0, n, scan, 0)

    for h in range(heads):
        for d in range(N_DIR):
            s_ref[d, h] = st_scr[d, h]
        o = (o_scr[0, h] + o_scr[1, h]).reshape(t, LANES)
        o = o * lax.rsqrt(jnp.mean(o * o, axis=-1, keepdims=True) + EPS) * ng_ref[...]
        z = z_ref[:, h * LANES:(h + 1) * LANES]
        o_ref[:, h * LANES:(h + 1) * LANES] = (o * (z * jax.nn.sigmoid(z))).astype(o_ref.dtype)


def _gdn_seq(l, cols, ba, conv_qkv, al, dtb, norm_g, s0, row0, nb, t, heads, dk):
    blk0 = row0 // t
    has_s0 = s0 is not None
    hw = heads * LANES
    n = t // CHUNK
    col = lambda off: pl.BlockSpec((t, hw), lambda b: (blk0 + b, off))
    cw = lambda off: pl.BlockSpec((None, CONV_W, hw), lambda b: (l, 0, off))
    vec = pl.BlockSpec((None, 1, LANES), lambda b: (l, 0, 0))
    in_specs = [
        col(0), col(1), col(2), col(3),
        pl.BlockSpec((t, LANES), lambda b: (blk0 + b, 0)),
        cw(0), cw(1), cw(2),
        vec, vec, vec,
        pl.BlockSpec((CHUNK, LANES), lambda b: (0, 0)),
    ]
    onehot0 = jnp.zeros((CHUNK, LANES), F32).at[:, 0].set(1.0)
    args = [cols, cols, cols, cols, ba, conv_qkv, conv_qkv, conv_qkv, al, dtb, norm_g, onehot0]
    if has_s0:
        in_specs.append(pl.BlockSpec((None, None, N_DIR, heads, dk, LANES), lambda b: (b, l, 0, 0, 0, 0)))
        args.append(s0)
    per = (N_DIR, heads, n)
    return pl.pallas_call(
        functools.partial(_gdn_seq_kernel, t=t, has_s0=has_s0, heads=heads, dk=dk),
        grid=(nb,),
        in_specs=in_specs,
        out_specs=[
            pl.BlockSpec((t, hw), lambda b: (b, 0)),
            pl.BlockSpec((None, N_DIR, heads, dk, LANES), lambda b: (b, 0, 0, 0, 0)),
        ],
        out_shape=[
            jax.ShapeDtypeStruct((nb * t, hw), BF16),
            jax.ShapeDtypeStruct((nb, N_DIR, heads, dk, LANES), F32),
        ],
        scratch_shapes=[
            pltpu.VMEM(per + (CHUNK, CHUNK), BF16),
            pltpu.VMEM(per + (CHUNK, CHUNK), BF16),
            pltpu.VMEM(per + (2 * CHUNK, LANES), BF16),
            pltpu.VMEM(per + (CHUNK, LANES), BF16),
            pltpu.VMEM(per + (CHUNK, LANES), F32),
            pltpu.VMEM(per + (SUBLANES, LANES), F32),
            pltpu.VMEM((N_DIR, heads, dk, LANES), F32),
            pltpu.VMEM(per + (CHUNK, LANES), F32),
        ],
        compiler_params=_cparams(("parallel",)),
        name=f"gdn_t{t}",
    )(*args)


def _lru_kernel(*refs, t, has_h0):
    if has_h0:
        (lx_ref, ly_ref, cw_ref, cb_ref, wa_ref, wx_ref, ba_ref, bx_ref, lam_ref, h0_ref,
         y_ref, last_ref, a_scr, b_scr, h_scr) = refs
    else:
        (lx_ref, ly_ref, cw_ref, cb_ref, wa_ref, wx_ref, ba_ref, bx_ref, lam_ref,
         y_ref, last_ref, a_scr, b_scr, h_scr) = refs
        h0_ref = None
    nblk = t // SUBLANES
    wl = lx_ref.shape[1]
    row = lax.broadcasted_iota(jnp.int32, (t, wl), 0)
    sub = lax.broadcasted_iota(jnp.int32, (nblk, SUBLANES, wl), 1)
    u = _conv_rows(lx_ref[...], cw_ref[...], row) + cb_ref[...]
    ub = u.astype(BF16)

    def block_diag(w_ref, d):
        return jnp.concatenate(
            [jnp.dot(ub[:, c * LANES:(c + 1) * LANES], w_ref[d, c], preferred_element_type=F32)
             for c in range(wl // LANES)], axis=1)

    for d in range(N_DIR):
        r = jax.nn.sigmoid(block_diag(wa_ref, d) + ba_ref[d:d + 1, :])
        i = jax.nn.sigmoid(block_diag(wx_ref, d) + bx_ref[d:d + 1, :])
        log_a = (LRU_C * r) * (-_softplus(-lam_ref[d:d + 1, :]))
        a = jnp.exp(log_a)
        b = jnp.sqrt(1.0 - a * a) * (i * u)
        a3 = a.reshape(nblk, SUBLANES, wl)
        b3 = b.reshape(nblk, SUBLANES, wl)
        step = 1
        while step < SUBLANES:
            if d == 0:
                a_s = pltpu.roll(a3, step, axis=1)
                b_s = pltpu.roll(b3, step, axis=1)
                m = sub >= step
            else:
                a_s = pltpu.roll(a3, SUBLANES - step, axis=1)
                b_s = pltpu.roll(b3, SUBLANES - step, axis=1)
                m = sub < SUBLANES - step
            b3 = b3 + a3 * jnp.where(m, b_s, 0.0)
            a3 = a3 * jnp.where(m, a_s, 1.0)
            step *= 2
        a_scr[d] = a3
        b_scr[d] = b3

    if has_h0:
        h_init = (jnp.broadcast_to(h0_ref[0:1, :], (SUBLANES, wl)),
                  jnp.broadcast_to(h0_ref[1:2, :], (SUBLANES, wl)))
    else:
        h_init = (jnp.zeros((SUBLANES, wl), F32), jnp.zeros((SUBLANES, wl), F32))

    def body(it, carry):
        hf, hb = carry
        kf = it
        kb = nblk - 1 - it
        new_f = b_scr[0, kf] + a_scr[0, kf] * hf
        new_b = b_scr[1, kb] + a_scr[1, kb] * hb
        h_scr[0, kf] = new_f
        h_scr[1, kb] = new_b
        hf = jnp.broadcast_to(new_f[SUBLANES - 1:SUBLANES, :], (SUBLANES, wl))
        hb = jnp.broadcast_to(new_b[0:1, :], (SUBLANES, wl))
        return hf, hb

    hf, hb = lax.fori_loop(0, nblk, body, h_init, unroll=4)
    last_ref[0:1, :] = hf[0:1, :]
    last_ref[1:2, :] = hb[0:1, :]
    rec = (h_scr[0] + h_scr[1]).reshape(t, wl)
    y_ref[...] = (jax.nn.gelu(ly_ref[...]) * rec).astype(y_ref.dtype)


def _lru(l, cols, conv_w, conv_b, wa, wx, ba, bx, lam, h0, row0, nb, t, lx_blk, ly_blk):
    nblocks = wa.shape[2]
    w = nblocks * LANES
    cg = LRU_GROUP
    wl = cg * LANES
    blk0 = row0 // t
    has_h0 = h0 is not None
    in_specs = [
        pl.BlockSpec((t, wl), lambda b, c: (blk0 + b, lx_blk // cg + c)),
        pl.BlockSpec((t, wl), lambda b, c: (blk0 + b, ly_blk // cg + c)),
        pl.BlockSpec((None, CONV_W, wl), lambda b, c: (l, 0, c)),
        pl.BlockSpec((None, 1, wl), lambda b, c: (l, 0, c)),
        pl.BlockSpec((None, N_DIR, cg, LANES, LANES), lambda b, c: (l, 0, c, 0, 0)),
        pl.BlockSpec((None, N_DIR, cg, LANES, LANES), lambda b, c: (l, 0, c, 0, 0)),
        pl.BlockSpec((None, N_DIR, wl), lambda b, c: (l, 0, c)),
        pl.BlockSpec((None, N_DIR, wl), lambda b, c: (l, 0, c)),
        pl.BlockSpec((None, N_DIR, wl), lambda b, c: (l, 0, c)),
    ]
    args = [cols, cols, conv_w, conv_b, wa, wx, ba, bx, lam]
    if has_h0:
        in_specs.append(pl.BlockSpec((None, None, N_DIR, wl), lambda b, c: (b, l, 0, c)))
        args.append(h0)
    nblk = t // SUBLANES
    assert lx_blk % cg == 0 and ly_blk % cg == 0 and nblocks % cg == 0
    return pl.pallas_call(
        functools.partial(_lru_kernel, t=t, has_h0=has_h0),
        grid=(nb, nblocks // cg),
        in_specs=in_specs,
        out_specs=[
            pl.BlockSpec((t, wl), lambda b, c: (b, c)),
            pl.BlockSpec((None, N_DIR, wl), lambda b, c: (b, 0, c)),
        ],
        out_shape=[
            jax.ShapeDtypeStruct((nb * t, w), BF16),
            jax.ShapeDtypeStruct((nb, N_DIR, w), F32),
        ],
        scratch_shapes=[pltpu.VMEM((N_DIR, nblk, SUBLANES, wl), F32)] * 3,
        compiler_params=_cparams(("parallel", "parallel")),
        name=f"lru_t{t}",
    )(*args)


def _merge_kernel(oap_ref, oas_ref, obp_ref, obs_ref, ga_ref, gb_ref, x_ref, g1_ref, sh2_ref, sc2_ref,
                  n2_ref, wdn_ref, wlru_ref, wo_ref, wr_ref, x1_ref, hn_ref, afft_ref, *, n_experts, tiles_p):
    d = x_ref.shape[1]
    is_p = pl.program_id(0) < tiles_p
    oa = jnp.where(is_p, oap_ref[...], oas_ref[...])
    ob = jnp.where(is_p, obp_ref[...], obs_ref[...])
    ya = jnp.dot(oa, wdn_ref[...], preferred_element_type=F32)
    yb = jnp.dot(ob, wlru_ref[...], preferred_element_type=F32)
    mix = jax.nn.sigmoid(ga_ref[...]) * ya + jax.nn.sigmoid(gb_ref[...]) * yb
    mix = jnp.dot(mix.astype(BF16), wo_ref[...], preferred_element_type=F32)
    x1 = x_ref[...] + g1_ref[...] * mix
    x1_ref[...] = x1
    y = x1 * lax.rsqrt(jnp.mean(x1 * x1, axis=-1, keepdims=True) + EPS) * n2_ref[...]
    hn = y * (1.0 + sc2_ref[...]) + sh2_ref[...]
    logits = jnp.dot(hn.astype(BF16), wr_ref[...], preferred_element_type=F32)
    lane = lax.broadcasted_iota(jnp.int32, logits.shape, 1)
    logits = jnp.where(lane < n_experts, logits, -jnp.inf)
    e = jnp.exp(logits - jnp.max(logits, axis=-1, keepdims=True))
    aff = e / jnp.sum(e, axis=-1, keepdims=True)
    hn_ref[:, :d] = hn
    hn_ref[:, d:] = aff
    afft_ref[...] = aff.T[:n_experts, :]


def _merge(l, o_p, o_s, y_p, y_s, cols, x, mod5, norm2_g, w_dn, w_lru, w_o, w_r, row_of_tile, tm,
           ga_blk, gb_blk, n_experts):
    n, d = x.shape
    dv = o_p.shape[1]
    w = y_p.shape[1]
    tiles_p = o_p.shape[0] // tm
    modspec = lambda k: pl.BlockSpec((None, None, None, 1, d), lambda i: (l, row_of_tile(i), k, 0, 0))
    layer = lambda a: pl.BlockSpec((None,) + a.shape[1:], lambda i: (l,) + (0,) * (a.ndim - 1))
    p_tile = lambda width: pl.BlockSpec((tm, width), lambda i: (jnp.minimum(i, tiles_p - 1), 0))
    s_tile = lambda width: pl.BlockSpec((tm, width), lambda i: (jnp.maximum(i - tiles_p, 0), 0))
    return pl.pallas_call(
        functools.partial(_merge_kernel, n_experts=n_experts, tiles_p=tiles_p),
        grid=(n // tm,),
        in_specs=[
            p_tile(dv), s_tile(dv), p_tile(w), s_tile(w),
            pl.BlockSpec((tm, d), lambda i: (i, ga_blk)),
            pl.BlockSpec((tm, d), lambda i: (i, gb_blk)),
            pl.BlockSpec((tm, d), lambda i: (i, 0)),
            modspec(2), modspec(3), modspec(4),
            layer(norm2_g), layer(w_dn), layer(w_lru), layer(w_o), layer(w_r),
        ],
        out_specs=[
            pl.BlockSpec((tm, d), lambda i: (i, 0)),
            pl.BlockSpec((tm, d + LANES), lambda i: (i, 0)),
            pl.BlockSpec((n_experts, tm), lambda i: (0, i)),
        ],
        out_shape=[
            jax.ShapeDtypeStruct((n, d), F32),
            jax.ShapeDtypeStruct((n, d + LANES), F32),
            jax.ShapeDtypeStruct((n_experts, n), F32),
        ],
        compiler_params=_cparams(("parallel",)),
        name="merge",
    )(o_p, o_s, y_p, y_s, cols, cols, x, mod5, mod5, mod5, norm2_g, w_dn, w_lru, w_o, w_r)


def _ffn_kernel(xp_ref, xs_ref, wg_ref, wu_ref, wd_ref, o_ref, xb_scr):
    f = pl.program_id(1)
    cap_p = xp_ref.shape[0]
    d = xb_scr.shape[1]

    @pl.when(f == 0)
    def _():
        xb_scr[:cap_p, :] = xp_ref[:, :d].astype(BF16)
        xb_scr[cap_p:, :] = xs_ref[:, :d].astype(BF16)

    x = xb_scr[...]
    g = jnp.dot(x, wg_ref[...].astype(BF16), preferred_element_type=F32)
    u = jnp.dot(x, wu_ref[...].astype(BF16), preferred_element_type=F32)
    hid = ((g * jax.nn.sigmoid(g)) * u).astype(BF16)
    y = jnp.dot(hid, wd_ref[...].astype(BF16), preferred_element_type=F32)

    @pl.when(f == 0)
    def _():
        o_ref[...] = y

    @pl.when(f > 0)
    def _():
        o_ref[...] += y

    @pl.when(f == pl.num_programs(1) - 1)
    def _():
        e = pl.program_id(0)
        for ref, r0 in ((xp_ref, 0), (xs_ref, cap_p)):
            aff = ref[:, d:]
            lane = lax.broadcasted_iota(jnp.int32, aff.shape, 1)
            gv = jnp.sum(jnp.where(lane == e, aff, 0.0), axis=1, keepdims=True)
            rows = ref.shape[0]
            o_ref[r0:r0 + rows, :] = o_ref[r0:r0 + rows, :] * gv


def _ffn(l, xe_p, xe_s, w_gate, w_up, w_down, tf):
    e, cap_p, da = xe_p.shape
    d = da - LANES
    cap_s = xe_s.shape[1]
    r = cap_p + cap_s
    ff = w_gate.shape[3]
    return pl.pallas_call(
        _ffn_kernel,
        grid=(e, ff // tf),
        in_specs=[
            pl.BlockSpec((None, cap_p, da), lambda i, f: (i, 0, 0)),
            pl.BlockSpec((None, cap_s, da), lambda i, f: (i, 0, 0)),
            pl.BlockSpec((None, None, d, tf), lambda i, f: (l, i, 0, f)),
            pl.BlockSpec((None, None, d, tf), lambda i, f: (l, i, 0, f)),
            pl.BlockSpec((None, None, tf, d), lambda i, f: (l, i, f, 0)),
        ],
        out_specs=pl.BlockSpec((None, r, d), lambda i, f: (i, 0, 0)),
        out_shape=jax.ShapeDtypeStruct((e, r, d), F32),
        scratch_shapes=[pltpu.VMEM((r, d), BF16)],
        compiler_params=_cparams(("parallel", "arbitrary")),
        name="expert_ffn",
    )(xe_p, xe_s, w_gate, w_up, w_down)


ROUTE_TB = 128
ROUTE_ROWS = 24


def _thr_kernel(afft_ref, o_ref, *, groups):
    n_e = afft_ref.shape[0]
    sub = lax.broadcasted_iota(jnp.int32, (n_e, LANES), 0)
    lane = lax.broadcasted_iota(jnp.int32, (n_e, LANES), 1)
    rows = []
    for lo, hi, cap in groups:
        a = afft_ref[:, lo:hi]
        above_all = 4.0

        def count_ge(v, a=a):
            return jnp.sum((a >= v).astype(F32), axis=1, keepdims=True)

        def count_gt(v, a=a):
            return jnp.sum((a > v).astype(F32), axis=1, keepdims=True)

        def bisect(i, lh, cap=cap, count_ge=count_ge):
            lo_v, hi_v = lh
            mid = 0.5 * (lo_v + hi_v)
            ok = count_ge(mid) >= cap
            return jnp.where(ok, mid, lo_v), jnp.where(ok, hi_v, mid)

        lo_v, _ = lax.fori_loop(0, 48, bisect, (jnp.zeros((n_e, 1), F32), jnp.full((n_e, 1), 2.0, F32)))
        thr = jnp.min(jnp.where(a >= lo_v, a, above_all), axis=1, keepdims=True)

        def not_done(thr, cap=cap, count_gt=count_gt):
            return jnp.max(count_gt(thr)) >= cap

        def step_up(thr, a=a, cap=cap, count_gt=count_gt):
            nxt = jnp.min(jnp.where(a > thr, a, above_all), axis=1, keepdims=True)
            return jnp.where(count_gt(thr) >= cap, nxt, thr)

        thr = lax.while_loop(not_done, step_up, thr)
        need = cap - count_gt(thr)
        for col in (thr, need):
            m = jnp.where(sub == lane, jnp.broadcast_to(col, (n_e, LANES)), 0.0)
            rows.append(jnp.sum(m, axis=0, keepdims=True))
    rows.append(jnp.zeros((SUBLANES - len(rows), LANES), F32))
    o_ref[...] = jnp.concatenate(rows, axis=0)


def _thresholds(afft, groups):
    n_e, n = afft.shape
    return pl.pallas_call(
        functools.partial(_thr_kernel, groups=groups),
        grid=(1,),
        in_specs=[pl.BlockSpec((n_e, n), lambda i: (0, 0))],
        out_specs=pl.BlockSpec((SUBLANES, LANES), lambda i: (0, 0)),
        out_shape=jax.ShapeDtypeStruct((SUBLANES, LANES), F32),
        compiler_params=_cparams(("arbitrary",)),
        name="route_thresholds",
    )(afft)


def _route_kernel(aff_ref, tn_ref, ls_ref, slot_ref, off_ref, cnt_ref, lst_ref,
                  run_eq, run_sel, *, nblk_p, cap_p, n_e):
    tb = ROUTE_TB
    j = pl.program_id(0)

    @pl.when(j == 0)
    def _():
        run_eq[...] = jnp.zeros_like(run_eq)
        run_sel[...] = jnp.zeros_like(run_sel)

    @pl.when(j == nblk_p)
    def _():
        run_eq[...] = jnp.zeros_like(run_eq)
        run_sel[...] = jnp.full_like(run_sel, float(cap_p))

    g = (j >= nblk_p).astype(jnp.int32)
    thr = tn_ref[pl.ds(2 * g, 1), :]
    need = tn_ref[pl.ds(2 * g + 1, 1), :]
    a = aff_ref[...]
    lane = lax.broadcasted_iota(jnp.int32, (tb, LANES), 1)
    valid = lane < n_e
    gt = (a > thr) & valid
    eq = (a == thr) & valid
    eqf = eq.astype(F32)
    ls = ls_ref[...]
    eq_rank = run_eq[...] + jnp.dot(ls, eqf.astype(BF16), preferred_element_type=F32)
    sel = gt | (eq & (eq_rank < need))
    self_ = sel.astype(F32)
    pos = jnp.dot(ls, self_.astype(BF16), preferred_element_type=F32)
    cnt = jnp.sum(self_, axis=0, keepdims=True)
    off = run_sel[...]
    run_eq[...] = run_eq[...] + jnp.sum(eqf, axis=0, keepdims=True)
    run_sel[...] = off + cnt
    slot_ref[...] = jnp.where(sel, off + pos, -1.0)
    off_ref[...] = off.astype(jnp.int32)
    cnt_ref[...] = cnt.astype(jnp.int32)

    tok = (lax.broadcasted_iota(jnp.int32, (tb, LANES), 0) + j * tb).astype(F32)
    lanef = lane.astype(F32)
    for e in range(n_e):
        pe = jnp.broadcast_to(pos[:, e:e + 1], (tb, LANES))
        se = jnp.broadcast_to(self_[:, e:e + 1], (tb, LANES))
        hit = (pe == lanef) & (se > 0.0)
        lst_ref[e] = jnp.sum(jnp.where(hit, tok, 0.0), axis=0, keepdims=True).astype(jnp.int32)


def _route(hn_aug, thr_need, n_p, cap_p, n_e):
    n = hn_aug.shape[0]
    aff_blk = hn_aug.shape[1] // LANES - 1
    tb = ROUTE_TB
    nblk = n // tb
    ls = jnp.tril(jnp.ones((tb, tb), F32), -1).astype(BF16)
    blk_row = pl.BlockSpec((None, 1, LANES), lambda j: (j, 0, 0))
    return pl.pallas_call(
        functools.partial(_route_kernel, nblk_p=n_p // tb, cap_p=cap_p, n_e=n_e),
        grid=(nblk,),
        in_specs=[
            pl.BlockSpec((tb, LANES), lambda j: (j, aff_blk)),
            pl.BlockSpec((SUBLANES, LANES), lambda j: (0, 0)),
            pl.BlockSpec((tb, tb), lambda j: (0, 0)),
        ],
        out_specs=[
            pl.BlockSpec((tb, LANES), lambda j: (j, 0)),
            blk_row,
            blk_row,
            pl.BlockSpec((n_e, None, 1, LANES), lambda j: (0, j, 0, 0)),
        ],
        out_shape=[
            jax.ShapeDtypeStruct((n, LANES), F32),
            jax.ShapeDtypeStruct((nblk, 1, LANES), jnp.int32),
            jax.ShapeDtypeStruct((nblk, 1, LANES), jnp.int32),
            jax.ShapeDtypeStruct((n_e, nblk, 1, LANES), jnp.int32),
        ],
        scratch_shapes=[pltpu.VMEM((1, LANES), F32), pltpu.VMEM((1, LANES), F32)],
        compiler_params=_cparams(("arbitrary",)),
        name="route_slots",
    )(hn_aug, thr_need, ls)


def _gather_kernel(off_ref, cnt_ref, lst_hbm, hn_hbm, o_ref, hn_scr, lst_smem, sem_h, sem_l,
                   *, row0, slot0, blk0, nblk_g):
    e = pl.program_id(0)
    load_lst = pltpu.make_async_copy(lst_hbm.at[e, pl.ds(blk0, nblk_g)], lst_smem, sem_l)
    load_lst.start()

    @pl.when(e == 0)
    def _():
        rows = hn_scr.shape[0]
        load = pltpu.make_async_copy(hn_hbm.at[pl.ds(row0, rows)], hn_scr, sem_h)
        load.start()
        load.wait()

    load_lst.wait()

    def block(jb, carry):
        c = cnt_ref[blk0 + jb, e]
        o = off_ref[blk0 + jb, e] - slot0

        def row(q, carry2):
            t = lst_smem[jb, q] - row0
            o_ref[pl.ds(o + q, 1), :] = hn_scr[pl.ds(t, 1), :]
            return carry2

        lax.fori_loop(0, c, row, 0)
        return carry

    lax.fori_loop(0, nblk_g, block, 0)


def _gather(off, cnt, lst, hn, row0, rows, slot0, cap):
    n_e = lst.shape[0]
    width = hn.shape[1]
    blk0 = row0 // ROUTE_TB
    nblk_g = rows // ROUTE_TB
    grid_spec = pltpu.PrefetchScalarGridSpec(
        num_scalar_prefetch=2,
        grid=(n_e,),
        in_specs=[pl.BlockSpec(memory_space=pl.ANY), pl.BlockSpec(memory_space=pl.ANY)],
        out_specs=pl.BlockSpec((None, cap, width), lambda e, o_, c_: (e, 0, 0)),
        scratch_shapes=[
            pltpu.VMEM((rows, width), F32),
            pltpu.SMEM((nblk_g, LANES), jnp.int32),
            pltpu.SemaphoreType.DMA(()),
            pltpu.SemaphoreType.DMA(()),
        ],
    )
    return pl.pallas_call(
        functools.partial(_gather_kernel, row0=row0, slot0=slot0, blk0=blk0, nblk_g=nblk_g),
        grid_spec=grid_spec,
        out_shape=jax.ShapeDtypeStruct((n_e, cap, width), F32),
        compiler_params=_cparams(("arbitrary",)),
        name=f"gather_rows{rows}",
    )(off, cnt, lst.reshape(n_e, -1, LANES), hn)


COMB_CH = 32
COMB_GRP = 4
COMB_MAXCH = 80


def _combine_kernel(off_ref, cnt_ref, y_hbm, slot_ref, x1_ref, g2_ref, fg_ref, o_ref,
                    buf, acc, desc, sems, *, n_e, final):
    tb = ROUTE_TB
    j = pl.program_id(0)
    nblk = pl.num_programs(0)
    r_total = y_hbm.shape[1]
    par = j % 2

    def issue(jj, p):
        s = jnp.int32(0)
        for e in range(n_e):
            o = off_ref[jj, e]
            c = cnt_ref[jj, e]
            st8 = (o // SUBLANES) * SUBLANES
            nch = jnp.where(c > 0, (o - st8 + c + COMB_CH - 1) // COMB_CH, 0)

            def one(k, s, e=e, st8=st8):
                lo_row = st8 + k * COMB_CH
                base = jnp.minimum(lo_row, r_total - COMB_CH)
                pltpu.make_async_copy(y_hbm.at[e, pl.ds(base, COMB_CH)], buf.at[p, s], sems.at[p]).start()
                desc[p, 0, s] = e
                desc[p, 1, s] = lo_row
                desc[p, 2, s] = base
                return s + 1

            s = lax.fori_loop(0, nch, one, s)
        desc[p, 3, 0] = s

    @pl.when(j == 0)
    def _():
        buf[...] = jnp.zeros_like(buf)

        def clear(i, carry):
            for p in range(2):
                for row in range(4):
                    desc[p, row, i] = 0
            return carry

        lax.fori_loop(0, LANES, clear, 0)
        issue(j, par)

    @pl.when(j + 1 < nblk)
    def _():
        issue(j + 1, 1 - par)

    n_ch = desc[par, 3, 0]

    def drain(k, carry):
        pltpu.make_async_copy(y_hbm.at[0, pl.ds(0, COMB_CH)], buf.at[par, k], sems.at[par]).wait()
        return carry

    lax.fori_loop(0, n_ch, drain, 0)

    slot = slot_ref[...]
    lane = lax.broadcasted_iota(jnp.int32, (tb, LANES), 1)
    lanef = lane.astype(F32)
    acc[...] = jnp.zeros_like(acc)

    def group(gi, carry):
        hit = jnp.zeros((tb, LANES), jnp.bool_)
        for c4 in range(COMB_GRP):
            s = gi * COMB_GRP + c4
            e = desc[par, 0, s]
            lo_row = desc[par, 1, s].astype(F32)
            base = desc[par, 2, s].astype(F32)
            col = jnp.sum(jnp.where(lane == e, slot, 0.0), axis=1, keepdims=True)
            colb = jnp.broadcast_to(col, (tb, LANES))
            in_chunk = (lane >= c4 * COMB_CH) & (lane < (c4 + 1) * COMB_CH)
            match = (colb == base + (lanef - float(c4 * COMB_CH))) & (colb >= lo_row)
            hit = hit | (in_chunk & match & (s < n_ch))
        eb = jnp.where(hit, 1.0, 0.0).astype(BF16)
        yg = buf[par, pl.ds(gi * COMB_GRP, COMB_GRP)].reshape(COMB_GRP * COMB_CH, -1)
        y1 = yg.astype(BF16)
        r1 = yg - y1.astype(F32)
        y2 = r1.astype(BF16)
        y3 = (r1 - y2.astype(F32)).astype(BF16)
        acc[...] += (jnp.dot(eb, y1, preferred_element_type=F32)
                     + jnp.dot(eb, y2, preferred_element_type=F32)
                     + jnp.dot(eb, y3, preferred_element_type=F32))
        return carry

    lax.fori_loop(0, (n_ch + COMB_GRP - 1) // COMB_GRP, group, 0)

    x2 = x1_ref[...] + g2_ref[...] * acc[...]
    if final:
        x2 = x2 * lax.rsqrt(jnp.mean(x2 * x2, axis=-1, keepdims=True) + EPS) * fg_ref[...]
    o_ref[...] = x2


def _combine(l, off, cnt, ye, slot, x1, mod5, final_g, row_of_tile, n_e, final):
    n, d = x1.shape
    tb = ROUTE_TB
    assert COMB_GRP * COMB_CH == LANES and COMB_MAXCH % COMB_GRP == 0
    assert COMB_MAXCH >= n_e * -(-(tb + SUBLANES - 1) // COMB_CH)
    grid_spec = pltpu.PrefetchScalarGridSpec(
        num_scalar_prefetch=2,
        grid=(n // tb,),
        in_specs=[
            pl.BlockSpec(memory_space=pl.ANY),
            pl.BlockSpec((tb, LANES), lambda j, o_, c_: (j, 0)),
            pl.BlockSpec((tb, d), lambda j, o_, c_: (j, 0)),
            pl.BlockSpec((None, None, None, 1, d), lambda j, o_, c_: (l, row_of_tile(j), 5, 0, 0)),
            pl.BlockSpec((1, d), lambda j, o_, c_: (0, 0)),
        ],
        out_specs=pl.BlockSpec((tb, d), lambda j, o_, c_: (j, 0)),
        scratch_shapes=[
            pltpu.VMEM((2, COMB_MAXCH, COMB_CH, d), F32),
            pltpu.VMEM((tb, d), F32),
            pltpu.SMEM((2, 4, LANES), jnp.int32),
            pltpu.SemaphoreType.DMA((2,)),
        ],
    )
    return pl.pallas_call(
        functools.partial(_combine_kernel, n_e=n_e, final=final),
        grid_spec=grid_spec,
        out_shape=jax.ShapeDtypeStruct((n, d), F32),
        compiler_params=_cparams(("arbitrary",)),
        name="combine",
    )(off, cnt, ye, slot, x1, mod5, final_g.reshape(1, d))


def _pos_embed_2d(n_tokens, d_model):
    rows = n_tokens // GRID_W
    r = jnp.broadcast_to(jnp.arange(rows, dtype=F32)[:, None], (rows, GRID_W)).reshape(-1)
    col = jnp.broadcast_to(jnp.arange(GRID_W, dtype=F32)[None, :], (rows, GRID_W)).reshape(-1)
    quarter = d_model // 4
    freq = jnp.exp(-math.log(10000.0) * jnp.arange(quarter, dtype=F32) / quarter)
    ar = r[:, None] * freq
    ac = col[:, None] * freq
    return jnp.concatenate([jnp.sin(ar), jnp.cos(ar), jnp.sin(ac), jnp.cos(ac)], axis=-1)


def kernel(x_prompt, x_sample, state_delta, state_lru, c, c_ctx, norm1_g, w_mod, b_mod, w_in, conv_qkv, dn_a_log, dn_dt_bias, dn_norm_g, w_dn_out, conv_lru_w, conv_lru_b, lru_wa, lru_ba, lru_wx, lru_bx, lru_lambda, w_lru_out, w_o, norm2_g, w_router, w_gate, w_up, w_down, final_g):
    bp, tp, d = x_prompt.shape
    bs, ts, _ = x_sample.shape
    depth = w_in.shape[0]
    heads, dk, dv = state_delta.shape[3:]
    qk = heads * dk
    vw = heads * dv
    lru_w = state_lru.shape[-1]
    n_experts = w_router.shape[-1]
    n_p, n_s = bp * tp, bs * ts
    n = n_p + n_s
    cap_p = 2 * n_p // n_experts
    cap_s = 2 * n_s // n_experts
    assert dk == LANES and dv == LANES and n_p % ts == 0 and ts % tp == 0

    xs = x_sample + _pos_embed_2d(ts, d)[None]
    x = jnp.concatenate([x_prompt.reshape(n_p, d), xs.reshape(n_s, d)], axis=0)

    cond8 = jnp.zeros((8, d), F32).at[0].set(c_ctx).at[1:1 + bs].set(c)
    mod = _modulation(cond8, w_mod, b_mod)
    mod = mod.reshape(depth, 8, 6, 1, d)

    n_small = 2 * N_DIR * heads
    c0 = 2 * qk + 2 * vw
    w_main, w_ba = _w_in_prep(w_in, c0, n_small)
    lx_blk = c0 // LANES
    ly_blk = lx_blk + lru_w // LANES
    ga_blk = (c0 + 2 * lru_w) // d
    gb_blk = ga_blk + 1
    w_dn_b = w_dn_out.astype(BF16)
    w_lru_b = w_lru_out.astype(BF16)
    w_o_b = w_o.astype(BF16)
    w_r_b = jnp.pad(w_router, ((0, 0), (0, 0), (0, LANES - n_experts))).astype(BF16)
    wa_b = lru_wa.astype(BF16)
    wx_b = lru_wx.astype(BF16)
    lane_pad = lambda a: jnp.pad(a.reshape(depth, 1, N_DIR * heads), ((0, 0), (0, 0), (0, LANES - N_DIR * heads)))
    al_v = lane_pad(dn_a_log)
    dt_v = lane_pad(dn_dt_bias)
    norm1_3 = norm1_g.reshape(depth, 1, d)
    norm2_3 = norm2_g.reshape(depth, 1, d)
    dn_norm_3 = dn_norm_g.reshape(depth, 1, dv)
    conv_lru_b3 = conv_lru_b.reshape(depth, 1, lru_w)

    tm_in = ts
    tiles_p_in = n_p // tm_in
    row_in = lambda i: jnp.where(i < tiles_p_in, 0, i - tiles_p_in + 1)
    tm_mg = 512
    tiles_p_mg = n_p // tm_mg
    per_seq = ts // tm_mg
    row_mg = lambda i: jnp.where(i < tiles_p_mg, 0, (i - tiles_p_mg) // per_seq + 1)
    tiles_p_cb = n_p // ROUTE_TB
    per_seq_cb = ts // ROUTE_TB
    row_cb = lambda i: jnp.where(i < tiles_p_cb, 0, (i - tiles_p_cb) // per_seq_cb + 1)
    assert n_p % ROUTE_TB == 0 and ts % ROUTE_TB == 0 and n_experts + 2 <= ROUTE_ROWS

    sd_out, sl_out = [], []
    for l in range(depth):
        cols, ba = _in_proj(l, x, norm1_3, mod, w_main, w_ba, row_in, tm_in)
        o_p, sd_p = _gdn_seq(l, cols, ba, conv_qkv, al_v, dt_v, dn_norm_3, None, 0, bp, tp, heads, dk)
        o_s, _ = _gdn_seq(l, cols, ba, conv_qkv, al_v, dt_v, dn_norm_3, state_delta, n_p, bs, ts, heads, dk)
        y_p, sl_p = _lru(l, cols, conv_lru_w, conv_lru_b3, wa_b, wx_b, lru_ba, lru_bx, lru_lambda, None,
                         0, bp, tp, lx_blk, ly_blk)
        y_s, _ = _lru(l, cols, conv_lru_w, conv_lru_b3, wa_b, wx_b, lru_ba, lru_bx, lru_lambda, state_lru,
                      n_p, bs, ts, lx_blk, ly_blk)
        x1, hn_aug, afft = _merge(l, o_p, o_s, y_p, y_s, cols, x, mod, norm2_3, w_dn_b, w_lru_b, w_o_b, w_r_b,
                                  row_mg, tm_mg, ga_blk, gb_blk, n_experts)
        thr_need = _thresholds(afft, ((0, n_p, cap_p), (n_p, n, cap_s)))
        slot, off, cnt, lst = _route(hn_aug, thr_need, n_p, cap_p, n_experts)
        nblk = n // ROUTE_TB
        off = off.reshape(nblk, LANES)
        cnt = cnt.reshape(nblk, LANES)
        xe_p = _gather(off, cnt, lst, hn_aug, 0, n_p, 0, cap_p)
        xe_s = _gather(off, cnt, lst, hn_aug, n_p, n_s, cap_p, cap_s)
        ye = _ffn(l, xe_p, xe_s, w_gate, w_up, w_down, 512)
        x = _combine(l, off, cnt, ye, slot, x1, mod, final_g, row_cb, n_experts, l == depth - 1)
        sd_out.append(sd_p)
        sl_out.append(sl_p)

    y = x
    y_prompt = y[:n_p].reshape(bp, tp, d)
    y_sample = y[n_p:].reshape(bs, ts, d)
    return (y_prompt, y_sample, jnp.stack(sd_out, axis=1), jnp.stack(sl_out, axis=1))
```

```python
import functools
import math

import jax
import jax.numpy as jnp
from jax import lax
from jax.experimental import pallas as pl
from jax.experimental.pallas import tpu as pltpu

F32 = jnp.float32
BF16 = jnp.bfloat16

EPS = 1e-6
CHUNK = 64
SUB = 8
GDN_BATCH = 32
CONV_LEFT = 2
CONV_W = 4
LRU_C = 8.0
LRU_GROUP = 4
N_DIR = 2
GRID_W = 64
LANES = 128
SUBLANES = 8
VMEM_LIMIT = 56 * 1024 * 1024


def _cparams(sem):
    return pltpu.CompilerParams(dimension_semantics=sem, vmem_limit_bytes=VMEM_LIMIT)


def _bdot(a, b):
    return jnp.dot(a.astype(BF16), b.astype(BF16), preferred_element_type=F32)


def _mod_kernel(c_ref, w_ref, b_ref, o_ref):
    c = c_ref[...]
    c = c * jax.nn.sigmoid(c)
    o_ref[...] = _bdot(c, w_ref[...]) + b_ref[...]


def _modulation(cond8, w_mod, b_mod):
    depth, d, n6 = w_mod.shape
    tn = 1536
    return pl.pallas_call(
        _mod_kernel,
        grid=(depth, n6 // tn),
        in_specs=[
            pl.BlockSpec((8, d), lambda l, j: (0, 0)),
            pl.BlockSpec((None, d, tn), lambda l, j: (l, 0, j)),
            pl.BlockSpec((None, 1, tn), lambda l, j: (l, 0, j)),
        ],
        out_specs=pl.BlockSpec((None, 8, tn), lambda l, j: (l, 0, j)),
        out_shape=jax.ShapeDtypeStruct((depth, 8, n6), F32),
        compiler_params=_cparams(("parallel", "parallel")),
        name="modulation",
    )(cond8, w_mod, b_mod.reshape(depth, 1, n6))


def _w_in_prep_kernel(w_ref, main_ref, ba_ref, *, c0, n_small):
    w = w_ref[...]
    rows = w.shape[0]
    main_ref[...] = jnp.concatenate([w[:, :c0], w[:, c0 + n_small:]], axis=1).astype(BF16)
    ba_ref[...] = jnp.concatenate(
        [w[:, c0:c0 + n_small], jnp.zeros((rows, LANES - n_small), F32)], axis=1).astype(BF16)


def _w_in_prep(w_in, c0, n_small):
    depth, d, ncol = w_in.shape
    tk = 256
    return pl.pallas_call(
        functools.partial(_w_in_prep_kernel, c0=c0, n_small=n_small),
        grid=(depth, d // tk),
        in_specs=[pl.BlockSpec((None, tk, ncol), lambda l, i: (l, i, 0))],
        out_specs=[
            pl.BlockSpec((None, tk, ncol - n_small), lambda l, i: (l, i, 0)),
            pl.BlockSpec((None, tk, LANES), lambda l, i: (l, i, 0)),
        ],
        out_shape=[
            jax.ShapeDtypeStruct((depth, d, ncol - n_small), BF16),
            jax.ShapeDtypeStruct((depth, d, LANES), BF16),
        ],
        compiler_params=_cparams(("parallel", "parallel")),
        name="w_in_prep",
    )(w_in)


def _in_proj_kernel(x_ref, g_ref, sh_ref, sc_ref, w_ref, wba_ref, o_ref, oba_ref, hn_scr):
    @pl.when(pl.program_id(1) == 0)
    def _():
        x = x_ref[...]
        y = x * lax.rsqrt(jnp.mean(x * x, axis=-1, keepdims=True) + EPS) * g_ref[...]
        hn = (y * (1.0 + sc_ref[...]) + sh_ref[...]).astype(BF16)
        hn_scr[...] = hn
        oba_ref[...] = jnp.dot(hn, wba_ref[...], preferred_element_type=F32)

    o_ref[...] = jnp.dot(hn_scr[...], w_ref[...], preferred_element_type=F32)


def _in_proj(l, x, norm_g, mod5, w_main, w_ba, row_of_tile, tm):
    n, d = x.shape
    ncols = w_main.shape[2]
    tn = 1536
    return pl.pallas_call(
        _in_proj_kernel,
        grid=(n // tm, ncols // tn),
        in_specs=[
            pl.BlockSpec((tm, d), lambda i, j: (i, 0)),
            pl.BlockSpec((None, 1, d), lambda i, j: (l, 0, 0)),
            pl.BlockSpec((None, None, None, 1, d), lambda i, j: (l, row_of_tile(i), 0, 0, 0)),
            pl.BlockSpec((None, None, None, 1, d), lambda i, j: (l, row_of_tile(i), 1, 0, 0)),
            pl.BlockSpec((None, d, tn), lambda i, j: (l, 0, j)),
            pl.BlockSpec((None, d, LANES), lambda i, j: (l, 0, 0)),
        ],
        out_specs=[
            pl.BlockSpec((tm, tn), lambda i, j: (i, j)),
            pl.BlockSpec((tm, LANES), lambda i, j: (i, 0)),
        ],
        out_shape=[
            jax.ShapeDtypeStruct((n, ncols), F32),
            jax.ShapeDtypeStruct((n, LANES), F32),
        ],
        scratch_shapes=[pltpu.VMEM((tm, d), BF16)],
        compiler_params=_cparams(("parallel", "arbitrary")),
        name="in_proj",
    )(x, norm_g, mod5, mod5, w_main, w_ba)


def _conv_rows(x, w, row):
    t = x.shape[0]
    acc = x * w[CONV_LEFT:CONV_LEFT + 1, :]
    for j in range(CONV_W):
        off = j - CONV_LEFT
        if off == 0:
            continue
        xs = pltpu.roll(x, (-off) % t, axis=0)
        valid = (row + off >= 0) & (row + off < t)
        acc = acc + jnp.where(valid, xs, 0.0) * w[j:j + 1, :]
    return acc


def _softplus(x):
    return jnp.maximum(x, 0.0) + jnp.log1p(jnp.exp(-jnp.abs(x)))


def _bmm(a, b):
    return jnp.einsum('nij,njk->nik', a.astype(BF16), b.astype(BF16), preferred_element_type=F32)


def _bmm_nt(a, b, precision=None):
    return jnp.einsum('nid,njd->nij', a, b, preferred_element_type=F32, precision=precision)


def _unit_tri_inverse(a, ii, jj):
    def same(b):
        return (ii // b) == (jj // b)

    eye = (ii == jj).astype(F32)
    d1 = jnp.where(same(SUB), a, 0.0)
    d2 = _bmm(d1, d1)
    d4 = _bmm(d2, d2)
    x = eye - d1
    x = x + _bmm(x, d2)
    x = x + _bmm(x, d4)
    b = SUB
    while b < CHUNK:
        o = jnp.where(same(2 * b) & jnp.logical_not(same(b)), a, 0.0)
        x = x - _bmm(_bmm(x, o), x)
        b *= 2
    return x


def _gdn_kernel(*refs, t, has_s0, dk):
    if has_s0:
        (q_ref, k_ref, v_ref, z_ref, ba_ref, cq_ref, ck_ref, cv_ref, al_ref, dt_ref, ng_ref,
         oh_ref, s0_ref, o_ref, s_ref) = refs
    else:
        (q_ref, k_ref, v_ref, z_ref, ba_ref, cq_ref, ck_ref, cv_ref, al_ref, dt_ref, ng_ref,
         oh_ref, o_ref, s_ref) = refs
        s0_ref = None
    n = t // CHUNK
    row = lax.broadcasted_iota(jnp.int32, (t, LANES), 0)
    pos = row % CHUNK

    def conv_silu(x_ref, w_ref):
        y = _conv_rows(x_ref[...], w_ref[...], row)
        return y * jax.nn.sigmoid(y)

    def l2n(x):
        return x * lax.rsqrt(jnp.sum(x * x, axis=-1, keepdims=True) + EPS)

    q = l2n(conv_silu(q_ref, cq_ref)) * (dk ** -0.5)
    k = l2n(conv_silu(k_ref, ck_ref))
    v = conv_silu(v_ref, cv_ref)
    q3 = q.reshape(n, CHUNK, LANES).astype(BF16)
    k3 = k.reshape(n, CHUNK, LANES)
    v3 = v.reshape(n, CHUNK, LANES)
    k3b = k3.astype(BF16)
    gram = _bmm_nt(k3b, k3b)
    qk = _bmm_nt(q3, k3b)
    q3 = q.reshape(n, CHUNK, LANES)

    ba = ba_ref[...]
    al = al_ref[...]
    dtb = dt_ref[...]
    ii = lax.broadcasted_iota(jnp.int32, (CHUNK, CHUNK), 0)
    jj = lax.broadcasted_iota(jnp.int32, (CHUNK, CHUNK), 1)
    onehot0 = jnp.broadcast_to(oh_ref[...][None], (n, CHUNK, LANES))

    per_dir = []
    for d in range(N_DIR):
        beta = jax.nn.sigmoid(ba[:, d:d + 1])
        g = -jnp.exp(al[:, d:d + 1]) * _softplus(ba[:, 2 + d:3 + d] + dtb[:, d:d + 1])
        beta_b = jnp.broadcast_to(beta, (t, LANES))
        dec = jnp.broadcast_to(g, (t, LANES))
        step = 1
        while step < CHUNK:
            if d == 0:
                sh = pltpu.roll(dec, step, axis=0)
                dec = dec + jnp.where(pos >= step, sh, 0.0)
            else:
                sh = pltpu.roll(dec, t - step, axis=0)
                dec = dec + jnp.where(pos < CHUNK - step, sh, 0.0)
            step *= 2
        dec3 = dec.reshape(n, CHUNK, LANES)
        tot3 = jnp.broadcast_to(dec3[:, CHUNK - 1:CHUNK, :] if d == 0 else dec3[:, 0:1, :],
                                (n, CHUNK, LANES))
        beta3 = beta_b.reshape(n, CHUNK, LANES)
        dec_row = _bmm_nt(onehot0, dec3, precision=lax.Precision.HIGHEST)
        dec_col = dec3[:, :, :CHUNK]
        tri = (ii >= jj) if d == 0 else (ii <= jj)
        strict = (ii > jj) if d == 0 else (ii < jj)
        gamma = jnp.where(tri, jnp.exp(jnp.where(tri, dec_col - dec_row, 0.0)), 0.0)
        a = jnp.where(strict, gram * gamma, 0.0) * beta3[:, :, :CHUNK]
        x = _unit_tri_inverse(a, ii, jj)
        attn = jnp.where(tri, qk * gamma, 0.0)
        edec = jnp.exp(dec3)
        kbd = k3 * (beta3 * edec)
        vb = v3 * beta3
        qd = q3 * edec
        kd = k3 * jnp.exp(tot3 - dec3)
        cd = jnp.exp(tot3)
        per_dir.append((x.astype(BF16), attn.astype(BF16), kbd.astype(BF16), vb, qd.astype(BF16),
                        kd.astype(BF16), cd))

    states = []
    outs = []
    for d in range(N_DIR):
        if has_s0:
            states.append(s0_ref[d])
        else:
            states.append(jnp.zeros((LANES, LANES), F32))
        outs.append([None] * n)
    for it in range(n):
        for d in range(N_DIR):
            c = it if d == 0 else n - 1 - it
            x, attn, kbd, vb, qd, kd, cd = per_dir[d]
            s = states[d]
            sb = s.astype(BF16)
            r = jnp.dot(kbd[c], sb, preferred_element_type=F32)
            vn = jnp.dot(x[c], (vb[c] - r).astype(BF16), preferred_element_type=F32).astype(BF16)
            outs[d][c] = (jnp.dot(qd[c], sb, preferred_element_type=F32)
                          + jnp.dot(attn[c], vn, preferred_element_type=F32))
            states[d] = s * cd[c][0:1, :] + lax.dot_general(kd[c], vn, (((0,), (0,)), ((), ())),
                                                    preferred_element_type=F32)
    for d in range(N_DIR):
        s_ref[d] = states[d]
    o = jnp.concatenate([outs[0][c] + outs[1][c] for c in range(n)], axis=0)
    o = o * lax.rsqrt(jnp.mean(o * o, axis=-1, keepdims=True) + EPS) * ng_ref[...]
    z = z_ref[...]
    o_ref[...] = (o * (z * jax.nn.sigmoid(z))).astype(o_ref.dtype)


def _gdn(cols, bah, conv_qkv, al, dtb, norm_g, s0, row0, nb, t, heads, dk):
    n = cols.shape[0]
    blk0 = row0 // t
    has_s0 = s0 is not None
    col = lambda off: pl.BlockSpec((t, LANES), lambda b, h: (blk0 + b, off + h))
    cw = lambda off: pl.BlockSpec((CONV_W, LANES), lambda b, h: (0, off + h))
    in_specs = [
        col(0), col(heads), col(2 * heads), col(3 * heads),
        pl.BlockSpec((None, t, 4), lambda b, h: (h, blk0 + b, 0)),
        cw(0), cw(heads), cw(2 * heads),
        pl.BlockSpec((None, 1, N_DIR), lambda b, h: (h, 0, 0)),
        pl.BlockSpec((None, 1, N_DIR), lambda b, h: (h, 0, 0)),
        pl.BlockSpec((1, LANES), lambda b, h: (0, 0)),
        pl.BlockSpec((CHUNK, LANES), lambda b, h: (0, 0)),
    ]
    onehot0 = jnp.zeros((CHUNK, LANES), F32).at[:, 0].set(1.0)
    args = [cols, cols, cols, cols, bah, conv_qkv, conv_qkv, conv_qkv, al, dtb, norm_g.reshape(1, LANES),
            onehot0]
    if has_s0:
        in_specs.append(pl.BlockSpec((None, N_DIR, None, dk, LANES), lambda b, h: (b, 0, h, 0, 0)))
        args.append(s0)
    return pl.pallas_call(
        functools.partial(_gdn_kernel, t=t, has_s0=has_s0, dk=dk),
        grid=(nb, heads),
        in_specs=in_specs,
        out_specs=[
            pl.BlockSpec((t, LANES), lambda b, h: (b, h)),
            pl.BlockSpec((None, N_DIR, None, dk, LANES), lambda b, h: (b, 0, h, 0, 0)),
        ],
        out_shape=[
            jax.ShapeDtypeStruct((nb * t, heads * LANES), BF16),
            jax.ShapeDtypeStruct((nb, N_DIR, heads, dk, LANES), F32),
        ],
        compiler_params=_cparams(("parallel", "parallel")),
        name=f"gdn_t{t}",
    )(*args)


def _gdn_seq_kernel(*refs, t, has_s0, heads, dk, hg):
    n_st = N_DIR * heads
    n_in = 13 if has_s0 else 12
    ins, outs = refs[:n_in], refs[n_in:]
    (q_ref, k_ref, v_ref, z_ref, ba_ref, cq_ref, ck_ref, cv_ref, al_ref, dt_ref, ng_ref, oh_ref) = ins[:12]
    s0_ref = ins[12] if has_s0 else None
    o_ref, s_ref, x_scr, at_scr, kq_scr, kd_scr, vb_scr, cd_scr, o_scr = outs[:9]
    st_scrs = outs[9:9 + n_st]
    n = t // CHUNK
    row = lax.broadcasted_iota(jnp.int32, (t, LANES), 0)
    pos = row % CHUNK
    ii = lax.broadcasted_iota(jnp.int32, (CHUNK, CHUNK), 0)
    jj = lax.broadcasted_iota(jnp.int32, (CHUNK, CHUNK), 1)
    ba = ba_ref[...]
    al = al_ref[...]
    dtb = dt_ref[...]

    def conv_silu(x_ref, w_ref, h):
        y = _conv_rows(x_ref[:, h * LANES:(h + 1) * LANES], w_ref[:, h * LANES:(h + 1) * LANES], row)
        return y * jax.nn.sigmoid(y)

    def l2n(x):
        return x * lax.rsqrt(jnp.sum(x * x, axis=-1, keepdims=True) + EPS)

    nb = hg * n
    ii3 = lax.broadcasted_iota(jnp.int32, (N_DIR * nb, CHUNK, CHUNK), 1)
    jj3 = lax.broadcasted_iota(jnp.int32, (N_DIR * nb, CHUNK, CHUNK), 2)
    fwd = lax.broadcasted_iota(jnp.int32, (N_DIR * nb, CHUNK, CHUNK), 0) < nb
    ahead = jnp.where(fwd, ii3 - jj3, jj3 - ii3)
    tri = ahead >= 0
    strict = ahead > 0
    onehot0 = jnp.broadcast_to(oh_ref[...][None], (N_DIR * nb, CHUNK, LANES))

    for h0 in range(0, heads, hg):
        hs = range(h0, h0 + hg)
        as3 = lambda x: x.reshape(n, CHUNK, LANES)
        q3 = jnp.concatenate([as3(l2n(conv_silu(q_ref, cq_ref, h)) * (dk ** -0.5)) for h in hs], axis=0)
        k3 = jnp.concatenate([as3(l2n(conv_silu(k_ref, ck_ref, h))) for h in hs], axis=0)
        v3 = jnp.concatenate([as3(conv_silu(v_ref, cv_ref, h)) for h in hs], axis=0)
        k3b = k3.astype(BF16)
        gram = _bmm_nt(k3b, k3b)
        qk = _bmm_nt(q3.astype(BF16), k3b)
        decs, tots, betas = [], [], []
        for d in range(N_DIR):
            for h in hs:
                lb = d * heads + h
                la = 2 * heads + lb
                beta = jax.nn.sigmoid(ba[:, lb:lb + 1])
                g = -jnp.exp(al[:, lb:lb + 1]) * _softplus(ba[:, la:la + 1] + dtb[:, lb:lb + 1])
                dec = jnp.broadcast_to(g, (t, LANES))
                step = 1
                while step < CHUNK:
                    if d == 0:
                        sh = pltpu.roll(dec, step, axis=0)
                        dec = dec + jnp.where(pos >= step, sh, 0.0)
                    else:
                        sh = pltpu.roll(dec, t - step, axis=0)
                        dec = dec + jnp.where(pos < CHUNK - step, sh, 0.0)
                    step *= 2
                dec3 = as3(dec)
                decs.append(dec3)
                tots.append(jnp.broadcast_to(dec3[:, CHUNK - 1:CHUNK, :] if d == 0 else dec3[:, 0:1, :],
                                             (n, CHUNK, LANES)))
                betas.append(as3(jnp.broadcast_to(beta, (t, LANES))))
        dec3 = jnp.concatenate(decs, axis=0)
        tot3 = jnp.concatenate(tots, axis=0)
        beta3 = jnp.concatenate(betas, axis=0)
        both = lambda x: jnp.concatenate([x, x], axis=0)
        dec_row = _bmm_nt(onehot0, dec3, precision=lax.Precision.HIGHEST)
        dec_col = dec3[:, :, :CHUNK]
        gamma = jnp.where(tri, jnp.exp(jnp.where(tri, dec_col - dec_row, 0.0)), 0.0)
        a = jnp.where(strict, both(gram) * gamma, 0.0) * beta3[:, :, :CHUNK]
        xinv = _unit_tri_inverse(a, ii, jj).astype(BF16)
        attn = jnp.where(tri, both(qk) * gamma, 0.0).astype(BF16)
        edec = jnp.exp(dec3)
        k32, q32, v32 = both(k3), both(q3), both(v3)
        kbd = (k32 * (beta3 * edec)).astype(BF16)
        qd = (q32 * edec).astype(BF16)
        kd = (k32 * jnp.exp(tot3 - dec3)).astype(BF16)
        vb = v32 * beta3
        cd = jnp.exp(tot3[:, :SUBLANES, :])
        for d in range(N_DIR):
            src = slice(d * nb, (d + 1) * nb)
            dst = pl.ds((d * heads + h0) * n, nb)
            x_scr[dst] = xinv[src]
            at_scr[dst] = attn[src]
            kq_scr[dst, :CHUNK, :] = kbd[src]
            kq_scr[dst, CHUNK:, :] = qd[src]
            kd_scr[dst] = kd[src]
            vb_scr[dst] = vb[src]
            cd_scr[dst] = cd[src]

    for d in range(N_DIR):
        for h in range(heads):
            if has_s0:
                st_scrs[d * heads + h][...] = s0_ref[d, h]
            else:
                st_scrs[d * heads + h][...] = jnp.zeros((dk, LANES), F32)

    def scan(it, carry):
        for h in range(heads):
            for d in range(N_DIR):
                st = st_scrs[d * heads + h]
                b = (d * heads + h) * n + (it if d == 0 else n - 1 - it)
                s = st[...]
                sb = s.astype(BF16)
                rq = jnp.dot(kq_scr[b], sb, preferred_element_type=F32)
                vn = jnp.dot(x_scr[b], (vb_scr[b] - rq[:CHUNK]).astype(BF16),
                             preferred_element_type=F32).astype(BF16)
                o_scr[b] = rq[CHUNK:] + jnp.dot(at_scr[b], vn, preferred_element_type=F32)
                st[...] = s * cd_scr[b][0:1, :] + lax.dot_general(
                    kd_scr[b], vn, (((0,), (0,)), ((), ())), preferred_element_type=F32)
        return carry

    lax.fori_loop(0, n, scan, 0)

    for h in range(heads):
        for d in range(N_DIR):
            s_ref[d, h] = st_scrs[d * heads + h][...]
        of = o_scr[pl.ds(h * n, n)]
        ob = o_scr[pl.ds((heads + h) * n, n)]
        o = (of + ob).reshape(t, LANES)
        o = o * lax.rsqrt(jnp.mean(o * o, axis=-1, keepdims=True) + EPS) * ng_ref[...]
        z = z_ref[:, h * LANES:(h + 1) * LANES]
        o_ref[:, h * LANES:(h + 1) * LANES] = (o * (z * jax.nn.sigmoid(z))).astype(o_ref.dtype)


def _gdn_seq(l, cols, ba, conv_qkv, al, dtb, norm_g, s0, row0, nb, t, heads, dk):
    blk0 = row0 // t
    has_s0 = s0 is not None
    hw = heads * LANES
    n = t // CHUNK
    col = lambda off: pl.BlockSpec((t, hw), lambda b: (blk0 + b, off))
    cw = lambda off: pl.BlockSpec((None, CONV_W, hw), lambda b: (l, 0, off))
    vec = pl.BlockSpec((None, 1, LANES), lambda b: (l, 0, 0))
    in_specs = [
        col(0), col(1), col(2), col(3),
        pl.BlockSpec((t, LANES), lambda b: (blk0 + b, 0)),
        cw(0), cw(1), cw(2),
        vec, vec, vec,
        pl.BlockSpec((CHUNK, LANES), lambda b: (0, 0)),
    ]
    onehot0 = jnp.zeros((CHUNK, LANES), F32).at[:, 0].set(1.0)
    args = [cols, cols, cols, cols, ba, conv_qkv, conv_qkv, conv_qkv, al, dtb, norm_g, onehot0]
    if has_s0:
        in_specs.append(pl.BlockSpec((None, None, N_DIR, heads, dk, LANES), lambda b: (b, l, 0, 0, 0, 0)))
        args.append(s0)
    per = (N_DIR * heads * n,)
    hg = max(1, min(heads, GDN_BATCH // (N_DIR * n)))
    assert heads % hg == 0
    return pl.pallas_call(
        functools.partial(_gdn_seq_kernel, t=t, has_s0=has_s0, heads=heads, dk=dk, hg=hg),
        grid=(nb,),
        in_specs=in_specs,
        out_specs=[
            pl.BlockSpec((t, hw), lambda b: (b, 0)),
            pl.BlockSpec((None, N_DIR, heads, dk, LANES), lambda b: (b, 0, 0, 0, 0)),
        ],
        out_shape=[
            jax.ShapeDtypeStruct((nb * t, hw), BF16),
            jax.ShapeDtypeStruct((nb, N_DIR, heads, dk, LANES), F32),
        ],
        scratch_shapes=[
            pltpu.VMEM(per + (CHUNK, CHUNK), BF16),
            pltpu.VMEM(per + (CHUNK, CHUNK), BF16),
            pltpu.VMEM(per + (2 * CHUNK, LANES), BF16),
            pltpu.VMEM(per + (CHUNK, LANES), BF16),
            pltpu.VMEM(per + (CHUNK, LANES), F32),
            pltpu.VMEM(per + (SUBLANES, LANES), F32),
            pltpu.VMEM(per + (CHUNK, LANES), F32),
        ] + [pltpu.VMEM((dk, LANES), F32)] * (N_DIR * heads),
        compiler_params=_cparams(("parallel",)),
        name=f"gdn_t{t}",
    )(*args)


def _lru_kernel(*refs, t, has_h0):
    if has_h0:
        (lx_ref, ly_ref, cw_ref, cb_ref, wa_ref, wx_ref, ba_ref, bx_ref, lam_ref, h0_ref,
         y_ref, last_ref, a_scr, b_scr, h_scr) = refs
    else:
        (lx_ref, ly_ref, cw_ref, cb_ref, wa_ref, wx_ref, ba_ref, bx_ref, lam_ref,
         y_ref, last_ref, a_scr, b_scr, h_scr) = refs
        h0_ref = None
    nblk = t // SUBLANES
    wl = lx_ref.shape[1]
    row = lax.broadcasted_iota(jnp.int32, (t, wl), 0)
    sub = lax.broadcasted_iota(jnp.int32, (nblk, SUBLANES, wl), 1)
    u = _conv_rows(lx_ref[...], cw_ref[...], row) + cb_ref[...]
    ub = u.astype(BF16)

    def block_diag(w_ref, d):
        return jnp.concatenate(
            [jnp.dot(ub[:, c * LANES:(c + 1) * LANES], w_ref[d, c], preferred_element_type=F32)
             for c in range(wl // LANES)], axis=1)

    for d in range(N_DIR):
        r = jax.nn.sigmoid(block_diag(wa_ref, d) + ba_ref[d:d + 1, :])
        i = jax.nn.sigmoid(block_diag(wx_ref, d) + bx_ref[d:d + 1, :])
        log_a = (LRU_C * r) * (-_softplus(-lam_ref[d:d + 1, :]))
        a = jnp.exp(log_a)
        b = jnp.sqrt(1.0 - a * a) * (i * u)
        a3 = a.reshape(nblk, SUBLANES, wl)
        b3 = b.reshape(nblk, SUBLANES, wl)
        step = 1
        while step < SUBLANES:
            if d == 0:
                a_s = pltpu.roll(a3, step, axis=1)
                b_s = pltpu.roll(b3, step, axis=1)
                m = sub >= step
            else:
                a_s = pltpu.roll(a3, SUBLANES - step, axis=1)
                b_s = pltpu.roll(b3, SUBLANES - step, axis=1)
                m = sub < SUBLANES - step
            b3 = b3 + a3 * jnp.where(m, b_s, 0.0)
            a3 = a3 * jnp.where(m, a_s, 1.0)
            step *= 2
        a_scr[d] = a3
        b_scr[d] = b3

    if has_h0:
        h_init = (jnp.broadcast_to(h0_ref[0:1, :], (SUBLANES, wl)),
                  jnp.broadcast_to(h0_ref[1:2, :], (SUBLANES, wl)))
    else:
        h_init = (jnp.zeros((SUBLANES, wl), F32), jnp.zeros((SUBLANES, wl), F32))

    def body(it, carry):
        hf, hb = carry
        kf = it
        kb = nblk - 1 - it
        new_f = b_scr[0, kf] + a_scr[0, kf] * hf
        new_b = b_scr[1, kb] + a_scr[1, kb] * hb
        h_scr[0, kf] = new_f
        h_scr[1, kb] = new_b
        hf = jnp.broadcast_to(new_f[SUBLANES - 1:SUBLANES, :], (SUBLANES, wl))
        hb = jnp.broadcast_to(new_b[0:1, :], (SUBLANES, wl))
        return hf, hb

    hf, hb = lax.fori_loop(0, nblk, body, h_init, unroll=4)
    last_ref[0:1, :] = hf[0:1, :]
    last_ref[1:2, :] = hb[0:1, :]
    rec = (h_scr[0] + h_scr[1]).reshape(t, wl)
    y_ref[...] = (jax.nn.gelu(ly_ref[...]) * rec).astype(y_ref.dtype)


def _lru(l, cols, conv_w, conv_b, wa, wx, ba, bx, lam, h0, row0, nb, t, lx_blk, ly_blk):
    nblocks = wa.shape[2]
    w = nblocks * LANES
    cg = LRU_GROUP
    wl = cg * LANES
    blk0 = row0 // t
    has_h0 = h0 is not None
    in_specs = [
        pl.BlockSpec((t, wl), lambda b, c: (blk0 + b, lx_blk // cg + c)),
        pl.BlockSpec((t, wl), lambda b, c: (blk0 + b, ly_blk // cg + c)),
        pl.BlockSpec((None, CONV_W, wl), lambda b, c: (l, 0, c)),
        pl.BlockSpec((None, 1, wl), lambda b, c: (l, 0, c)),
        pl.BlockSpec((None, N_DIR, cg, LANES, LANES), lambda b, c: (l, 0, c, 0, 0)),
        pl.BlockSpec((None, N_DIR, cg, LANES, LANES), lambda b, c: (l, 0, c, 0, 0)),
        pl.BlockSpec((None, N_DIR, wl), lambda b, c: (l, 0, c)),
        pl.BlockSpec((None, N_DIR, wl), lambda b, c: (l, 0, c)),
        pl.BlockSpec((None, N_DIR, wl), lambda b, c: (l, 0, c)),
    ]
    args = [cols, cols, conv_w, conv_b, wa, wx, ba, bx, lam]
    if has_h0:
        in_specs.append(pl.BlockSpec((None, None, N_DIR, wl), lambda b, c: (b, l, 0, c)))
        args.append(h0)
    nblk = t // SUBLANES
    assert lx_blk % cg == 0 and ly_blk % cg == 0 and nblocks % cg == 0
    return pl.pallas_call(
        functools.partial(_lru_kernel, t=t, has_h0=has_h0),
        grid=(nb, nblocks // cg),
        in_specs=in_specs,
        out_specs=[
            pl.BlockSpec((t, wl), lambda b, c: (b, c)),
            pl.BlockSpec((None, N_DIR, wl), lambda b, c: (b, 0, c)),
        ],
        out_shape=[
            jax.ShapeDtypeStruct((nb * t, w), BF16),
            jax.ShapeDtypeStruct((nb, N_DIR, w), F32),
        ],
        scratch_shapes=[pltpu.VMEM((N_DIR, nblk, SUBLANES, wl), F32)] * 3,
        compiler_params=_cparams(("parallel", "parallel")),
        name=f"lru_t{t}",
    )(*args)


def _merge_kernel(oap_ref, oas_ref, obp_ref, obs_ref, ga_ref, gb_ref, x_ref, g1_ref, sh2_ref, sc2_ref,
                  n2_ref, wdn_ref, wlru_ref, wo_ref, wr_ref, x1_ref, hn_ref, afft_ref, *, n_experts, tiles_p):
    d = x_ref.shape[1]
    is_p = pl.program_id(0) < tiles_p
    oa = jnp.where(is_p, oap_ref[...], oas_ref[...])
    ob = jnp.where(is_p, obp_ref[...], obs_ref[...])
    ya = jnp.dot(oa, wdn_ref[...], preferred_element_type=F32)
    yb = jnp.dot(ob, wlru_ref[...], preferred_element_type=F32)
    mix = jax.nn.sigmoid(ga_ref[...]) * ya + jax.nn.sigmoid(gb_ref[...]) * yb
    mix = jnp.dot(mix.astype(BF16), wo_ref[...], preferred_element_type=F32)
    x1 = x_ref[...] + g1_ref[...] * mix
    x1_ref[...] = x1
    y = x1 * lax.rsqrt(jnp.mean(x1 * x1, axis=-1, keepdims=True) + EPS) * n2_ref[...]
    hn = y * (1.0 + sc2_ref[...]) + sh2_ref[...]
    logits = jnp.dot(hn.astype(BF16), wr_ref[...], preferred_element_type=F32)
    lane = lax.broadcasted_iota(jnp.int32, logits.shape, 1)
    logits = jnp.where(lane < n_experts, logits, -jnp.inf)
    e = jnp.exp(logits - jnp.max(logits, axis=-1, keepdims=True))
    aff = e / jnp.sum(e, axis=-1, keepdims=True)
    hn_ref[:, :d] = hn
    hn_ref[:, d:] = aff
    afft_ref[...] = aff.T[:n_experts, :]


def _merge(l, o_p, o_s, y_p, y_s, cols, x, mod5, norm2_g, w_dn, w_lru, w_o, w_r, row_of_tile, tm,
           ga_blk, gb_blk, n_experts):
    n, d = x.shape
    dv = o_p.shape[1]
    w = y_p.shape[1]
    tiles_p = o_p.shape[0] // tm
    modspec = lambda k: pl.BlockSpec((None, None, None, 1, d), lambda i: (l, row_of_tile(i), k, 0, 0))
    layer = lambda a: pl.BlockSpec((None,) + a.shape[1:], lambda i: (l,) + (0,) * (a.ndim - 1))
    p_tile = lambda width: pl.BlockSpec((tm, width), lambda i: (jnp.minimum(i, tiles_p - 1), 0))
    s_tile = lambda width: pl.BlockSpec((tm, width), lambda i: (jnp.maximum(i - tiles_p, 0), 0))
    return pl.pallas_call(
        functools.partial(_merge_kernel, n_experts=n_experts, tiles_p=tiles_p),
        grid=(n // tm,),
        in_specs=[
            p_tile(dv), s_tile(dv), p_tile(w), s_tile(w),
            pl.BlockSpec((tm, d), lambda i: (i, ga_blk)),
            pl.BlockSpec((tm, d), lambda i: (i, gb_blk)),
            pl.BlockSpec((tm, d), lambda i: (i, 0)),
            modspec(2), modspec(3), modspec(4),
            layer(norm2_g), layer(w_dn), layer(w_lru), layer(w_o), layer(w_r),
        ],
        out_specs=[
            pl.BlockSpec((tm, d), lambda i: (i, 0)),
            pl.BlockSpec((tm, d + LANES), lambda i: (i, 0)),
            pl.BlockSpec((n_experts, tm), lambda i: (0, i)),
        ],
        out_shape=[
            jax.ShapeDtypeStruct((n, d), F32),
            jax.ShapeDtypeStruct((n, d + LANES), F32),
            jax.ShapeDtypeStruct((n_experts, n), F32),
        ],
        compiler_params=_cparams(("parallel",)),
        name="merge",
    )(o_p, o_s, y_p, y_s, cols, cols, x, mod5, mod5, mod5, norm2_g, w_dn, w_lru, w_o, w_r)


def _ffn_kernel(xp_ref, xs_ref, wg_ref, wu_ref, wd_ref, o_ref, xb_scr):
    f = pl.program_id(1)
    cap_p = xp_ref.shape[0]
    d = xb_scr.shape[1]

    @pl.when(f == 0)
    def _():
        xb_scr[:cap_p, :] = xp_ref[:, :d].astype(BF16)
        xb_scr[cap_p:, :] = xs_ref[:, :d].astype(BF16)

    x = xb_scr[...]
    g = jnp.dot(x, wg_ref[...].astype(BF16), preferred_element_type=F32)
    u = jnp.dot(x, wu_ref[...].astype(BF16), preferred_element_type=F32)
    hid = ((g * jax.nn.sigmoid(g)) * u).astype(BF16)
    y = jnp.dot(hid, wd_ref[...].astype(BF16), preferred_element_type=F32)

    @pl.when(f == 0)
    def _():
        o_ref[...] = y

    @pl.when(f > 0)
    def _():
        o_ref[...] += y

    @pl.when(f == pl.num_programs(1) - 1)
    def _():
        e = pl.program_id(0)
        for ref, r0 in ((xp_ref, 0), (xs_ref, cap_p)):
            aff = ref[:, d:]
            lane = lax.broadcasted_iota(jnp.int32, aff.shape, 1)
            gv = jnp.sum(jnp.where(lane == e, aff, 0.0), axis=1, keepdims=True)
            rows = ref.shape[0]
            o_ref[r0:r0 + rows, :] = o_ref[r0:r0 + rows, :] * gv


def _ffn(l, xe_p, xe_s, w_gate, w_up, w_down, tf):
    e, cap_p, da = xe_p.shape
    d = da - LANES
    cap_s = xe_s.shape[1]
    r = cap_p + cap_s
    ff = w_gate.shape[3]
    return pl.pallas_call(
        _ffn_kernel,
        grid=(e, ff // tf),
        in_specs=[
            pl.BlockSpec((None, cap_p, da), lambda i, f: (i, 0, 0)),
            pl.BlockSpec((None, cap_s, da), lambda i, f: (i, 0, 0)),
            pl.BlockSpec((None, None, d, tf), lambda i, f: (l, i, 0, f)),
            pl.BlockSpec((None, None, d, tf), lambda i, f: (l, i, 0, f)),
            pl.BlockSpec((None, None, tf, d), lambda i, f: (l, i, f, 0)),
        ],
        out_specs=pl.BlockSpec((None, r, d), lambda i, f: (i, 0, 0)),
        out_shape=jax.ShapeDtypeStruct((e, r, d), F32),
        scratch_shapes=[pltpu.VMEM((r, d), BF16)],
        compiler_params=_cparams(("parallel", "arbitrary")),
        name="expert_ffn",
    )(xe_p, xe_s, w_gate, w_up, w_down)


ROUTE_TB = 128
ROUTE_ROWS = 24


def _thr_kernel(afft_ref, o_ref, *, groups):
    n_e = afft_ref.shape[0]
    sub = lax.broadcasted_iota(jnp.int32, (n_e, LANES), 0)
    lane = lax.broadcasted_iota(jnp.int32, (n_e, LANES), 1)
    rows = []
    for lo, hi, cap in groups:
        a = afft_ref[:, lo:hi]
        above_all = 4.0

        def count_ge(v, a=a):
            return jnp.sum((a >= v).astype(F32), axis=1, keepdims=True)

        def count_gt(v, a=a):
            return jnp.sum((a > v).astype(F32), axis=1, keepdims=True)

        def bisect(i, lh, cap=cap, count_ge=count_ge):
            lo_v, hi_v = lh
            mid = 0.5 * (lo_v + hi_v)
            ok = count_ge(mid) >= cap
            return jnp.where(ok, mid, lo_v), jnp.where(ok, hi_v, mid)

        lo_v, _ = lax.fori_loop(0, 48, bisect, (jnp.zeros((n_e, 1), F32), jnp.full((n_e, 1), 2.0, F32)))
        thr = jnp.min(jnp.where(a >= lo_v, a, above_all), axis=1, keepdims=True)

        def not_done(thr, cap=cap, count_gt=count_gt):
            return jnp.max(count_gt(thr)) >= cap

        def step_up(thr, a=a, cap=cap, count_gt=count_gt):
            nxt = jnp.min(jnp.where(a > thr, a, above_all), axis=1, keepdims=True)
            return jnp.where(count_gt(thr) >= cap, nxt, thr)

        thr = lax.while_loop(not_done, step_up, thr)
        need = cap - count_gt(thr)
        for col in (thr, need):
            m = jnp.where(sub == lane, jnp.broadcast_to(col, (n_e, LANES)), 0.0)
            rows.append(jnp.sum(m, axis=0, keepdims=True))
    rows.append(jnp.zeros((SUBLANES - len(rows), LANES), F32))
    o_ref[...] = jnp.concatenate(rows, axis=0)


def _thresholds(afft, groups):
    n_e, n = afft.shape
    return pl.pallas_call(
        functools.partial(_thr_kernel, groups=groups),
        grid=(1,),
        in_specs=[pl.BlockSpec((n_e, n), lambda i: (0, 0))],
        out_specs=pl.BlockSpec((SUBLANES, LANES), lambda i: (0, 0)),
        out_shape=jax.ShapeDtypeStruct((SUBLANES, LANES), F32),
        compiler_params=_cparams(("arbitrary",)),
        name="route_thresholds",
    )(afft)


def _route_kernel(aff_ref, tn_ref, ls_ref, slot_ref, off_ref, cnt_ref, lst_ref,
                  run_eq, run_sel, *, nblk_p, cap_p, n_e):
    tb = ROUTE_TB
    j = pl.program_id(0)

    @pl.when(j == 0)
    def _():
        run_eq[...] = jnp.zeros_like(run_eq)
        run_sel[...] = jnp.zeros_like(run_sel)

    @pl.when(j == nblk_p)
    def _():
        run_eq[...] = jnp.zeros_like(run_eq)
        run_sel[...] = jnp.full_like(run_sel, float(cap_p))

    g = (j >= nblk_p).astype(jnp.int32)
    thr = tn_ref[pl.ds(2 * g, 1), :]
    need = tn_ref[pl.ds(2 * g + 1, 1), :]
    a = aff_ref[...]
    lane = lax.broadcasted_iota(jnp.int32, (tb, LANES), 1)
    valid = lane < n_e
    gt = (a > thr) & valid
    eq = (a == thr) & valid
    eqf = eq.astype(F32)
    ls = ls_ref[...]
    eq_rank = run_eq[...] + jnp.dot(ls, eqf.astype(BF16), preferred_element_type=F32)
    sel = gt | (eq & (eq_rank < need))
    self_ = sel.astype(F32)
    pos = jnp.dot(ls, self_.astype(BF16), preferred_element_type=F32)
    cnt = jnp.sum(self_, axis=0, keepdims=True)
    off = run_sel[...]
    run_eq[...] = run_eq[...] + jnp.sum(eqf, axis=0, keepdims=True)
    run_sel[...] = off + cnt
    slot_ref[...] = jnp.where(sel, off + pos, -1.0)
    off_ref[...] = off.astype(jnp.int32)
    cnt_ref[...] = cnt.astype(jnp.int32)

    tok = (lax.broadcasted_iota(jnp.int32, (tb, LANES), 0) + j * tb).astype(F32)
    lanef = lane.astype(F32)
    for e in range(n_e):
        pe = jnp.broadcast_to(pos[:, e:e + 1], (tb, LANES))
        se = jnp.broadcast_to(self_[:, e:e + 1], (tb, LANES))
        hit = (pe == lanef) & (se > 0.0)
        lst_ref[e] = jnp.sum(jnp.where(hit, tok, 0.0), axis=0, keepdims=True).astype(jnp.int32)


def _route(hn_aug, thr_need, n_p, cap_p, n_e):
    n = hn_aug.shape[0]
    aff_blk = hn_aug.shape[1] // LANES - 1
    tb = ROUTE_TB
    nblk = n // tb
    ls = jnp.tril(jnp.ones((tb, tb), F32), -1).astype(BF16)
    blk_row = pl.BlockSpec((None, 1, LANES), lambda j: (j, 0, 0))
    return pl.pallas_call(
        functools.partial(_route_kernel, nblk_p=n_p // tb, cap_p=cap_p, n_e=n_e),
        grid=(nblk,),
        in_specs=[
            pl.BlockSpec((tb, LANES), lambda j: (j, aff_blk)),
            pl.BlockSpec((SUBLANES, LANES), lambda j: (0, 0)),
            pl.BlockSpec((tb, tb), lambda j: (0, 0)),
        ],
        out_specs=[
            pl.BlockSpec((tb, LANES), lambda j: (j, 0)),
            blk_row,
            blk_row,
            pl.BlockSpec((n_e, None, 1, LANES), lambda j: (0, j, 0, 0)),
        ],
        out_shape=[
            jax.ShapeDtypeStruct((n, LANES), F32),
            jax.ShapeDtypeStruct((nblk, 1, LANES), jnp.int32),
            jax.ShapeDtypeStruct((nblk, 1, LANES), jnp.int32),
            jax.ShapeDtypeStruct((n_e, nblk, 1, LANES), jnp.int32),
        ],
        scratch_shapes=[pltpu.VMEM((1, LANES), F32), pltpu.VMEM((1, LANES), F32)],
        compiler_params=_cparams(("arbitrary",)),
        name="route_slots",
    )(hn_aug, thr_need, ls)


GATHER_UNROLL = 4


def _gather_kernel(off_ref, cnt_ref, lst_hbm, hn_hbm, o_ref, hn_scr, lst_smem, sem_h, sem_l,
                   *, row0, slot0, blk0, nblk_g):
    e = pl.program_id(0)
    load_lst = pltpu.make_async_copy(lst_hbm.at[e, pl.ds(blk0, nblk_g)], lst_smem, sem_l)
    load_lst.start()

    @pl.when(e == 0)
    def _():
        rows = hn_scr.shape[0]
        load = pltpu.make_async_copy(hn_hbm.at[pl.ds(row0, rows)], hn_scr, sem_h)
        load.start()
        load.wait()

    load_lst.wait()

    def block(jb, carry):
        c = cnt_ref[blk0 + jb, e]
        o = off_ref[blk0 + jb, e] - slot0

        def copy_row(q):
            t = lst_smem[jb, q] - row0
            o_ref[pl.ds(o + q, 1), :] = hn_scr[pl.ds(t, 1), :]

        def rows(qq, carry2):
            for u in range(GATHER_UNROLL):
                copy_row(qq * GATHER_UNROLL + u)
            return carry2

        def row(q, carry2):
            copy_row(q)
            return carry2

        full = c // GATHER_UNROLL
        lax.fori_loop(0, full, rows, 0)
        lax.fori_loop(full * GATHER_UNROLL, c, row, 0)
        return carry

    lax.fori_loop(0, nblk_g, block, 0)


def _gather(off, cnt, lst, hn, row0, rows, slot0, cap):
    n_e = lst.shape[0]
    width = hn.shape[1]
    blk0 = row0 // ROUTE_TB
    nblk_g = rows // ROUTE_TB
    grid_spec = pltpu.PrefetchScalarGridSpec(
        num_scalar_prefetch=2,
        grid=(n_e,),
        in_specs=[pl.BlockSpec(memory_space=pl.ANY), pl.BlockSpec(memory_space=pl.ANY)],
        out_specs=pl.BlockSpec((None, cap, width), lambda e, o_, c_: (e, 0, 0)),
        scratch_shapes=[
            pltpu.VMEM((rows, width), F32),
            pltpu.SMEM((nblk_g, LANES), jnp.int32),
            pltpu.SemaphoreType.DMA(()),
            pltpu.SemaphoreType.DMA(()),
        ],
    )
    return pl.pallas_call(
        functools.partial(_gather_kernel, row0=row0, slot0=slot0, blk0=blk0, nblk_g=nblk_g),
        grid_spec=grid_spec,
        out_shape=jax.ShapeDtypeStruct((n_e, cap, width), F32),
        compiler_params=_cparams(("arbitrary",)),
        name=f"gather_rows{rows}",
    )(off, cnt, lst.reshape(n_e, -1, LANES), hn)


COMB_CH = 32
COMB_GRP = 4
COMB_MAXCH = 80


def _combine_kernel(off_ref, cnt_ref, y_hbm, slot_ref, x1_ref, g2_ref, fg_ref, o_ref,
                    buf, acc, desc, sems, *, n_e, final):
    tb = ROUTE_TB
    j = pl.program_id(0)
    nblk = pl.num_programs(0)
    r_total = y_hbm.shape[1]
    par = j % 2

    def issue(jj, p):
        s = jnp.int32(0)
        for e in range(n_e):
            o = off_ref[jj, e]
            c = cnt_ref[jj, e]
            st8 = (o // SUBLANES) * SUBLANES
            nch = jnp.where(c > 0, (o - st8 + c + COMB_CH - 1) // COMB_CH, 0)

            def one(k, s, e=e, st8=st8):
                lo_row = st8 + k * COMB_CH
                base = jnp.minimum(lo_row, r_total - COMB_CH)
                pltpu.make_async_copy(y_hbm.at[e, pl.ds(base, COMB_CH)], buf.at[p, s], sems.at[p]).start()
                desc[p, 0, s] = e
                desc[p, 1, s] = lo_row
                desc[p, 2, s] = base
                return s + 1

            s = lax.fori_loop(0, nch, one, s)
        desc[p, 3, 0] = s

    @pl.when(j == 0)
    def _():
        buf[...] = jnp.zeros_like(buf)

        def clear(i, carry):
            for p in range(2):
                for row in range(4):
                    desc[p, row, i] = 0
            return carry

        lax.fori_loop(0, LANES, clear, 0)
        issue(j, par)

    @pl.when(j + 1 < nblk)
    def _():
        issue(j + 1, 1 - par)

    n_ch = desc[par, 3, 0]

    def drain(k, carry):
        pltpu.make_async_copy(y_hbm.at[0, pl.ds(0, COMB_CH)], buf.at[par, k], sems.at[par]).wait()
        return carry

    lax.fori_loop(0, n_ch, drain, 0)

    slot = slot_ref[...]
    lane = lax.broadcasted_iota(jnp.int32, (tb, LANES), 1)
    lanef = lane.astype(F32)
    acc[...] = jnp.zeros_like(acc)

    def group(gi, carry):
        hit = jnp.zeros((tb, LANES), jnp.bool_)
        for c4 in range(COMB_GRP):
            s = gi * COMB_GRP + c4
            e = desc[par, 0, s]
            lo_row = desc[par, 1, s].astype(F32)
            base = desc[par, 2, s].astype(F32)
            col = jnp.sum(jnp.where(lane == e, slot, 0.0), axis=1, keepdims=True)
            colb = jnp.broadcast_to(col, (tb, LANES))
            in_chunk = (lane >= c4 * COMB_CH) & (lane < (c4 + 1) * COMB_CH)
            match = (colb == base + (lanef - float(c4 * COMB_CH))) & (colb >= lo_row)
            hit = hit | (in_chunk & match & (s < n_ch))
        eb = jnp.where(hit, 1.0, 0.0).astype(BF16)
        yg = buf[par, pl.ds(gi * COMB_GRP, COMB_GRP)].reshape(COMB_GRP * COMB_CH, -1)
        y1 = yg.astype(BF16)
        r1 = yg - y1.astype(F32)
        y2 = r1.astype(BF16)
        y3 = (r1 - y2.astype(F32)).astype(BF16)
        acc[...] += jnp.dot(jnp.concatenate([eb, eb, eb], axis=1), jnp.concatenate([y1, y2, y3], axis=0),
                            preferred_element_type=F32)
        return carry

    lax.fori_loop(0, (n_ch + COMB_GRP - 1) // COMB_GRP, group, 0)

    x2 = x1_ref[...] + g2_ref[...] * acc[...]
    if final:
        x2 = x2 * lax.rsqrt(jnp.mean(x2 * x2, axis=-1, keepdims=True) + EPS) * fg_ref[...]
    o_ref[...] = x2


def _combine(l, off, cnt, ye, slot, x1, mod5, final_g, row_of_tile, n_e, final):
    n, d = x1.shape
    tb = ROUTE_TB
    assert COMB_GRP * COMB_CH == LANES and COMB_MAXCH % COMB_GRP == 0
    assert COMB_MAXCH >= n_e * -(-(tb + SUBLANES - 1) // COMB_CH)
    grid_spec = pltpu.PrefetchScalarGridSpec(
        num_scalar_prefetch=2,
        grid=(n // tb,),
        in_specs=[
            pl.BlockSpec(memory_space=pl.ANY),
            pl.BlockSpec((tb, LANES), lambda j, o_, c_: (j, 0)),
            pl.BlockSpec((tb, d), lambda j, o_, c_: (j, 0)),
            pl.BlockSpec((None, None, None, 1, d), lambda j, o_, c_: (l, row_of_tile(j), 5, 0, 0)),
            pl.BlockSpec((1, d), lambda j, o_, c_: (0, 0)),
        ],
        out_specs=pl.BlockSpec((tb, d), lambda j, o_, c_: (j, 0)),
        scratch_shapes=[
            pltpu.VMEM((2, COMB_MAXCH, COMB_CH, d), F32),
            pltpu.VMEM((tb, d), F32),
            pltpu.SMEM((2, 4, LANES), jnp.int32),
            pltpu.SemaphoreType.DMA((2,)),
        ],
    )
    return pl.pallas_call(
        functools.partial(_combine_kernel, n_e=n_e, final=final),
        grid_spec=grid_spec,
        out_shape=jax.ShapeDtypeStruct((n, d), F32),
        compiler_params=_cparams(("arbitrary",)),
        name="combine",
    )(off, cnt, ye, slot, x1, mod5, final_g.reshape(1, d))


def _pos_embed_2d(n_tokens, d_model):
    rows = n_tokens // GRID_W
    r = jnp.broadcast_to(jnp.arange(rows, dtype=F32)[:, None], (rows, GRID_W)).reshape(-1)
    col = jnp.broadcast_to(jnp.arange(GRID_W, dtype=F32)[None, :], (rows, GRID_W)).reshape(-1)
    quarter = d_model // 4
    freq = jnp.exp(-math.log(10000.0) * jnp.arange(quarter, dtype=F32) / quarter)
    ar = r[:, None] * freq
    ac = col[:, None] * freq
    return jnp.concatenate([jnp.sin(ar), jnp.cos(ar), jnp.sin(ac), jnp.cos(ac)], axis=-1)


def kernel(x_prompt, x_sample, state_delta, state_lru, c, c_ctx, norm1_g, w_mod, b_mod, w_in, conv_qkv, dn_a_log, dn_dt_bias, dn_norm_g, w_dn_out, conv_lru_w, conv_lru_b, lru_wa, lru_ba, lru_wx, lru_bx, lru_lambda, w_lru_out, w_o, norm2_g, w_router, w_gate, w_up, w_down, final_g):
    bp, tp, d = x_prompt.shape
    bs, ts, _ = x_sample.shape
    depth = w_in.shape[0]
    heads, dk, dv = state_delta.shape[3:]
    qk = heads * dk
    vw = heads * dv
    lru_w = state_lru.shape[-1]
    n_experts = w_router.shape[-1]
    n_p, n_s = bp * tp, bs * ts
    n = n_p + n_s
    cap_p = 2 * n_p // n_experts
    cap_s = 2 * n_s // n_experts
    assert dk == LANES and dv == LANES and n_p % ts == 0 and ts % tp == 0

    xs = x_sample + _pos_embed_2d(ts, d)[None]
    x = jnp.concatenate([x_prompt.reshape(n_p, d), xs.reshape(n_s, d)], axis=0)

    cond8 = jnp.zeros((8, d), F32).at[0].set(c_ctx).at[1:1 + bs].set(c)
    mod = _modulation(cond8, w_mod, b_mod)
    mod = mod.reshape(depth, 8, 6, 1, d)

    n_small = 2 * N_DIR * heads
    c0 = 2 * qk + 2 * vw
    w_main, w_ba = _w_in_prep(w_in, c0, n_small)
    lx_blk = c0 // LANES
    ly_blk = lx_blk + lru_w // LANES
    ga_blk = (c0 + 2 * lru_w) // d
    gb_blk = ga_blk + 1
    w_dn_b = w_dn_out.astype(BF16)
    w_lru_b = w_lru_out.astype(BF16)
    w_o_b = w_o.astype(BF16)
    w_r_b = jnp.pad(w_router, ((0, 0), (0, 0), (0, LANES - n_experts))).astype(BF16)
    wa_b = lru_wa.astype(BF16)
    wx_b = lru_wx.astype(BF16)
    lane_pad = lambda a: jnp.pad(a.reshape(depth, 1, N_DIR * heads), ((0, 0), (0, 0), (0, LANES - N_DIR * heads)))
    al_v = lane_pad(dn_a_log)
    dt_v = lane_pad(dn_dt_bias)
    norm1_3 = norm1_g.reshape(depth, 1, d)
    norm2_3 = norm2_g.reshape(depth, 1, d)
    dn_norm_3 = dn_norm_g.reshape(depth, 1, dv)
    conv_lru_b3 = conv_lru_b.reshape(depth, 1, lru_w)

    tm_in = ts
    tiles_p_in = n_p // tm_in
    row_in = lambda i: jnp.where(i < tiles_p_in, 0, i - tiles_p_in + 1)
    tm_mg = 512
    tiles_p_mg = n_p // tm_mg
    per_seq = ts // tm_mg
    row_mg = lambda i: jnp.where(i < tiles_p_mg, 0, (i - tiles_p_mg) // per_seq + 1)
    tiles_p_cb = n_p // ROUTE_TB
    per_seq_cb = ts // ROUTE_TB
    row_cb = lambda i: jnp.where(i < tiles_p_cb, 0, (i - tiles_p_cb) // per_seq_cb + 1)
    assert n_p % ROUTE_TB == 0 and ts % ROUTE_TB == 0 and n_experts + 2 <= ROUTE_ROWS

    sd_out, sl_out = [], []
    for l in range(depth):
        cols, ba = _in_proj(l, x, norm1_3, mod, w_main, w_ba, row_in, tm_in)
        o_p, sd_p = _gdn_seq(l, cols, ba, conv_qkv, al_v, dt_v, dn_norm_3, None, 0, bp, tp, heads, dk)
        o_s, _ = _gdn_seq(l, cols, ba, conv_qkv, al_v, dt_v, dn_norm_3, state_delta, n_p, bs, ts, heads, dk)
        y_p, sl_p = _lru(l, cols, conv_lru_w, conv_lru_b3, wa_b, wx_b, lru_ba, lru_bx, lru_lambda, None,
                         0, bp, tp, lx_blk, ly_blk)
        y_s, _ = _lru(l, cols, conv_lru_w, conv_lru_b3, wa_b, wx_b, lru_ba, lru_bx, lru_lambda, state_lru,
                      n_p, bs, ts, lx_blk, ly_blk)
        x1, hn_aug, afft = _merge(l, o_p, o_s, y_p, y_s, cols, x, mod, norm2_3, w_dn_b, w_lru_b, w_o_b, w_r_b,
                                  row_mg, tm_mg, ga_blk, gb_blk, n_experts)
        thr_need = _thresholds(afft, ((0, n_p, cap_p), (n_p, n, cap_s)))
        slot, off, cnt, lst = _route(hn_aug, thr_need, n_p, cap_p, n_experts)
        nblk = n // ROUTE_TB
        off = off.reshape(nblk, LANES)
        cnt = cnt.reshape(nblk, LANES)
        xe_p = _gather(off, cnt, lst, hn_aug, 0, n_p, 0, cap_p)
        xe_s = _gather(off, cnt, lst, hn_aug, n_p, n_s, cap_p, cap_s)
        ye = _ffn(l, xe_p, xe_s, w_gate, w_up, w_down, 512)
        x = _combine(l, off, cnt, ye, slot, x1, mod, final_g, row_cb, n_experts, l == depth - 1)
        sd_out.append(sd_p)
        sl_out.append(sl_p)

    y = x
    y_prompt = y[:n_p].reshape(bp, tp, d)
    y_sample = y[n_p:].reshape(bs, ts, d)
    return (y_prompt, y_sample, jnp.stack(sd_out, axis=1), jnp.stack(sl_out, axis=1))
```

```python
import functools
import math

import jax
import jax.numpy as jnp
from jax import lax
from jax.experimental import pallas as pl
from jax.experimental.pallas import tpu as pltpu

F32 = jnp.float32
BF16 = jnp.bfloat16

EPS = 1e-6
CHUNK = 64
SUB = 8
GDN_BATCH = 32
CONV_LEFT = 2
CONV_W = 4
LRU_C = 8.0
LRU_GROUP = 4
N_DIR = 2
GRID_W = 64
LANES = 128
SUBLANES = 8
VMEM_LIMIT = 56 * 1024 * 1024


def _cparams(sem):
    return pltpu.CompilerParams(dimension_semantics=sem, vmem_limit_bytes=VMEM_LIMIT)


def _bdot(a, b):
    return jnp.dot(a.astype(BF16), b.astype(BF16), preferred_element_type=F32)


def _mod_kernel(c_ref, w_ref, b_ref, o_ref):
    c = c_ref[...]
    c = c * jax.nn.sigmoid(c)
    o_ref[...] = _bdot(c, w_ref[...]) + b_ref[...]


def _modulation(cond8, w_mod, b_mod):
    depth, d, n6 = w_mod.shape
    tn = 1536
    return pl.pallas_call(
        _mod_kernel,
        grid=(depth, n6 // tn),
        in_specs=[
            pl.BlockSpec((8, d), lambda l, j: (0, 0)),
            pl.BlockSpec((None, d, tn), lambda l, j: (l, 0, j)),
            pl.BlockSpec((None, 1, tn), lambda l, j: (l, 0, j)),
        ],
        out_specs=pl.BlockSpec((None, 8, tn), lambda l, j: (l, 0, j)),
        out_shape=jax.ShapeDtypeStruct((depth, 8, n6), F32),
        compiler_params=_cparams(("parallel", "parallel")),
        name="modulation",
    )(cond8, w_mod, b_mod.reshape(depth, 1, n6))


def _w_in_prep_kernel(w_ref, main_ref, ba_ref, *, c0, n_small):
    w = w_ref[...]
    rows = w.shape[0]
    main_ref[...] = jnp.concatenate([w[:, :c0], w[:, c0 + n_small:]], axis=1).astype(BF16)
    ba_ref[...] = jnp.concatenate(
        [w[:, c0:c0 + n_small], jnp.zeros((rows, LANES - n_small), F32)], axis=1).astype(BF16)


def _w_in_prep(w_in, c0, n_small):
    depth, d, ncol = w_in.shape
    tk = 256
    return pl.pallas_call(
        functools.partial(_w_in_prep_kernel, c0=c0, n_small=n_small),
        grid=(depth, d // tk),
        in_specs=[pl.BlockSpec((None, tk, ncol), lambda l, i: (l, i, 0))],
        out_specs=[
            pl.BlockSpec((None, tk, ncol - n_small), lambda l, i: (l, i, 0)),
            pl.BlockSpec((None, tk, LANES), lambda l, i: (l, i, 0)),
        ],
        out_shape=[
            jax.ShapeDtypeStruct((depth, d, ncol - n_small), BF16),
            jax.ShapeDtypeStruct((depth, d, LANES), BF16),
        ],
        compiler_params=_cparams(("parallel", "parallel")),
        name="w_in_prep",
    )(w_in)


def _in_proj_kernel(x_ref, g_ref, sh_ref, sc_ref, w_ref, wba_ref, o_ref, oba_ref, hn_scr):
    @pl.when(pl.program_id(1) == 0)
    def _():
        x = x_ref[...]
        y = x * lax.rsqrt(jnp.mean(x * x, axis=-1, keepdims=True) + EPS) * g_ref[...]
        hn = (y * (1.0 + sc_ref[...]) + sh_ref[...]).astype(BF16)
        hn_scr[...] = hn
        oba_ref[...] = jnp.dot(hn, wba_ref[...], preferred_element_type=F32)

    o_ref[...] = jnp.dot(hn_scr[...], w_ref[...], preferred_element_type=F32)


def _in_proj(l, x, norm_g, mod5, w_main, w_ba, row_of_tile, tm):
    n, d = x.shape
    ncols = w_main.shape[2]
    tn = 1536
    return pl.pallas_call(
        _in_proj_kernel,
        grid=(n // tm, ncols // tn),
        in_specs=[
            pl.BlockSpec((tm, d), lambda i, j: (i, 0)),
            pl.BlockSpec((None, 1, d), lambda i, j: (l, 0, 0)),
            pl.BlockSpec((None, None, None, 1, d), lambda i, j: (l, row_of_tile(i), 0, 0, 0)),
            pl.BlockSpec((None, None, None, 1, d), lambda i, j: (l, row_of_tile(i), 1, 0, 0)),
            pl.BlockSpec((None, d, tn), lambda i, j: (l, 0, j)),
            pl.BlockSpec((None, d, LANES), lambda i, j: (l, 0, 0)),
        ],
        out_specs=[
            pl.BlockSpec((tm, tn), lambda i, j: (i, j)),
            pl.BlockSpec((tm, LANES), lambda i, j: (i, 0)),
        ],
        out_shape=[
            jax.ShapeDtypeStruct((n, ncols), F32),
            jax.ShapeDtypeStruct((n, LANES), F32),
        ],
        scratch_shapes=[pltpu.VMEM((tm, d), BF16)],
        compiler_params=_cparams(("parallel", "arbitrary")),
        name="in_proj",
    )(x, norm_g, mod5, mod5, w_main, w_ba)


def _conv_rows(x, w, row):
    t = x.shape[0]
    acc = x * w[CONV_LEFT:CONV_LEFT + 1, :]
    for j in range(CONV_W):
        off = j - CONV_LEFT
        if off == 0:
            continue
        xs = pltpu.roll(x, (-off) % t, axis=0)
        valid = (row + off >= 0) & (row + off < t)
        acc = acc + jnp.where(valid, xs, 0.0) * w[j:j + 1, :]
    return acc


def _softplus(x):
    return jnp.maximum(x, 0.0) + jnp.log1p(jnp.exp(-jnp.abs(x)))


def _bmm(a, b):
    return jnp.einsum('nij,njk->nik', a.astype(BF16), b.astype(BF16), preferred_element_type=F32)


def _bmm_nt(a, b, precision=None):
    return jnp.einsum('nid,njd->nij', a, b, preferred_element_type=F32, precision=precision)


def _unit_tri_inverse(a, ii, jj):
    def same(b):
        return (ii // b) == (jj // b)

    eye = (ii == jj).astype(F32)
    d1 = jnp.where(same(SUB), a, 0.0)
    d2 = _bmm(d1, d1)
    d4 = _bmm(d2, d2)
    x = eye - d1
    x = x + _bmm(x, d2)
    x = x + _bmm(x, d4)
    b = SUB
    while b < CHUNK:
        o = jnp.where(same(2 * b) & jnp.logical_not(same(b)), a, 0.0)
        x = x - _bmm(_bmm(x, o), x)
        b *= 2
    return x


def _gdn_kernel(*refs, t, has_s0, dk):
    if has_s0:
        (q_ref, k_ref, v_ref, z_ref, ba_ref, cq_ref, ck_ref, cv_ref, al_ref, dt_ref, ng_ref,
         oh_ref, s0_ref, o_ref, s_ref) = refs
    else:
        (q_ref, k_ref, v_ref, z_ref, ba_ref, cq_ref, ck_ref, cv_ref, al_ref, dt_ref, ng_ref,
         oh_ref, o_ref, s_ref) = refs
        s0_ref = None
    n = t // CHUNK
    row = lax.broadcasted_iota(jnp.int32, (t, LANES), 0)
    pos = row % CHUNK

    def conv_silu(x_ref, w_ref):
        y = _conv_rows(x_ref[...], w_ref[...], row)
        return y * jax.nn.sigmoid(y)

    def l2n(x):
        return x * lax.rsqrt(jnp.sum(x * x, axis=-1, keepdims=True) + EPS)

    q = l2n(conv_silu(q_ref, cq_ref)) * (dk ** -0.5)
    k = l2n(conv_silu(k_ref, ck_ref))
    v = conv_silu(v_ref, cv_ref)
    q3 = q.reshape(n, CHUNK, LANES).astype(BF16)
    k3 = k.reshape(n, CHUNK, LANES)
    v3 = v.reshape(n, CHUNK, LANES)
    k3b = k3.astype(BF16)
    gram = _bmm_nt(k3b, k3b)
    qk = _bmm_nt(q3, k3b)
    q3 = q.reshape(n, CHUNK, LANES)

    ba = ba_ref[...]
    al = al_ref[...]
    dtb = dt_ref[...]
    ii = lax.broadcasted_iota(jnp.int32, (CHUNK, CHUNK), 0)
    jj = lax.broadcasted_iota(jnp.int32, (CHUNK, CHUNK), 1)
    onehot0 = jnp.broadcast_to(oh_ref[...][None], (n, CHUNK, LANES))

    per_dir = []
    for d in range(N_DIR):
        beta = jax.nn.sigmoid(ba[:, d:d + 1])
        g = -jnp.exp(al[:, d:d + 1]) * _softplus(ba[:, 2 + d:3 + d] + dtb[:, d:d + 1])
        beta_b = jnp.broadcast_to(beta, (t, LANES))
        dec = jnp.broadcast_to(g, (t, LANES))
        step = 1
        while step < CHUNK:
            if d == 0:
                sh = pltpu.roll(dec, step, axis=0)
                dec = dec + jnp.where(pos >= step, sh, 0.0)
            else:
                sh = pltpu.roll(dec, t - step, axis=0)
                dec = dec + jnp.where(pos < CHUNK - step, sh, 0.0)
            step *= 2
        dec3 = dec.reshape(n, CHUNK, LANES)
        tot3 = jnp.broadcast_to(dec3[:, CHUNK - 1:CHUNK, :] if d == 0 else dec3[:, 0:1, :],
                                (n, CHUNK, LANES))
        beta3 = beta_b.reshape(n, CHUNK, LANES)
        dec_row = _bmm_nt(onehot0, dec3, precision=lax.Precision.HIGHEST)
        dec_col = dec3[:, :, :CHUNK]
        tri = (ii >= jj) if d == 0 else (ii <= jj)
        strict = (ii > jj) if d == 0 else (ii < jj)
        gamma = jnp.where(tri, jnp.exp(jnp.where(tri, dec_col - dec_row, 0.0)), 0.0)
        a = jnp.where(strict, gram * gamma, 0.0) * beta3[:, :, :CHUNK]
        x = _unit_tri_inverse(a, ii, jj)
        attn = jnp.where(tri, qk * gamma, 0.0)
        edec = jnp.exp(dec3)
        kbd = k3 * (beta3 * edec)
        vb = v3 * beta3
        qd = q3 * edec
        kd = k3 * jnp.exp(tot3 - dec3)
        cd = jnp.exp(tot3)
        per_dir.append((x.astype(BF16), attn.astype(BF16), kbd.astype(BF16), vb, qd.astype(BF16),
                        kd.astype(BF16), cd))

    states = []
    outs = []
    for d in range(N_DIR):
        if has_s0:
            states.append(s0_ref[d])
        else:
            states.append(jnp.zeros((LANES, LANES), F32))
        outs.append([None] * n)
    for it in range(n):
        for d in range(N_DIR):
            c = it if d == 0 else n - 1 - it
            x, attn, kbd, vb, qd, kd, cd = per_dir[d]
            s = states[d]
            sb = s.astype(BF16)
            r = jnp.dot(kbd[c], sb, preferred_element_type=F32)
            vn = jnp.dot(x[c], (vb[c] - r).astype(BF16), preferred_element_type=F32).astype(BF16)
            outs[d][c] = (jnp.dot(qd[c], sb, preferred_element_type=F32)
                          + jnp.dot(attn[c], vn, preferred_element_type=F32))
            states[d] = s * cd[c][0:1, :] + lax.dot_general(kd[c], vn, (((0,), (0,)), ((), ())),
                                                    preferred_element_type=F32)
    for d in range(N_DIR):
        s_ref[d] = states[d]
    o = jnp.concatenate([outs[0][c] + outs[1][c] for c in range(n)], axis=0)
    o = o * lax.rsqrt(jnp.mean(o * o, axis=-1, keepdims=True) + EPS) * ng_ref[...]
    z = z_ref[...]
    o_ref[...] = (o * (z * jax.nn.sigmoid(z))).astype(o_ref.dtype)


def _gdn(cols, bah, conv_qkv, al, dtb, norm_g, s0, row0, nb, t, heads, dk):
    n = cols.shape[0]
    blk0 = row0 // t
    has_s0 = s0 is not None
    col = lambda off: pl.BlockSpec((t, LANES), lambda b, h: (blk0 + b, off + h))
    cw = lambda off: pl.BlockSpec((CONV_W, LANES), lambda b, h: (0, off + h))
    in_specs = [
        col(0), col(heads), col(2 * heads), col(3 * heads),
        pl.BlockSpec((None, t, 4), lambda b, h: (h, blk0 + b, 0)),
        cw(0), cw(heads), cw(2 * heads),
        pl.BlockSpec((None, 1, N_DIR), lambda b, h: (h, 0, 0)),
        pl.BlockSpec((None, 1, N_DIR), lambda b, h: (h, 0, 0)),
        pl.BlockSpec((1, LANES), lambda b, h: (0, 0)),
        pl.BlockSpec((CHUNK, LANES), lambda b, h: (0, 0)),
    ]
    onehot0 = jnp.zeros((CHUNK, LANES), F32).at[:, 0].set(1.0)
    args = [cols, cols, cols, cols, bah, conv_qkv, conv_qkv, conv_qkv, al, dtb, norm_g.reshape(1, LANES),
            onehot0]
    if has_s0:
        in_specs.append(pl.BlockSpec((None, N_DIR, None, dk, LANES), lambda b, h: (b, 0, h, 0, 0)))
        args.append(s0)
    return pl.pallas_call(
        functools.partial(_gdn_kernel, t=t, has_s0=has_s0, dk=dk),
        grid=(nb, heads),
        in_specs=in_specs,
        out_specs=[
            pl.BlockSpec((t, LANES), lambda b, h: (b, h)),
            pl.BlockSpec((None, N_DIR, None, dk, LANES), lambda b, h: (b, 0, h, 0, 0)),
        ],
        out_shape=[
            jax.ShapeDtypeStruct((nb * t, heads * LANES), BF16),
            jax.ShapeDtypeStruct((nb, N_DIR, heads, dk, LANES), F32),
        ],
        compiler_params=_cparams(("parallel", "parallel")),
        name=f"gdn_t{t}",
    )(*args)


def _gdn_seq_kernel(*refs, t, has_s0, heads, dk, hg):
    n_st = N_DIR * heads
    n_in = 13 if has_s0 else 12
    ins, outs = refs[:n_in], refs[n_in:]
    (q_ref, k_ref, v_ref, z_ref, ba_ref, cq_ref, ck_ref, cv_ref, al_ref, dt_ref, ng_ref, oh_ref) = ins[:12]
    s0_ref = ins[12] if has_s0 else None
    o_ref, s_ref, x_scr, at_scr, kq_scr, kd_scr, vb_scr, cd_scr, o_scr = outs[:9]
    st_scrs = outs[9:9 + n_st]
    n = t // CHUNK
    row = lax.broadcasted_iota(jnp.int32, (t, LANES), 0)
    pos = row % CHUNK
    ii = lax.broadcasted_iota(jnp.int32, (CHUNK, CHUNK), 0)
    jj = lax.broadcasted_iota(jnp.int32, (CHUNK, CHUNK), 1)
    ba = ba_ref[...]
    al = al_ref[...]
    dtb = dt_ref[...]

    def conv_silu(x_ref, w_ref, h):
        y = _conv_rows(x_ref[:, h * LANES:(h + 1) * LANES], w_ref[:, h * LANES:(h + 1) * LANES], row)
        return y * jax.nn.sigmoid(y)

    def l2n(x):
        return x * lax.rsqrt(jnp.sum(x * x, axis=-1, keepdims=True) + EPS)

    nb = hg * n
    ii3 = lax.broadcasted_iota(jnp.int32, (N_DIR * nb, CHUNK, CHUNK), 1)
    jj3 = lax.broadcasted_iota(jnp.int32, (N_DIR * nb, CHUNK, CHUNK), 2)
    fwd = lax.broadcasted_iota(jnp.int32, (N_DIR * nb, CHUNK, CHUNK), 0) < nb
    ahead = jnp.where(fwd, ii3 - jj3, jj3 - ii3)
    tri = ahead >= 0
    strict = ahead > 0
    onehot0 = jnp.broadcast_to(oh_ref[...][None], (N_DIR * nb, CHUNK, LANES))

    for h0 in range(0, heads, hg):
        hs = range(h0, h0 + hg)
        as3 = lambda x: x.reshape(n, CHUNK, LANES)
        q3 = jnp.concatenate([as3(l2n(conv_silu(q_ref, cq_ref, h)) * (dk ** -0.5)) for h in hs], axis=0)
        k3 = jnp.concatenate([as3(l2n(conv_silu(k_ref, ck_ref, h))) for h in hs], axis=0)
        v3 = jnp.concatenate([as3(conv_silu(v_ref, cv_ref, h)) for h in hs], axis=0)
        k3b = k3.astype(BF16)
        gram = _bmm_nt(k3b, k3b)
        qk = _bmm_nt(q3.astype(BF16), k3b)
        decs, tots, betas = [], [], []
        for d in range(N_DIR):
            for h in hs:
                lb = d * heads + h
                la = 2 * heads + lb
                beta = jax.nn.sigmoid(ba[:, lb:lb + 1])
                g = -jnp.exp(al[:, lb:lb + 1]) * _softplus(ba[:, la:la + 1] + dtb[:, lb:lb + 1])
                dec = jnp.broadcast_to(g, (t, LANES))
                step = 1
                while step < CHUNK:
                    if d == 0:
                        sh = pltpu.roll(dec, step, axis=0)
                        dec = dec + jnp.where(pos >= step, sh, 0.0)
                    else:
                        sh = pltpu.roll(dec, t - step, axis=0)
                        dec = dec + jnp.where(pos < CHUNK - step, sh, 0.0)
                    step *= 2
                dec3 = as3(dec)
                decs.append(dec3)
                tots.append(jnp.broadcast_to(dec3[:, CHUNK - 1:CHUNK, :] if d == 0 else dec3[:, 0:1, :],
                                             (n, CHUNK, LANES)))
                betas.append(as3(jnp.broadcast_to(beta, (t, LANES))))
        dec3 = jnp.concatenate(decs, axis=0)
        tot3 = jnp.concatenate(tots, axis=0)
        beta3 = jnp.concatenate(betas, axis=0)
        both = lambda x: jnp.concatenate([x, x], axis=0)
        dec_row = _bmm_nt(onehot0, dec3, precision=lax.Precision.HIGHEST)
        dec_col = dec3[:, :, :CHUNK]
        gamma = jnp.where(tri, jnp.exp(jnp.where(tri, dec_col - dec_row, 0.0)), 0.0)
        a = jnp.where(strict, both(gram) * gamma, 0.0) * beta3[:, :, :CHUNK]
        xinv = _unit_tri_inverse(a, ii, jj).astype(BF16)
        attn = jnp.where(tri, both(qk) * gamma, 0.0).astype(BF16)
        edec = jnp.exp(dec3)
        k32, q32, v32 = both(k3), both(q3), both(v3)
        kbd = (k32 * (beta3 * edec)).astype(BF16)
        qd = (q32 * edec).astype(BF16)
        kd = jnp.swapaxes(k32 * jnp.exp(tot3 - dec3), 1, 2).astype(BF16)
        vb = v32 * beta3
        cd = jnp.exp(tot3[:, :SUBLANES, :])
        for d in range(N_DIR):
            src = slice(d * nb, (d + 1) * nb)
            dst = pl.ds((d * heads + h0) * n, nb)
            x_scr[dst] = xinv[src]
            at_scr[dst] = attn[src]
            kq_scr[dst, :CHUNK, :] = kbd[src]
            kq_scr[dst, CHUNK:, :] = qd[src]
            kd_scr[dst] = kd[src]
            vb_scr[dst] = vb[src]
            cd_scr[dst] = cd[src]

    for d in range(N_DIR):
        for h in range(heads):
            if has_s0:
                st_scrs[d * heads + h][...] = s0_ref[d, h]
            else:
                st_scrs[d * heads + h][...] = jnp.zeros((dk, LANES), F32)

    def scan(it, carry):
        chains = [(d * heads + h, (d * heads + h) * n + (it if d == 0 else n - 1 - it))
                  for h in range(heads) for d in range(N_DIR)]
        ss = [st_scrs[i][...] for i, _ in chains]
        rqs = [jnp.dot(kq_scr[b], s.astype(BF16), preferred_element_type=F32)
               for (_, b), s in zip(chains, ss)]
        vns = [jnp.dot(x_scr[b], (vb_scr[b] - rq[:CHUNK]).astype(BF16), preferred_element_type=F32).astype(BF16)
               for (_, b), rq in zip(chains, rqs)]
        for (i, b), s, rq, vn in zip(chains, ss, rqs, vns):
            o_scr[b] = rq[CHUNK:] + jnp.dot(at_scr[b], vn, preferred_element_type=F32)
            st_scrs[i][...] = s * cd_scr[b][0:1, :] + jnp.dot(kd_scr[b], vn, preferred_element_type=F32)
        return carry

    lax.fori_loop(0, n, scan, 0)

    for h in range(heads):
        for d in range(N_DIR):
            s_ref[d, h] = st_scrs[d * heads + h][...]
        of = o_scr[pl.ds(h * n, n)]
        ob = o_scr[pl.ds((heads + h) * n, n)]
        o = (of + ob).reshape(t, LANES)
        o = o * lax.rsqrt(jnp.mean(o * o, axis=-1, keepdims=True) + EPS) * ng_ref[...]
        z = z_ref[:, h * LANES:(h + 1) * LANES]
        o_ref[:, h * LANES:(h + 1) * LANES] = (o * (z * jax.nn.sigmoid(z))).astype(o_ref.dtype)


def _gdn_seq(l, cols, ba, conv_qkv, al, dtb, norm_g, s0, row0, nb, t, heads, dk):
    blk0 = row0 // t
    has_s0 = s0 is not None
    hw = heads * LANES
    n = t // CHUNK
    col = lambda off: pl.BlockSpec((t, hw), lambda b: (blk0 + b, off))
    cw = lambda off: pl.BlockSpec((None, CONV_W, hw), lambda b: (l, 0, off))
    vec = pl.BlockSpec((None, 1, LANES), lambda b: (l, 0, 0))
    in_specs = [
        col(0), col(1), col(2), col(3),
        pl.BlockSpec((t, LANES), lambda b: (blk0 + b, 0)),
        cw(0), cw(1), cw(2),
        vec, vec, vec,
        pl.BlockSpec((CHUNK, LANES), lambda b: (0, 0)),
    ]
    onehot0 = jnp.zeros((CHUNK, LANES), F32).at[:, 0].set(1.0)
    args = [cols, cols, cols, cols, ba, conv_qkv, conv_qkv, conv_qkv, al, dtb, norm_g, onehot0]
    if has_s0:
        in_specs.append(pl.BlockSpec((None, None, N_DIR, heads, dk, LANES), lambda b: (b, l, 0, 0, 0, 0)))
        args.append(s0)
    per = (N_DIR * heads * n,)
    hg = max(1, min(heads, GDN_BATCH // (N_DIR * n)))
    assert heads % hg == 0
    return pl.pallas_call(
        functools.partial(_gdn_seq_kernel, t=t, has_s0=has_s0, heads=heads, dk=dk, hg=hg),
        grid=(nb,),
        in_specs=in_specs,
        out_specs=[
            pl.BlockSpec((t, hw), lambda b: (b, 0)),
            pl.BlockSpec((None, N_DIR, heads, dk, LANES), lambda b: (b, 0, 0, 0, 0)),
        ],
        out_shape=[
            jax.ShapeDtypeStruct((nb * t, hw), BF16),
            jax.ShapeDtypeStruct((nb, N_DIR, heads, dk, LANES), F32),
        ],
        scratch_shapes=[
            pltpu.VMEM(per + (CHUNK, CHUNK), BF16),
            pltpu.VMEM(per + (CHUNK, CHUNK), BF16),
            pltpu.VMEM(per + (2 * CHUNK, LANES), BF16),
            pltpu.VMEM(per + (dk, CHUNK), BF16),
            pltpu.VMEM(per + (CHUNK, LANES), F32),
            pltpu.VMEM(per + (SUBLANES, LANES), F32),
            pltpu.VMEM(per + (CHUNK, LANES), F32),
        ] + [pltpu.VMEM((dk, LANES), F32)] * (N_DIR * heads),
        compiler_params=_cparams(("parallel",)),
        name=f"gdn_t{t}",
    )(*args)


def _lru_kernel(*refs, t, has_h0):
    if has_h0:
        (lx_ref, ly_ref, cw_ref, cb_ref, wa_ref, wx_ref, ba_ref, bx_ref, lam_ref, h0_ref,
         y_ref, last_ref, a_scr, b_scr, h_scr) = refs
    else:
        (lx_ref, ly_ref, cw_ref, cb_ref, wa_ref, wx_ref, ba_ref, bx_ref, lam_ref,
         y_ref, last_ref, a_scr, b_scr, h_scr) = refs
        h0_ref = None
    nblk = t // SUBLANES
    wl = lx_ref.shape[1]
    row = lax.broadcasted_iota(jnp.int32, (t, wl), 0)
    sub = lax.broadcasted_iota(jnp.int32, (nblk, SUBLANES, wl), 1)
    u = _conv_rows(lx_ref[...], cw_ref[...], row) + cb_ref[...]
    ub = u.astype(BF16)

    def block_diag(w_ref, d):
        return jnp.concatenate(
            [jnp.dot(ub[:, c * LANES:(c + 1) * LANES], w_ref[d, c], preferred_element_type=F32)
             for c in range(wl // LANES)], axis=1)

    for d in range(N_DIR):
        r = jax.nn.sigmoid(block_diag(wa_ref, d) + ba_ref[d:d + 1, :])
        i = jax.nn.sigmoid(block_diag(wx_ref, d) + bx_ref[d:d + 1, :])
        log_a = (LRU_C * r) * (-_softplus(-lam_ref[d:d + 1, :]))
        a = jnp.exp(log_a)
        b = jnp.sqrt(1.0 - a * a) * (i * u)
        a3 = a.reshape(nblk, SUBLANES, wl)
        b3 = b.reshape(nblk, SUBLANES, wl)
        step = 1
        while step < SUBLANES:
            if d == 0:
                a_s = pltpu.roll(a3, step, axis=1)
                b_s = pltpu.roll(b3, step, axis=1)
                m = sub >= step
            else:
                a_s = pltpu.roll(a3, SUBLANES - step, axis=1)
                b_s = pltpu.roll(b3, SUBLANES - step, axis=1)
                m = sub < SUBLANES - step
            b3 = b3 + a3 * jnp.where(m, b_s, 0.0)
            a3 = a3 * jnp.where(m, a_s, 1.0)
            step *= 2
        a_scr[d] = a3
        b_scr[d] = b3

    if has_h0:
        h_init = (jnp.broadcast_to(h0_ref[0:1, :], (SUBLANES, wl)),
                  jnp.broadcast_to(h0_ref[1:2, :], (SUBLANES, wl)))
    else:
        h_init = (jnp.zeros((SUBLANES, wl), F32), jnp.zeros((SUBLANES, wl), F32))

    def body(it, carry):
        hf, hb = carry
        kf = it
        kb = nblk - 1 - it
        new_f = b_scr[0, kf] + a_scr[0, kf] * hf
        new_b = b_scr[1, kb] + a_scr[1, kb] * hb
        h_scr[0, kf] = new_f
        h_scr[1, kb] = new_b
        hf = jnp.broadcast_to(new_f[SUBLANES - 1:SUBLANES, :], (SUBLANES, wl))
        hb = jnp.broadcast_to(new_b[0:1, :], (SUBLANES, wl))
        return hf, hb

    hf, hb = lax.fori_loop(0, nblk, body, h_init, unroll=4)
    last_ref[0:1, :] = hf[0:1, :]
    last_ref[1:2, :] = hb[0:1, :]
    rec = (h_scr[0] + h_scr[1]).reshape(t, wl)
    y_ref[...] = (jax.nn.gelu(ly_ref[...]) * rec).astype(y_ref.dtype)


def _lru(l, cols, conv_w, conv_b, wa, wx, ba, bx, lam, h0, row0, nb, t, lx_blk, ly_blk):
    nblocks = wa.shape[2]
    w = nblocks * LANES
    cg = LRU_GROUP
    wl = cg * LANES
    blk0 = row0 // t
    has_h0 = h0 is not None
    in_specs = [
        pl.BlockSpec((t, wl), lambda b, c: (blk0 + b, lx_blk // cg + c)),
        pl.BlockSpec((t, wl), lambda b, c: (blk0 + b, ly_blk // cg + c)),
        pl.BlockSpec((None, CONV_W, wl), lambda b, c: (l, 0, c)),
        pl.BlockSpec((None, 1, wl), lambda b, c: (l, 0, c)),
        pl.BlockSpec((None, N_DIR, cg, LANES, LANES), lambda b, c: (l, 0, c, 0, 0)),
        pl.BlockSpec((None, N_DIR, cg, LANES, LANES), lambda b, c: (l, 0, c, 0, 0)),
        pl.BlockSpec((None, N_DIR, wl), lambda b, c: (l, 0, c)),
        pl.BlockSpec((None, N_DIR, wl), lambda b, c: (l, 0, c)),
        pl.BlockSpec((None, N_DIR, wl), lambda b, c: (l, 0, c)),
    ]
    args = [cols, cols, conv_w, conv_b, wa, wx, ba, bx, lam]
    if has_h0:
        in_specs.append(pl.BlockSpec((None, None, N_DIR, wl), lambda b, c: (b, l, 0, c)))
        args.append(h0)
    nblk = t // SUBLANES
    assert lx_blk % cg == 0 and ly_blk % cg == 0 and nblocks % cg == 0
    return pl.pallas_call(
        functools.partial(_lru_kernel, t=t, has_h0=has_h0),
        grid=(nb, nblocks // cg),
        in_specs=in_specs,
        out_specs=[
            pl.BlockSpec((t, wl), lambda b, c: (b, c)),
            pl.BlockSpec((None, N_DIR, wl), lambda b, c: (b, 0, c)),
        ],
        out_shape=[
            jax.ShapeDtypeStruct((nb * t, w), BF16),
            jax.ShapeDtypeStruct((nb, N_DIR, w), F32),
        ],
        scratch_shapes=[pltpu.VMEM((N_DIR, nblk, SUBLANES, wl), F32)] * 3,
        compiler_params=_cparams(("parallel", "parallel")),
        name=f"lru_t{t}",
    )(*args)


def _merge_kernel(oap_ref, oas_ref, obp_ref, obs_ref, ga_ref, gb_ref, x_ref, g1_ref, sh2_ref, sc2_ref,
                  n2_ref, wdn_ref, wlru_ref, wo_ref, wr_ref, x1_ref, hn_ref, afft_ref, *, n_experts, tiles_p):
    d = x_ref.shape[1]
    is_p = pl.program_id(0) < tiles_p
    oa = jnp.where(is_p, oap_ref[...], oas_ref[...])
    ob = jnp.where(is_p, obp_ref[...], obs_ref[...])
    ya = jnp.dot(oa, wdn_ref[...], preferred_element_type=F32)
    yb = jnp.dot(ob, wlru_ref[...], preferred_element_type=F32)
    mix = jax.nn.sigmoid(ga_ref[...]) * ya + jax.nn.sigmoid(gb_ref[...]) * yb
    mix = jnp.dot(mix.astype(BF16), wo_ref[...], preferred_element_type=F32)
    x1 = x_ref[...] + g1_ref[...] * mix
    x1_ref[...] = x1
    y = x1 * lax.rsqrt(jnp.mean(x1 * x1, axis=-1, keepdims=True) + EPS) * n2_ref[...]
    hn = y * (1.0 + sc2_ref[...]) + sh2_ref[...]
    logits = jnp.dot(hn.astype(BF16), wr_ref[...], preferred_element_type=F32)
    lane = lax.broadcasted_iota(jnp.int32, logits.shape, 1)
    logits = jnp.where(lane < n_experts, logits, -jnp.inf)
    e = jnp.exp(logits - jnp.max(logits, axis=-1, keepdims=True))
    aff = e / jnp.sum(e, axis=-1, keepdims=True)
    hn_ref[:, :d] = hn
    hn_ref[:, d:] = aff
    afft_ref[...] = aff.T[:n_experts, :]


def _merge(l, o_p, o_s, y_p, y_s, cols, x, mod5, norm2_g, w_dn, w_lru, w_o, w_r, row_of_tile, tm,
           ga_blk, gb_blk, n_experts):
    n, d = x.shape
    dv = o_p.shape[1]
    w = y_p.shape[1]
    tiles_p = o_p.shape[0] // tm
    modspec = lambda k: pl.BlockSpec((None, None, None, 1, d), lambda i: (l, row_of_tile(i), k, 0, 0))
    layer = lambda a: pl.BlockSpec((None,) + a.shape[1:], lambda i: (l,) + (0,) * (a.ndim - 1))
    p_tile = lambda width: pl.BlockSpec((tm, width), lambda i: (jnp.minimum(i, tiles_p - 1), 0))
    s_tile = lambda width: pl.BlockSpec((tm, width), lambda i: (jnp.maximum(i - tiles_p, 0), 0))
    return pl.pallas_call(
        functools.partial(_merge_kernel, n_experts=n_experts, tiles_p=tiles_p),
        grid=(n // tm,),
        in_specs=[
            p_tile(dv), s_tile(dv), p_tile(w), s_tile(w),
            pl.BlockSpec((tm, d), lambda i: (i, ga_blk)),
            pl.BlockSpec((tm, d), lambda i: (i, gb_blk)),
            pl.BlockSpec((tm, d), lambda i: (i, 0)),
            modspec(2), modspec(3), modspec(4),
            layer(norm2_g), layer(w_dn), layer(w_lru), layer(w_o), layer(w_r),
        ],
        out_specs=[
            pl.BlockSpec((tm, d), lambda i: (i, 0)),
            pl.BlockSpec((tm, d + LANES), lambda i: (i, 0)),
            pl.BlockSpec((n_experts, tm), lambda i: (0, i)),
        ],
        out_shape=[
            jax.ShapeDtypeStruct((n, d), F32),
            jax.ShapeDtypeStruct((n, d + LANES), F32),
            jax.ShapeDtypeStruct((n_experts, n), F32),
        ],
        compiler_params=_cparams(("parallel",)),
        name="merge",
    )(o_p, o_s, y_p, y_s, cols, cols, x, mod5, mod5, mod5, norm2_g, w_dn, w_lru, w_o, w_r)


def _ffn_kernel(xp_ref, xs_ref, wg_ref, wu_ref, wd_ref, o_ref, xb_scr):
    f = pl.program_id(1)
    cap_p = xp_ref.shape[0]
    d = xb_scr.shape[1]

    @pl.when(f == 0)
    def _():
        xb_scr[:cap_p, :] = xp_ref[:, :d].astype(BF16)
        xb_scr[cap_p:, :] = xs_ref[:, :d].astype(BF16)

    x = xb_scr[...]
    g = jnp.dot(x, wg_ref[...].astype(BF16), preferred_element_type=F32)
    u = jnp.dot(x, wu_ref[...].astype(BF16), preferred_element_type=F32)
    hid = ((g * jax.nn.sigmoid(g)) * u).astype(BF16)
    y = jnp.dot(hid, wd_ref[...].astype(BF16), preferred_element_type=F32)

    @pl.when(f == 0)
    def _():
        o_ref[...] = y

    @pl.when(f > 0)
    def _():
        o_ref[...] += y

    @pl.when(f == pl.num_programs(1) - 1)
    def _():
        e = pl.program_id(0)
        for ref, r0 in ((xp_ref, 0), (xs_ref, cap_p)):
            aff = ref[:, d:]
            lane = lax.broadcasted_iota(jnp.int32, aff.shape, 1)
            gv = jnp.sum(jnp.where(lane == e, aff, 0.0), axis=1, keepdims=True)
            rows = ref.shape[0]
            o_ref[r0:r0 + rows, :] = o_ref[r0:r0 + rows, :] * gv


def _ffn(l, xe_p, xe_s, w_gate, w_up, w_down, tf):
    e, cap_p, da = xe_p.shape
    d = da - LANES
    cap_s = xe_s.shape[1]
    r = cap_p + cap_s
    ff = w_gate.shape[3]
    return pl.pallas_call(
        _ffn_kernel,
        grid=(e, ff // tf),
        in_specs=[
            pl.BlockSpec((None, cap_p, da), lambda i, f: (i, 0, 0)),
            pl.BlockSpec((None, cap_s, da), lambda i, f: (i, 0, 0)),
            pl.BlockSpec((None, None, d, tf), lambda i, f: (l, i, 0, f)),
            pl.BlockSpec((None, None, d, tf), lambda i, f: (l, i, 0, f)),
            pl.BlockSpec((None, None, tf, d), lambda i, f: (l, i, f, 0)),
        ],
        out_specs=pl.BlockSpec((None, r, d), lambda i, f: (i, 0, 0)),
        out_shape=jax.ShapeDtypeStruct((e, r, d), F32),
        scratch_shapes=[pltpu.VMEM((r, d), BF16)],
        compiler_params=_cparams(("parallel", "arbitrary")),
        name="expert_ffn",
    )(xe_p, xe_s, w_gate, w_up, w_down)


ROUTE_TB = 128
ROUTE_ROWS = 24


def _thr_kernel(afft_ref, o_ref, *, groups):
    n_e = afft_ref.shape[0]
    sub = lax.broadcasted_iota(jnp.int32, (n_e, LANES), 0)
    lane = lax.broadcasted_iota(jnp.int32, (n_e, LANES), 1)
    rows = []
    for lo, hi, cap in groups:
        a = afft_ref[:, lo:hi]
        above_all = 4.0

        def count_ge(v, a=a):
            return jnp.sum((a >= v).astype(F32), axis=1, keepdims=True)

        def count_gt(v, a=a):
            return jnp.sum((a > v).astype(F32), axis=1, keepdims=True)

        def bisect(i, lh, cap=cap, count_ge=count_ge):
            lo_v, hi_v = lh
            mid = 0.5 * (lo_v + hi_v)
            ok = count_ge(mid) >= cap
            return jnp.where(ok, mid, lo_v), jnp.where(ok, hi_v, mid)

        lo_v, _ = lax.fori_loop(0, 48, bisect, (jnp.zeros((n_e, 1), F32), jnp.full((n_e, 1), 2.0, F32)))
        thr = jnp.min(jnp.where(a >= lo_v, a, above_all), axis=1, keepdims=True)

        def not_done(thr, cap=cap, count_gt=count_gt):
            return jnp.max(count_gt(thr)) >= cap

        def step_up(thr, a=a, cap=cap, count_gt=count_gt):
            nxt = jnp.min(jnp.where(a > thr, a, above_all), axis=1, keepdims=True)
            return jnp.where(count_gt(thr) >= cap, nxt, thr)

        thr = lax.while_loop(not_done, step_up, thr)
        need = cap - count_gt(thr)
        for col in (thr, need):
            m = jnp.where(sub == lane, jnp.broadcast_to(col, (n_e, LANES)), 0.0)
            rows.append(jnp.sum(m, axis=0, keepdims=True))
    rows.append(jnp.zeros((SUBLANES - len(rows), LANES), F32))
    o_ref[...] = jnp.concatenate(rows, axis=0)


def _thresholds(afft, groups):
    n_e, n = afft.shape
    return pl.pallas_call(
        functools.partial(_thr_kernel, groups=groups),
        grid=(1,),
        in_specs=[pl.BlockSpec((n_e, n), lambda i: (0, 0))],
        out_specs=pl.BlockSpec((SUBLANES, LANES), lambda i: (0, 0)),
        out_shape=jax.ShapeDtypeStruct((SUBLANES, LANES), F32),
        compiler_params=_cparams(("arbitrary",)),
        name="route_thresholds",
    )(afft)


def _route_kernel(aff_ref, tn_ref, ls_ref, slot_ref, off_ref, cnt_ref, lst_ref,
                  run_eq, run_sel, *, nblk_p, cap_p, n_e):
    tb = ROUTE_TB
    j = pl.program_id(0)

    @pl.when(j == 0)
    def _():
        run_eq[...] = jnp.zeros_like(run_eq)
        run_sel[...] = jnp.zeros_like(run_sel)

    @pl.when(j == nblk_p)
    def _():
        run_eq[...] = jnp.zeros_like(run_eq)
        run_sel[...] = jnp.full_like(run_sel, float(cap_p))

    g = (j >= nblk_p).astype(jnp.int32)
    thr = tn_ref[pl.ds(2 * g, 1), :]
    need = tn_ref[pl.ds(2 * g + 1, 1), :]
    a = aff_ref[...]
    lane = lax.broadcasted_iota(jnp.int32, (tb, LANES), 1)
    valid = lane < n_e
    gt = (a > thr) & valid
    eq = (a == thr) & valid
    eqf = eq.astype(F32)
    ls = ls_ref[...]
    eq_rank = run_eq[...] + jnp.dot(ls, eqf.astype(BF16), preferred_element_type=F32)
    sel = gt | (eq & (eq_rank < need))
    self_ = sel.astype(F32)
    pos = jnp.dot(ls, self_.astype(BF16), preferred_element_type=F32)
    cnt = jnp.sum(self_, axis=0, keepdims=True)
    off = run_sel[...]
    run_eq[...] = run_eq[...] + jnp.sum(eqf, axis=0, keepdims=True)
    run_sel[...] = off + cnt
    slot_ref[...] = jnp.where(sel, off + pos, -1.0)
    off_ref[...] = off.astype(jnp.int32)
    cnt_ref[...] = cnt.astype(jnp.int32)

    tok = (lax.broadcasted_iota(jnp.int32, (tb, LANES), 0) + j * tb).astype(F32)
    lanef = lane.astype(F32)
    for e in range(n_e):
        pe = jnp.broadcast_to(pos[:, e:e + 1], (tb, LANES))
        se = jnp.broadcast_to(self_[:, e:e + 1], (tb, LANES))
        hit = (pe == lanef) & (se > 0.0)
        lst_ref[e] = jnp.sum(jnp.where(hit, tok, 0.0), axis=0, keepdims=True).astype(jnp.int32)


def _route(hn_aug, thr_need, n_p, cap_p, n_e):
    n = hn_aug.shape[0]
    aff_blk = hn_aug.shape[1] // LANES - 1
    tb = ROUTE_TB
    nblk = n // tb
    ls = jnp.tril(jnp.ones((tb, tb), F32), -1).astype(BF16)
    blk_row = pl.BlockSpec((None, 1, LANES), lambda j: (j, 0, 0))
    return pl.pallas_call(
        functools.partial(_route_kernel, nblk_p=n_p // tb, cap_p=cap_p, n_e=n_e),
        grid=(nblk,),
        in_specs=[
            pl.BlockSpec((tb, LANES), lambda j: (j, aff_blk)),
            pl.BlockSpec((SUBLANES, LANES), lambda j: (0, 0)),
            pl.BlockSpec((tb, tb), lambda j: (0, 0)),
        ],
        out_specs=[
            pl.BlockSpec((tb, LANES), lambda j: (j, 0)),
            blk_row,
            blk_row,
            pl.BlockSpec((n_e, None, 1, LANES), lambda j: (0, j, 0, 0)),
        ],
        out_shape=[
            jax.ShapeDtypeStruct((n, LANES), F32),
            jax.ShapeDtypeStruct((nblk, 1, LANES), jnp.int32),
            jax.ShapeDtypeStruct((nblk, 1, LANES), jnp.int32),
            jax.ShapeDtypeStruct((n_e, nblk, 1, LANES), jnp.int32),
        ],
        scratch_shapes=[pltpu.VMEM((1, LANES), F32), pltpu.VMEM((1, LANES), F32)],
        compiler_params=_cparams(("arbitrary",)),
        name="route_slots",
    )(hn_aug, thr_need, ls)


GATHER_UNROLL = 4


def _gather_kernel(off_ref, cnt_ref, lst_hbm, hn_hbm, o_ref, hn_scr, lst_smem, sem_h, sem_l,
                   *, row0, slot0, blk0, nblk_g):
    e = pl.program_id(0)
    load_lst = pltpu.make_async_copy(lst_hbm.at[e, pl.ds(blk0, nblk_g)], lst_smem, sem_l)
    load_lst.start()

    @pl.when(e == 0)
    def _():
        rows = hn_scr.shape[0]
        load = pltpu.make_async_copy(hn_hbm.at[pl.ds(row0, rows)], hn_scr, sem_h)
        load.start()
        load.wait()

    load_lst.wait()

    def block(jb, carry):
        c = cnt_ref[blk0 + jb, e]
        o = off_ref[blk0 + jb, e] - slot0

        def copy_row(q):
            t = lst_smem[jb, q] - row0
            o_ref[pl.ds(o + q, 1), :] = hn_scr[pl.ds(t, 1), :]

        def rows(qq, carry2):
            for u in range(GATHER_UNROLL):
                copy_row(qq * GATHER_UNROLL + u)
            return carry2

        def row(q, carry2):
            copy_row(q)
            return carry2

        full = c // GATHER_UNROLL
        lax.fori_loop(0, full, rows, 0)
        lax.fori_loop(full * GATHER_UNROLL, c, row, 0)
        return carry

    lax.fori_loop(0, nblk_g, block, 0)


def _gather(off, cnt, lst, hn, row0, rows, slot0, cap):
    n_e = lst.shape[0]
    width = hn.shape[1]
    blk0 = row0 // ROUTE_TB
    nblk_g = rows // ROUTE_TB
    grid_spec = pltpu.PrefetchScalarGridSpec(
        num_scalar_prefetch=2,
        grid=(n_e,),
        in_specs=[pl.BlockSpec(memory_space=pl.ANY), pl.BlockSpec(memory_space=pl.ANY)],
        out_specs=pl.BlockSpec((None, cap, width), lambda e, o_, c_: (e, 0, 0)),
        scratch_shapes=[
            pltpu.VMEM((rows, width), F32),
            pltpu.SMEM((nblk_g, LANES), jnp.int32),
            pltpu.SemaphoreType.DMA(()),
            pltpu.SemaphoreType.DMA(()),
        ],
    )
    return pl.pallas_call(
        functools.partial(_gather_kernel, row0=row0, slot0=slot0, blk0=blk0, nblk_g=nblk_g),
        grid_spec=grid_spec,
        out_shape=jax.ShapeDtypeStruct((n_e, cap, width), F32),
        compiler_params=_cparams(("arbitrary",)),
        name=f"gather_rows{rows}",
    )(off, cnt, lst.reshape(n_e, -1, LANES), hn)


COMB_CH = 32
COMB_GRP = 4
COMB_MAXCH = 80


def _combine_kernel(off_ref, cnt_ref, y_hbm, slot_ref, x1_ref, g2_ref, fg_ref, o_ref,
                    buf, acc, desc, sems, *, n_e, final):
    tb = ROUTE_TB
    j = pl.program_id(0)
    nblk = pl.num_programs(0)
    r_total = y_hbm.shape[1]
    par = j % 2

    def issue(jj, p):
        s = jnp.int32(0)
        for e in range(n_e):
            o = off_ref[jj, e]
            c = cnt_ref[jj, e]
            st8 = (o // SUBLANES) * SUBLANES
            nch = jnp.where(c > 0, (o - st8 + c + COMB_CH - 1) // COMB_CH, 0)

            def one(k, s, e=e, st8=st8):
                lo_row = st8 + k * COMB_CH
                base = jnp.minimum(lo_row, r_total - COMB_CH)
                pltpu.make_async_copy(y_hbm.at[e, pl.ds(base, COMB_CH)], buf.at[p, s], sems.at[p]).start()
                desc[p, 0, s] = e
                desc[p, 1, s] = lo_row
                desc[p, 2, s] = base
                return s + 1

            s = lax.fori_loop(0, nch, one, s)
        desc[p, 3, 0] = s

    @pl.when(j == 0)
    def _():
        buf[...] = jnp.zeros_like(buf)

        def clear(i, carry):
            for p in range(2):
                for row in range(4):
                    desc[p, row, i] = 0
            return carry

        lax.fori_loop(0, LANES, clear, 0)
        issue(j, par)

    @pl.when(j + 1 < nblk)
    def _():
        issue(j + 1, 1 - par)

    n_ch = desc[par, 3, 0]

    def drain(k, carry):
        pltpu.make_async_copy(y_hbm.at[0, pl.ds(0, COMB_CH)], buf.at[par, k], sems.at[par]).wait()
        return carry

    lax.fori_loop(0, n_ch, drain, 0)

    slot = slot_ref[...]
    lane = lax.broadcasted_iota(jnp.int32, (tb, LANES), 1)
    lanef = lane.astype(F32)
    acc[...] = jnp.zeros_like(acc)

    def group(gi, carry):
        hit = jnp.zeros((tb, LANES), jnp.bool_)
        for c4 in range(COMB_GRP):
            s = gi * COMB_GRP + c4
            e = desc[par, 0, s]
            lo_row = desc[par, 1, s].astype(F32)
            base = desc[par, 2, s].astype(F32)
            col = jnp.sum(jnp.where(lane == e, slot, 0.0), axis=1, keepdims=True)
            colb = jnp.broadcast_to(col, (tb, LANES))
            in_chunk = (lane >= c4 * COMB_CH) & (lane < (c4 + 1) * COMB_CH)
            match = (colb == base + (lanef - float(c4 * COMB_CH))) & (colb >= lo_row)
            hit = hit | (in_chunk & match & (s < n_ch))
        eb = jnp.where(hit, 1.0, 0.0).astype(BF16)
        yg = buf[par, pl.ds(gi * COMB_GRP, COMB_GRP)].reshape(COMB_GRP * COMB_CH, -1)
        y1 = yg.astype(BF16)
        r1 = yg - y1.astype(F32)
        y2 = r1.astype(BF16)
        y3 = (r1 - y2.astype(F32)).astype(BF16)
        acc[...] += jnp.dot(jnp.concatenate([eb, eb, eb], axis=1), jnp.concatenate([y1, y2, y3], axis=0),
                            preferred_element_type=F32)
        return carry

    lax.fori_loop(0, (n_ch + COMB_GRP - 1) // COMB_GRP, group, 0)

    x2 = x1_ref[...] + g2_ref[...] * acc[...]
    if final:
        x2 = x2 * lax.rsqrt(jnp.mean(x2 * x2, axis=-1, keepdims=True) + EPS) * fg_ref[...]
    o_ref[...] = x2


def _combine(l, off, cnt, ye, slot, x1, mod5, final_g, row_of_tile, n_e, final):
    n, d = x1.shape
    tb = ROUTE_TB
    assert COMB_GRP * COMB_CH == LANES and COMB_MAXCH % COMB_GRP == 0
    assert COMB_MAXCH >= n_e * -(-(tb + SUBLANES - 1) // COMB_CH)
    grid_spec = pltpu.PrefetchScalarGridSpec(
        num_scalar_prefetch=2,
        grid=(n // tb,),
        in_specs=[
            pl.BlockSpec(memory_space=pl.ANY),
            pl.BlockSpec((tb, LANES), lambda j, o_, c_: (j, 0)),
            pl.BlockSpec((tb, d), lambda j, o_, c_: (j, 0)),
            pl.BlockSpec((None, None, None, 1, d), lambda j, o_, c_: (l, row_of_tile(j), 5, 0, 0)),
            pl.BlockSpec((1, d), lambda j, o_, c_: (0, 0)),
        ],
        out_specs=pl.BlockSpec((tb, d), lambda j, o_, c_: (j, 0)),
        scratch_shapes=[
            pltpu.VMEM((2, COMB_MAXCH, COMB_CH, d), F32),
            pltpu.VMEM((tb, d), F32),
            pltpu.SMEM((2, 4, LANES), jnp.int32),
            pltpu.SemaphoreType.DMA((2,)),
        ],
    )
    return pl.pallas_call(
        functools.partial(_combine_kernel, n_e=n_e, final=final),
        grid_spec=grid_spec,
        out_shape=jax.ShapeDtypeStruct((n, d), F32),
        compiler_params=_cparams(("arbitrary",)),
        name="combine",
    )(off, cnt, ye, slot, x1, mod5, final_g.reshape(1, d))


def _pos_embed_2d(n_tokens, d_model):
    rows = n_tokens // GRID_W
    r = jnp.broadcast_to(jnp.arange(rows, dtype=F32)[:, None], (rows, GRID_W)).reshape(-1)
    col = jnp.broadcast_to(jnp.arange(GRID_W, dtype=F32)[None, :], (rows, GRID_W)).reshape(-1)
    quarter = d_model // 4
    freq = jnp.exp(-math.log(10000.0) * jnp.arange(quarter, dtype=F32) / quarter)
    ar = r[:, None] * freq
    ac = col[:, None] * freq
    return jnp.concatenate([jnp.sin(ar), jnp.cos(ar), jnp.sin(ac), jnp.cos(ac)], axis=-1)


def kernel(x_prompt, x_sample, state_delta, state_lru, c, c_ctx, norm1_g, w_mod, b_mod, w_in, conv_qkv, dn_a_log, dn_dt_bias, dn_norm_g, w_dn_out, conv_lru_w, conv_lru_b, lru_wa, lru_ba, lru_wx, lru_bx, lru_lambda, w_lru_out, w_o, norm2_g, w_router, w_gate, w_up, w_down, final_g):
    bp, tp, d = x_prompt.shape
    bs, ts, _ = x_sample.shape
    depth = w_in.shape[0]
    heads, dk, dv = state_delta.shape[3:]
    qk = heads * dk
    vw = heads * dv
    lru_w = state_lru.shape[-1]
    n_experts = w_router.shape[-1]
    n_p, n_s = bp * tp, bs * ts
    n = n_p + n_s
    cap_p = 2 * n_p // n_experts
    cap_s = 2 * n_s // n_experts
    assert dk == LANES and dv == LANES and n_p % ts == 0 and ts % tp == 0

    xs = x_sample + _pos_embed_2d(ts, d)[None]
    x = jnp.concatenate([x_prompt.reshape(n_p, d), xs.reshape(n_s, d)], axis=0)

    cond8 = jnp.zeros((8, d), F32).at[0].set(c_ctx).at[1:1 + bs].set(c)
    mod = _modulation(cond8, w_mod, b_mod)
    mod = mod.reshape(depth, 8, 6, 1, d)

    n_small = 2 * N_DIR * heads
    c0 = 2 * qk + 2 * vw
    w_main, w_ba = _w_in_prep(w_in, c0, n_small)
    lx_blk = c0 // LANES
    ly_blk = lx_blk + lru_w // LANES
    ga_blk = (c0 + 2 * lru_w) // d
    gb_blk = ga_blk + 1
    w_dn_b = w_dn_out.astype(BF16)
    w_lru_b = w_lru_out.astype(BF16)
    w_o_b = w_o.astype(BF16)
    w_r_b = jnp.pad(w_router, ((0, 0), (0, 0), (0, LANES - n_experts))).astype(BF16)
    wa_b = lru_wa.astype(BF16)
    wx_b = lru_wx.astype(BF16)
    lane_pad = lambda a: jnp.pad(a.reshape(depth, 1, N_DIR * heads), ((0, 0), (0, 0), (0, LANES - N_DIR * heads)))
    al_v = lane_pad(dn_a_log)
    dt_v = lane_pad(dn_dt_bias)
    norm1_3 = norm1_g.reshape(depth, 1, d)
    norm2_3 = norm2_g.reshape(depth, 1, d)
    dn_norm_3 = dn_norm_g.reshape(depth, 1, dv)
    conv_lru_b3 = conv_lru_b.reshape(depth, 1, lru_w)

    tm_in = ts
    tiles_p_in = n_p // tm_in
    row_in = lambda i: jnp.where(i < tiles_p_in, 0, i - tiles_p_in + 1)
    tm_mg = 512
    tiles_p_mg = n_p // tm_mg
    per_seq = ts // tm_mg
    row_mg = lambda i: jnp.where(i < tiles_p_mg, 0, (i - tiles_p_mg) // per_seq + 1)
    tiles_p_cb = n_p // ROUTE_TB
    per_seq_cb = ts // ROUTE_TB
    row_cb = lambda i: jnp.where(i < tiles_p_cb, 0, (i - tiles_p_cb) // per_seq_cb + 1)
    assert n_p % ROUTE_TB == 0 and ts % ROUTE_TB == 0 and n_experts + 2 <= ROUTE_ROWS

    sd_out, sl_out = [], []
    for l in range(depth):
        cols, ba = _in_proj(l, x, norm1_3, mod, w_main, w_ba, row_in, tm_in)
        o_p, sd_p = _gdn_seq(l, cols, ba, conv_qkv, al_v, dt_v, dn_norm_3, None, 0, bp, tp, heads, dk)
        o_s, _ = _gdn_seq(l, cols, ba, conv_qkv, al_v, dt_v, dn_norm_3, state_delta, n_p, bs, ts, heads, dk)
        y_p, sl_p = _lru(l, cols, conv_lru_w, conv_lru_b3, wa_b, wx_b, lru_ba, lru_bx, lru_lambda, None,
                         0, bp, tp, lx_blk, ly_blk)
        y_s, _ = _lru(l, cols, conv_lru_w, conv_lru_b3, wa_b, wx_b, lru_ba, lru_bx, lru_lambda, state_lru,
                      n_p, bs, ts, lx_blk, ly_blk)
        x1, hn_aug, afft = _merge(l, o_p, o_s, y_p, y_s, cols, x, mod, norm2_3, w_dn_b, w_lru_b, w_o_b, w_r_b,
                                  row_mg, tm_mg, ga_blk, gb_blk, n_experts)
        thr_need = _thresholds(afft, ((0, n_p, cap_p), (n_p, n, cap_s)))
        slot, off, cnt, lst = _route(hn_aug, thr_need, n_p, cap_p, n_experts)
        nblk = n // ROUTE_TB
        off = off.reshape(nblk, LANES)
        cnt = cnt.reshape(nblk, LANES)
        xe_p = _gather(off, cnt, lst, hn_aug, 0, n_p, 0, cap_p)
        xe_s = _gather(off, cnt, lst, hn_aug, n_p, n_s, cap_p, cap_s)
        ye = _ffn(l, xe_p, xe_s, w_gate, w_up, w_down, 512)
        x = _combine(l, off, cnt, ye, slot, x1, mod, final_g, row_cb, n_experts, l == depth - 1)
        sd_out.append(sd_p)
        sl_out.append(sl_p)

    y = x
    y_prompt = y[:n_p].reshape(bp, tp, d)
    y_sample = y[n_p:].reshape(bs, ts, d)
    return (y_prompt, y_sample, jnp.stack(sd_out, axis=1), jnp.stack(sl_out, axis=1))
```

```python
import functools
import math

import jax
import jax.numpy as jnp
from jax import lax
from jax.experimental import pallas as pl
from jax.experimental.pallas import tpu as pltpu

F32 = jnp.float32
BF16 = jnp.bfloat16

EPS = 1e-6
CHUNK = 64
SUB = 8
GDN_BATCH = 32
CONV_LEFT = 2
CONV_W = 4
LRU_C = 8.0
LRU_GROUP = 4
N_DIR = 2
GRID_W = 64
LANES = 128
SUBLANES = 8
VMEM_LIMIT = 56 * 1024 * 1024


def _cparams(sem):
    return pltpu.CompilerParams(dimension_semantics=sem, vmem_limit_bytes=VMEM_LIMIT)


def _bdot(a, b):
    return jnp.dot(a.astype(BF16), b.astype(BF16), preferred_element_type=F32)


def _mod_kernel(c_ref, w_ref, b_ref, o_ref):
    c = c_ref[...]
    c = c * jax.nn.sigmoid(c)
    o_ref[...] = _bdot(c, w_ref[...]) + b_ref[...]


def _modulation(cond8, w_mod, b_mod):
    depth, d, n6 = w_mod.shape
    tn = 1536
    return pl.pallas_call(
        _mod_kernel,
        grid=(depth, n6 // tn),
        in_specs=[
            pl.BlockSpec((8, d), lambda l, j: (0, 0)),
            pl.BlockSpec((None, d, tn), lambda l, j: (l, 0, j)),
            pl.BlockSpec((None, 1, tn), lambda l, j: (l, 0, j)),
        ],
        out_specs=pl.BlockSpec((None, 8, tn), lambda l, j: (l, 0, j)),
        out_shape=jax.ShapeDtypeStruct((depth, 8, n6), F32),
        compiler_params=_cparams(("parallel", "parallel")),
        name="modulation",
    )(cond8, w_mod, b_mod.reshape(depth, 1, n6))


def _w_in_prep_kernel(w_ref, main_ref, ba_ref, *, c0, n_small):
    w = w_ref[...]
    rows = w.shape[0]
    main_ref[...] = jnp.concatenate([w[:, :c0], w[:, c0 + n_small:]], axis=1).astype(BF16)
    ba_ref[...] = jnp.concatenate(
        [w[:, c0:c0 + n_small], jnp.zeros((rows, LANES - n_small), F32)], axis=1).astype(BF16)


def _w_in_prep(w_in, c0, n_small):
    depth, d, ncol = w_in.shape
    tk = 256
    return pl.pallas_call(
        functools.partial(_w_in_prep_kernel, c0=c0, n_small=n_small),
        grid=(depth, d // tk),
        in_specs=[pl.BlockSpec((None, tk, ncol), lambda l, i: (l, i, 0))],
        out_specs=[
            pl.BlockSpec((None, tk, ncol - n_small), lambda l, i: (l, i, 0)),
            pl.BlockSpec((None, tk, LANES), lambda l, i: (l, i, 0)),
        ],
        out_shape=[
            jax.ShapeDtypeStruct((depth, d, ncol - n_small), BF16),
            jax.ShapeDtypeStruct((depth, d, LANES), BF16),
        ],
        compiler_params=_cparams(("parallel", "parallel")),
        name="w_in_prep",
    )(w_in)


IN_PROJ_TN = 1536


def _in_proj_kernel(x_ref, g_ref, sh_ref, sc_ref, w_ref, wba_ref, o_ref, oba_ref):
    x = x_ref[...]
    y = x * lax.rsqrt(jnp.mean(x * x, axis=-1, keepdims=True) + EPS) * g_ref[...]
    hn = (y * (1.0 + sc_ref[...]) + sh_ref[...]).astype(BF16)
    oba_ref[...] = jnp.dot(hn, wba_ref[...], preferred_element_type=F32)
    for c in range(0, o_ref.shape[1], IN_PROJ_TN):
        o_ref[:, c:c + IN_PROJ_TN] = jnp.dot(hn, w_ref[:, c:c + IN_PROJ_TN], preferred_element_type=F32)


def _in_proj(l, x, norm_g, mod5, w_main, w_ba, row_of_tile, tm):
    n, d = x.shape
    ncols = w_main.shape[2]
    assert ncols % IN_PROJ_TN == 0
    resident = pl.Buffered(1)
    return pl.pallas_call(
        _in_proj_kernel,
        grid=(n // tm,),
        in_specs=[
            pl.BlockSpec((tm, d), lambda i: (i, 0)),
            pl.BlockSpec((None, 1, d), lambda i: (l, 0, 0)),
            pl.BlockSpec((None, None, None, 1, d), lambda i: (l, row_of_tile(i), 0, 0, 0)),
            pl.BlockSpec((None, None, None, 1, d), lambda i: (l, row_of_tile(i), 1, 0, 0)),
            pl.BlockSpec((None, d, ncols), lambda i: (l, 0, 0), pipeline_mode=resident),
            pl.BlockSpec((None, d, LANES), lambda i: (l, 0, 0), pipeline_mode=resident),
        ],
        out_specs=[
            pl.BlockSpec((tm, ncols), lambda i: (i, 0)),
            pl.BlockSpec((tm, LANES), lambda i: (i, 0)),
        ],
        out_shape=[
            jax.ShapeDtypeStruct((n, ncols), F32),
            jax.ShapeDtypeStruct((n, LANES), F32),
        ],
        compiler_params=_cparams(("parallel",)),
        name="in_proj",
    )(x, norm_g, mod5, mod5, w_main, w_ba)


def _conv_rows(x, w, row):
    t = x.shape[0]
    acc = x * w[CONV_LEFT:CONV_LEFT + 1, :]
    for j in range(CONV_W):
        off = j - CONV_LEFT
        if off == 0:
            continue
        xs = pltpu.roll(x, (-off) % t, axis=0)
        valid = (row + off >= 0) & (row + off < t)
        acc = acc + jnp.where(valid, xs, 0.0) * w[j:j + 1, :]
    return acc


def _softplus(x):
    return jnp.maximum(x, 0.0) + jnp.log1p(jnp.exp(-jnp.abs(x)))


def _bmm(a, b):
    return jnp.einsum('nij,njk->nik', a.astype(BF16), b.astype(BF16), preferred_element_type=F32)


def _bmm_nt(a, b, precision=None):
    return jnp.einsum('nid,njd->nij', a, b, preferred_element_type=F32, precision=precision)


def _unit_tri_inverse(a, ii, jj):
    def same(b):
        return (ii // b) == (jj // b)

    eye = (ii == jj).astype(F32)
    d1 = jnp.where(same(SUB), a, 0.0)
    d2 = _bmm(d1, d1)
    d4 = _bmm(d2, d2)
    x = eye - d1
    x = x + _bmm(x, d2)
    x = x + _bmm(x, d4)
    b = SUB
    while b < CHUNK:
        o = jnp.where(same(2 * b) & jnp.logical_not(same(b)), a, 0.0)
        x = x - _bmm(_bmm(x, o), x)
        b *= 2
    return x


def _gdn_seq_kernel(*refs, t, has_s0, heads, dk, hg):
    n_st = N_DIR * heads
    n_in = 13 if has_s0 else 12
    ins, outs = refs[:n_in], refs[n_in:]
    (q_ref, k_ref, v_ref, z_ref, ba_ref, cq_ref, ck_ref, cv_ref, al_ref, dt_ref, ng_ref, oh_ref) = ins[:12]
    s0_ref = ins[12] if has_s0 else None
    o_ref, s_ref, x_scr, at_scr, kq_scr, kd_scr, vb_scr, cd_scr, o_scr = outs[:9]
    st_scrs = outs[9:9 + n_st]
    n = t // CHUNK
    row = lax.broadcasted_iota(jnp.int32, (t, LANES), 0)
    pos = row % CHUNK
    ii = lax.broadcasted_iota(jnp.int32, (CHUNK, CHUNK), 0)
    jj = lax.broadcasted_iota(jnp.int32, (CHUNK, CHUNK), 1)
    ba = ba_ref[...]
    al = al_ref[...]
    dtb = dt_ref[...]

    def conv_silu(x_ref, w_ref, h):
        y = _conv_rows(x_ref[:, h * LANES:(h + 1) * LANES], w_ref[:, h * LANES:(h + 1) * LANES], row)
        return y * jax.nn.sigmoid(y)

    def l2n(x):
        return x * lax.rsqrt(jnp.sum(x * x, axis=-1, keepdims=True) + EPS)

    nb = hg * n
    ii3 = lax.broadcasted_iota(jnp.int32, (N_DIR * nb, CHUNK, CHUNK), 1)
    jj3 = lax.broadcasted_iota(jnp.int32, (N_DIR * nb, CHUNK, CHUNK), 2)
    fwd = lax.broadcasted_iota(jnp.int32, (N_DIR * nb, CHUNK, CHUNK), 0) < nb
    ahead = jnp.where(fwd, ii3 - jj3, jj3 - ii3)
    tri = ahead >= 0
    strict = ahead > 0
    onehot0 = jnp.broadcast_to(oh_ref[...][None], (N_DIR * nb, CHUNK, LANES))

    for h0 in range(0, heads, hg):
        hs = range(h0, h0 + hg)
        as3 = lambda x: x.reshape(n, CHUNK, LANES)
        q3 = jnp.concatenate([as3(l2n(conv_silu(q_ref, cq_ref, h)) * (dk ** -0.5)) for h in hs], axis=0)
        k3 = jnp.concatenate([as3(l2n(conv_silu(k_ref, ck_ref, h))) for h in hs], axis=0)
        v3 = jnp.concatenate([as3(conv_silu(v_ref, cv_ref, h)) for h in hs], axis=0)
        k3b = k3.astype(BF16)
        gram = _bmm_nt(k3b, k3b)
        qk = _bmm_nt(q3.astype(BF16), k3b)
        decs, tots, betas = [], [], []
        for d in range(N_DIR):
            for h in hs:
                lb = d * heads + h
                la = 2 * heads + lb
                beta = jax.nn.sigmoid(ba[:, lb:lb + 1])
                g = -jnp.exp(al[:, lb:lb + 1]) * _softplus(ba[:, la:la + 1] + dtb[:, lb:lb + 1])
                dec = jnp.broadcast_to(g, (t, LANES))
                step = 1
                while step < CHUNK:
                    if d == 0:
                        sh = pltpu.roll(dec, step, axis=0)
                        dec = dec + jnp.where(pos >= step, sh, 0.0)
                    else:
                        sh = pltpu.roll(dec, t - step, axis=0)
                        dec = dec + jnp.where(pos < CHUNK - step, sh, 0.0)
                    step *= 2
                dec3 = as3(dec)
                decs.append(dec3)
                tots.append(jnp.broadcast_to(dec3[:, CHUNK - 1:CHUNK, :] if d == 0 else dec3[:, 0:1, :],
                                             (n, CHUNK, LANES)))
                betas.append(as3(jnp.broadcast_to(beta, (t, LANES))))
        dec3 = jnp.concatenate(decs, axis=0)
        tot3 = jnp.concatenate(tots, axis=0)
        beta3 = jnp.concatenate(betas, axis=0)
        both = lambda x: jnp.concatenate([x, x], axis=0)
        dec_row = _bmm_nt(onehot0, dec3, precision=lax.Precision.HIGHEST)
        dec_col = dec3[:, :, :CHUNK]
        gamma = jnp.where(tri, jnp.exp(jnp.where(tri, dec_col - dec_row, 0.0)), 0.0)
        a = jnp.where(strict, both(gram) * gamma, 0.0) * beta3[:, :, :CHUNK]
        xinv = _unit_tri_inverse(a, ii, jj).astype(BF16)
        attn = jnp.where(tri, both(qk) * gamma, 0.0).astype(BF16)
        edec = jnp.exp(dec3)
        k32, q32, v32 = both(k3), both(q3), both(v3)
        kbd = (k32 * (beta3 * edec)).astype(BF16)
        qd = (q32 * edec).astype(BF16)
        kd = jnp.swapaxes(k32 * jnp.exp(tot3 - dec3), 1, 2).astype(BF16)
        vb = v32 * beta3
        cd = jnp.exp(tot3[:, :SUBLANES, :])
        for d in range(N_DIR):
            src = slice(d * nb, (d + 1) * nb)
            dst = pl.ds((d * heads + h0) * n, nb)
            x_scr[dst] = xinv[src]
            at_scr[dst] = attn[src]
            kq_scr[dst, :CHUNK, :] = kbd[src]
            kq_scr[dst, CHUNK:, :] = qd[src]
            kd_scr[dst] = kd[src]
            vb_scr[dst] = vb[src]
            cd_scr[dst] = cd[src]

    for d in range(N_DIR):
        for h in range(heads):
            if has_s0:
                st_scrs[d * heads + h][...] = s0_ref[d, h]
            else:
                st_scrs[d * heads + h][...] = jnp.zeros((dk, LANES), F32)

    def scan(it, carry):
        chains = [(d * heads + h, (d * heads + h) * n + (it if d == 0 else n - 1 - it))
                  for h in range(heads) for d in range(N_DIR)]
        ss = [st_scrs[i][...] for i, _ in chains]
        rqs = [jnp.dot(kq_scr[b], s.astype(BF16), preferred_element_type=F32)
               for (_, b), s in zip(chains, ss)]
        vns = [jnp.dot(x_scr[b], (vb_scr[b] - rq[:CHUNK]).astype(BF16), preferred_element_type=F32).astype(BF16)
               for (_, b), rq in zip(chains, rqs)]
        for (i, b), s, rq, vn in zip(chains, ss, rqs, vns):
            o_scr[b] = rq[CHUNK:] + jnp.dot(at_scr[b], vn, preferred_element_type=F32)
            st_scrs[i][...] = s * cd_scr[b][0:1, :] + jnp.dot(kd_scr[b], vn, preferred_element_type=F32)
        return carry

    lax.fori_loop(0, n, scan, 0)

    for h in range(heads):
        for d in range(N_DIR):
            s_ref[d, h] = st_scrs[d * heads + h][...]
        of = o_scr[pl.ds(h * n, n)]
        ob = o_scr[pl.ds((heads + h) * n, n)]
        o = (of + ob).reshape(t, LANES)
        o = o * lax.rsqrt(jnp.mean(o * o, axis=-1, keepdims=True) + EPS) * ng_ref[...]
        z = z_ref[:, h * LANES:(h + 1) * LANES]
        o_ref[:, h * LANES:(h + 1) * LANES] = (o * (z * jax.nn.sigmoid(z))).astype(o_ref.dtype)


def _gdn_seq(l, cols, ba, conv_qkv, al, dtb, norm_g, s0, row0, nb, t, heads, dk):
    blk0 = row0 // t
    has_s0 = s0 is not None
    hw = heads * LANES
    n = t // CHUNK
    col = lambda off: pl.BlockSpec((t, hw), lambda b: (blk0 + b, off))
    cw = lambda off: pl.BlockSpec((None, CONV_W, hw), lambda b: (l, 0, off))
    vec = pl.BlockSpec((None, 1, LANES), lambda b: (l, 0, 0))
    in_specs = [
        col(0), col(1), col(2), col(3),
        pl.BlockSpec((t, LANES), lambda b: (blk0 + b, 0)),
        cw(0), cw(1), cw(2),
        vec, vec, vec,
        pl.BlockSpec((CHUNK, LANES), lambda b: (0, 0)),
    ]
    onehot0 = jnp.zeros((CHUNK, LANES), F32).at[:, 0].set(1.0)
    args = [cols, cols, cols, cols, ba, conv_qkv, conv_qkv, conv_qkv, al, dtb, norm_g, onehot0]
    if has_s0:
        in_specs.append(pl.BlockSpec((None, None, N_DIR, heads, dk, LANES), lambda b: (b, l, 0, 0, 0, 0)))
        args.append(s0)
    per = (N_DIR * heads * n,)
    hg = max(1, min(heads, GDN_BATCH // (N_DIR * n)))
    assert heads % hg == 0
    return pl.pallas_call(
        functools.partial(_gdn_seq_kernel, t=t, has_s0=has_s0, heads=heads, dk=dk, hg=hg),
        grid=(nb,),
        in_specs=in_specs,
        out_specs=[
            pl.BlockSpec((t, hw), lambda b: (b, 0)),
            pl.BlockSpec((None, N_DIR, heads, dk, LANES), lambda b: (b, 0, 0, 0, 0)),
        ],
        out_shape=[
            jax.ShapeDtypeStruct((nb * t, hw), BF16),
            jax.ShapeDtypeStruct((nb, N_DIR, heads, dk, LANES), F32),
        ],
        scratch_shapes=[
            pltpu.VMEM(per + (CHUNK, CHUNK), BF16),
            pltpu.VMEM(per + (CHUNK, CHUNK), BF16),
            pltpu.VMEM(per + (2 * CHUNK, LANES), BF16),
            pltpu.VMEM(per + (dk, CHUNK), BF16),
            pltpu.VMEM(per + (CHUNK, LANES), F32),
            pltpu.VMEM(per + (SUBLANES, LANES), F32),
            pltpu.VMEM(per + (CHUNK, LANES), F32),
        ] + [pltpu.VMEM((dk, LANES), F32)] * (N_DIR * heads),
        compiler_params=_cparams(("parallel",)),
        name=f"gdn_t{t}",
    )(*args)


def _lru_kernel(*refs, t, has_h0):
    if has_h0:
        (lx_ref, ly_ref, cw_ref, cb_ref, wa_ref, wx_ref, ba_ref, bx_ref, lam_ref, h0_ref,
         y_ref, last_ref, a_scr, b_scr, h_scr) = refs
    else:
        (lx_ref, ly_ref, cw_ref, cb_ref, wa_ref, wx_ref, ba_ref, bx_ref, lam_ref,
         y_ref, last_ref, a_scr, b_scr, h_scr) = refs
        h0_ref = None
    nblk = t // SUBLANES
    wl = lx_ref.shape[1]
    row = lax.broadcasted_iota(jnp.int32, (t, wl), 0)
    sub = lax.broadcasted_iota(jnp.int32, (nblk, SUBLANES, wl), 1)
    u = _conv_rows(lx_ref[...], cw_ref[...], row) + cb_ref[...]
    ub = u.astype(BF16)

    def block_diag(w_ref, d):
        return jnp.concatenate(
            [jnp.dot(ub[:, c * LANES:(c + 1) * LANES], w_ref[d, c], preferred_element_type=F32)
             for c in range(wl // LANES)], axis=1)

    for d in range(N_DIR):
        r = jax.nn.sigmoid(block_diag(wa_ref, d) + ba_ref[d:d + 1, :])
        i = jax.nn.sigmoid(block_diag(wx_ref, d) + bx_ref[d:d + 1, :])
        log_a = (LRU_C * r) * (-_softplus(-lam_ref[d:d + 1, :]))
        a = jnp.exp(log_a)
        b = jnp.sqrt(1.0 - a * a) * (i * u)
        a3 = a.reshape(nblk, SUBLANES, wl)
        b3 = b.reshape(nblk, SUBLANES, wl)
        step = 1
        while step < SUBLANES:
            if d == 0:
                a_s = pltpu.roll(a3, step, axis=1)
                b_s = pltpu.roll(b3, step, axis=1)
                m = sub >= step
            else:
                a_s = pltpu.roll(a3, SUBLANES - step, axis=1)
                b_s = pltpu.roll(b3, SUBLANES - step, axis=1)
                m = sub < SUBLANES - step
            b3 = b3 + a3 * jnp.where(m, b_s, 0.0)
            a3 = a3 * jnp.where(m, a_s, 1.0)
            step *= 2
        a_scr[d] = a3
        b_scr[d] = b3

    if has_h0:
        h_init = (jnp.broadcast_to(h0_ref[0:1, :], (SUBLANES, wl)),
                  jnp.broadcast_to(h0_ref[1:2, :], (SUBLANES, wl)))
    else:
        h_init = (jnp.zeros((SUBLANES, wl), F32), jnp.zeros((SUBLANES, wl), F32))

    def body(it, carry):
        hf, hb = carry
        kf = it
        kb = nblk - 1 - it
        new_f = b_scr[0, kf] + a_scr[0, kf] * hf
        new_b = b_scr[1, kb] + a_scr[1, kb] * hb
        h_scr[0, kf] = new_f
        h_scr[1, kb] = new_b
        hf = jnp.broadcast_to(new_f[SUBLANES - 1:SUBLANES, :], (SUBLANES, wl))
        hb = jnp.broadcast_to(new_b[0:1, :], (SUBLANES, wl))
        return hf, hb

    hf, hb = lax.fori_loop(0, nblk, body, h_init, unroll=4)
    last_ref[0:1, :] = hf[0:1, :]
    last_ref[1:2, :] = hb[0:1, :]
    rec = (h_scr[0] + h_scr[1]).reshape(t, wl)
    y_ref[...] = (jax.nn.gelu(ly_ref[...]) * rec).astype(y_ref.dtype)


def _lru(l, cols, conv_w, conv_b, wa, wx, ba, bx, lam, h0, row0, nb, t, lx_blk, ly_blk):
    nblocks = wa.shape[2]
    w = nblocks * LANES
    cg = LRU_GROUP
    wl = cg * LANES
    blk0 = row0 // t
    has_h0 = h0 is not None
    in_specs = [
        pl.BlockSpec((t, wl), lambda b, c: (blk0 + b, lx_blk // cg + c)),
        pl.BlockSpec((t, wl), lambda b, c: (blk0 + b, ly_blk // cg + c)),
        pl.BlockSpec((None, CONV_W, wl), lambda b, c: (l, 0, c)),
        pl.BlockSpec((None, 1, wl), lambda b, c: (l, 0, c)),
        pl.BlockSpec((None, N_DIR, cg, LANES, LANES), lambda b, c: (l, 0, c, 0, 0)),
        pl.BlockSpec((None, N_DIR, cg, LANES, LANES), lambda b, c: (l, 0, c, 0, 0)),
        pl.BlockSpec((None, N_DIR, wl), lambda b, c: (l, 0, c)),
        pl.BlockSpec((None, N_DIR, wl), lambda b, c: (l, 0, c)),
        pl.BlockSpec((None, N_DIR, wl), lambda b, c: (l, 0, c)),
    ]
    args = [cols, cols, conv_w, conv_b, wa, wx, ba, bx, lam]
    if has_h0:
        in_specs.append(pl.BlockSpec((None, None, N_DIR, wl), lambda b, c: (b, l, 0, c)))
        args.append(h0)
    nblk = t // SUBLANES
    assert lx_blk % cg == 0 and ly_blk % cg == 0 and nblocks % cg == 0
    return pl.pallas_call(
        functools.partial(_lru_kernel, t=t, has_h0=has_h0),
        grid=(nb, nblocks // cg),
        in_specs=in_specs,
        out_specs=[
            pl.BlockSpec((t, wl), lambda b, c: (b, c)),
            pl.BlockSpec((None, N_DIR, wl), lambda b, c: (b, 0, c)),
        ],
        out_shape=[
            jax.ShapeDtypeStruct((nb * t, w), BF16),
            jax.ShapeDtypeStruct((nb, N_DIR, w), F32),
        ],
        scratch_shapes=[pltpu.VMEM((N_DIR, nblk, SUBLANES, wl), F32)] * 3,
        compiler_params=_cparams(("parallel", "parallel")),
        name=f"lru_t{t}",
    )(*args)


def _merge_kernel(oap_ref, oas_ref, obp_ref, obs_ref, ga_ref, gb_ref, x_ref, g1_ref, sh2_ref, sc2_ref,
                  n2_ref, wdn_ref, wlru_ref, wo_ref, wr_ref, x1_ref, hn_ref, afft_ref, *, n_experts, tiles_p):
    d = x_ref.shape[1]
    is_p = pl.program_id(0) < tiles_p
    oa = jnp.where(is_p, oap_ref[...], oas_ref[...])
    ob = jnp.where(is_p, obp_ref[...], obs_ref[...])
    ya = jnp.dot(oa, wdn_ref[...], preferred_element_type=F32)
    yb = jnp.dot(ob, wlru_ref[...], preferred_element_type=F32)
    mix = jax.nn.sigmoid(ga_ref[...]) * ya + jax.nn.sigmoid(gb_ref[...]) * yb
    mix = jnp.dot(mix.astype(BF16), wo_ref[...], preferred_element_type=F32)
    x1 = x_ref[...] + g1_ref[...] * mix
    x1_ref[...] = x1
    y = x1 * lax.rsqrt(jnp.mean(x1 * x1, axis=-1, keepdims=True) + EPS) * n2_ref[...]
    hn = y * (1.0 + sc2_ref[...]) + sh2_ref[...]
    logits = jnp.dot(hn.astype(BF16), wr_ref[...], preferred_element_type=F32)
    lane = lax.broadcasted_iota(jnp.int32, logits.shape, 1)
    logits = jnp.where(lane < n_experts, logits, -jnp.inf)
    e = jnp.exp(logits - jnp.max(logits, axis=-1, keepdims=True))
    aff = e / jnp.sum(e, axis=-1, keepdims=True)
    hn_ref[:, :d] = hn
    hn_ref[:, d:] = aff
    afft_ref[...] = aff.T[:n_experts, :]


def _merge(l, o_p, o_s, y_p, y_s, cols, x, mod5, norm2_g, w_dn, w_lru, w_o, w_r, row_of_tile, tm,
           ga_blk, gb_blk, n_experts):
    n, d = x.shape
    dv = o_p.shape[1]
    w = y_p.shape[1]
    tiles_p = o_p.shape[0] // tm
    modspec = lambda k: pl.BlockSpec((None, None, None, 1, d), lambda i: (l, row_of_tile(i), k, 0, 0))
    layer = lambda a: pl.BlockSpec((None,) + a.shape[1:], lambda i: (l,) + (0,) * (a.ndim - 1))
    p_tile = lambda width: pl.BlockSpec((tm, width), lambda i: (jnp.minimum(i, tiles_p - 1), 0))
    s_tile = lambda width: pl.BlockSpec((tm, width), lambda i: (jnp.maximum(i - tiles_p, 0), 0))
    return pl.pallas_call(
        functools.partial(_merge_kernel, n_experts=n_experts, tiles_p=tiles_p),
        grid=(n // tm,),
        in_specs=[
            p_tile(dv), s_tile(dv), p_tile(w), s_tile(w),
            pl.BlockSpec((tm, d), lambda i: (i, ga_blk)),
            pl.BlockSpec((tm, d), lambda i: (i, gb_blk)),
            pl.BlockSpec((tm, d), lambda i: (i, 0)),
            modspec(2), modspec(3), modspec(4),
            layer(norm2_g), layer(w_dn), layer(w_lru), layer(w_o), layer(w_r),
        ],
        out_specs=[
            pl.BlockSpec((tm, d), lambda i: (i, 0)),
            pl.BlockSpec((tm, d + LANES), lambda i: (i, 0)),
            pl.BlockSpec((n_experts, tm), lambda i: (0, i)),
        ],
        out_shape=[
            jax.ShapeDtypeStruct((n, d), F32),
            jax.ShapeDtypeStruct((n, d + LANES), F32),
            jax.ShapeDtypeStruct((n_experts, n), F32),
        ],
        compiler_params=_cparams(("parallel",)),
        name="merge",
    )(o_p, o_s, y_p, y_s, cols, cols, x, mod5, mod5, mod5, norm2_g, w_dn, w_lru, w_o, w_r)


def _ffn_kernel(xp_ref, xs_ref, wg_ref, wu_ref, wd_ref, o_ref, xb_scr):
    f = pl.program_id(1)
    cap_p = xp_ref.shape[0]
    d = xb_scr.shape[1]

    @pl.when(f == 0)
    def _():
        xb_scr[:cap_p, :] = xp_ref[:, :d].astype(BF16)
        xb_scr[cap_p:, :] = xs_ref[:, :d].astype(BF16)
        o_ref[...] = jnp.zeros_like(o_ref)

    x = xb_scr[...]
    g = jnp.dot(x, wg_ref[...].astype(BF16), preferred_element_type=F32)
    u = jnp.dot(x, wu_ref[...].astype(BF16), preferred_element_type=F32)
    hid = ((g * jax.nn.sigmoid(g)) * u).astype(BF16)
    o_ref[...] += jnp.dot(hid, wd_ref[...].astype(BF16), preferred_element_type=F32)

    @pl.when(f == pl.num_programs(1) - 1)
    def _():
        e = pl.program_id(0)
        for ref, r0 in ((xp_ref, 0), (xs_ref, cap_p)):
            aff = ref[:, d:]
            lane = lax.broadcasted_iota(jnp.int32, aff.shape, 1)
            gv = jnp.sum(jnp.where(lane == e, aff, 0.0), axis=1, keepdims=True)
            rows = ref.shape[0]
            o_ref[r0:r0 + rows, :] = o_ref[r0:r0 + rows, :] * gv


def _ffn(l, xe_p, xe_s, w_gate, w_up, w_down, tf):
    e, cap_p, da = xe_p.shape
    d = da - LANES
    cap_s = xe_s.shape[1]
    r = cap_p + cap_s
    ff = w_gate.shape[3]
    return pl.pallas_call(
        _ffn_kernel,
        grid=(e, ff // tf),
        in_specs=[
            pl.BlockSpec((None, cap_p, da), lambda i, f: (i, 0, 0)),
            pl.BlockSpec((None, cap_s, da), lambda i, f: (i, 0, 0)),
            pl.BlockSpec((None, None, d, tf), lambda i, f: (l, i, 0, f)),
            pl.BlockSpec((None, None, d, tf), lambda i, f: (l, i, 0, f)),
            pl.BlockSpec((None, None, tf, d), lambda i, f: (l, i, f, 0)),
        ],
        out_specs=pl.BlockSpec((None, r, d), lambda i, f: (i, 0, 0)),
        out_shape=jax.ShapeDtypeStruct((e, r, d), F32),
        scratch_shapes=[pltpu.VMEM((r, d), BF16)],
        compiler_params=_cparams(("parallel", "arbitrary")),
        name="expert_ffn",
    )(xe_p, xe_s, w_gate, w_up, w_down)


ROUTE_TB = 128


def _thr_kernel(afft_ref, o_ref, *, groups):
    n_e = afft_ref.shape[0]
    sub = lax.broadcasted_iota(jnp.int32, (n_e, LANES), 0)
    lane = lax.broadcasted_iota(jnp.int32, (n_e, LANES), 1)
    rows = []
    for lo, hi, cap in groups:
        a = afft_ref[:, lo:hi]
        above_all = 4.0

        def count_ge(v, a=a):
            return jnp.sum((a >= v).astype(F32), axis=1, keepdims=True)

        def count_gt(v, a=a):
            return jnp.sum((a > v).astype(F32), axis=1, keepdims=True)

        def bisect(i, lh, cap=cap, count_ge=count_ge):
            lo_v, hi_v = lh
            mid = 0.5 * (lo_v + hi_v)
            ok = count_ge(mid) >= cap
            return jnp.where(ok, mid, lo_v), jnp.where(ok, hi_v, mid)

        lo_v, _ = lax.fori_loop(0, 48, bisect, (jnp.zeros((n_e, 1), F32), jnp.full((n_e, 1), 2.0, F32)))
        thr = jnp.min(jnp.where(a >= lo_v, a, above_all), axis=1, keepdims=True)

        def not_done(thr, cap=cap, count_gt=count_gt):
            return jnp.max(count_gt(thr)) >= cap

        def step_up(thr, a=a, cap=cap, count_gt=count_gt):
            nxt = jnp.min(jnp.where(a > thr, a, above_all), axis=1, keepdims=True)
            return jnp.where(count_gt(thr) >= cap, nxt, thr)

        thr = lax.while_loop(not_done, step_up, thr)
        need = cap - count_gt(thr)
        for col in (thr, need):
            m = jnp.where(sub == lane, jnp.broadcast_to(col, (n_e, LANES)), 0.0)
            rows.append(jnp.sum(m, axis=0, keepdims=True))
    rows.append(jnp.zeros((SUBLANES - len(rows), LANES), F32))
    o_ref[...] = jnp.concatenate(rows, axis=0)


def _thresholds(afft, groups):
    n_e, n = afft.shape
    return pl.pallas_call(
        functools.partial(_thr_kernel, groups=groups),
        grid=(1,),
        in_specs=[pl.BlockSpec((n_e, n), lambda i: (0, 0))],
        out_specs=pl.BlockSpec((SUBLANES, LANES), lambda i: (0, 0)),
        out_shape=jax.ShapeDtypeStruct((SUBLANES, LANES), F32),
        compiler_params=_cparams(("arbitrary",)),
        name="route_thresholds",
    )(afft)


def _route_kernel(aff_ref, tn_ref, ls_ref, slot_ref, off_ref, cnt_ref, lst_ref,
                  run_eq, run_sel, *, nblk_p, cap_p, n_e):
    tb = ROUTE_TB
    j = pl.program_id(0)

    @pl.when(j == 0)
    def _():
        run_eq[...] = jnp.zeros_like(run_eq)
        run_sel[...] = jnp.zeros_like(run_sel)

    @pl.when(j == nblk_p)
    def _():
        run_eq[...] = jnp.zeros_like(run_eq)
        run_sel[...] = jnp.full_like(run_sel, float(cap_p))

    g = (j >= nblk_p).astype(jnp.int32)
    thr = tn_ref[pl.ds(2 * g, 1), :]
    need = tn_ref[pl.ds(2 * g + 1, 1), :]
    a = aff_ref[...]
    lane = lax.broadcasted_iota(jnp.int32, (tb, LANES), 1)
    valid = lane < n_e
    gt = (a > thr) & valid
    eq = (a == thr) & valid
    eqf = eq.astype(F32)
    ls = ls_ref[...]
    eq_rank = run_eq[...] + jnp.dot(ls, eqf.astype(BF16), preferred_element_type=F32)
    sel = gt | (eq & (eq_rank < need))
    self_ = sel.astype(F32)
    pos = jnp.dot(ls, self_.astype(BF16), preferred_element_type=F32)
    cnt = jnp.sum(self_, axis=0, keepdims=True)
    off = run_sel[...]
    run_eq[...] = run_eq[...] + jnp.sum(eqf, axis=0, keepdims=True)
    run_sel[...] = off + cnt
    slot_ref[...] = jnp.where(sel, off + pos, -1.0)
    off_ref[...] = off.astype(jnp.int32)
    cnt_ref[...] = cnt.astype(jnp.int32)

    tok = (lax.broadcasted_iota(jnp.int32, (tb, LANES), 0) + j * tb).astype(F32)
    lanef = lane.astype(F32)
    for e in range(n_e):
        pe = jnp.broadcast_to(pos[:, e:e + 1], (tb, LANES))
        se = jnp.broadcast_to(self_[:, e:e + 1], (tb, LANES))
        hit = (pe == lanef) & (se > 0.0)
        lst_ref[e] = jnp.sum(jnp.where(hit, tok, 0.0), axis=0, keepdims=True).astype(jnp.int32)


def _route(hn_aug, thr_need, n_p, cap_p, n_e):
    n = hn_aug.shape[0]
    aff_blk = hn_aug.shape[1] // LANES - 1
    tb = ROUTE_TB
    nblk = n // tb
    ls = jnp.tril(jnp.ones((tb, tb), F32), -1).astype(BF16)
    blk_row = pl.BlockSpec((None, 1, LANES), lambda j: (j, 0, 0))
    return pl.pallas_call(
        functools.partial(_route_kernel, nblk_p=n_p // tb, cap_p=cap_p, n_e=n_e),
        grid=(nblk,),
        in_specs=[
            pl.BlockSpec((tb, LANES), lambda j: (j, aff_blk)),
            pl.BlockSpec((SUBLANES, LANES), lambda j: (0, 0)),
            pl.BlockSpec((tb, tb), lambda j: (0, 0)),
        ],
        out_specs=[
            pl.BlockSpec((tb, LANES), lambda j: (j, 0)),
            blk_row,
            blk_row,
            pl.BlockSpec((n_e, None, 1, LANES), lambda j: (0, j, 0, 0)),
        ],
        out_shape=[
            jax.ShapeDtypeStruct((n, LANES), F32),
            jax.ShapeDtypeStruct((nblk, 1, LANES), jnp.int32),
            jax.ShapeDtypeStruct((nblk, 1, LANES), jnp.int32),
            jax.ShapeDtypeStruct((n_e, nblk, 1, LANES), jnp.int32),
        ],
        scratch_shapes=[pltpu.VMEM((1, LANES), F32), pltpu.VMEM((1, LANES), F32)],
        compiler_params=_cparams(("arbitrary",)),
        name="route_slots",
    )(hn_aug, thr_need, ls)


GATHER_UNROLL = 4


def _gather_kernel(off_ref, cnt_ref, lst_hbm, hn_hbm, o_ref, hn_scr, lst_smem, sem_h, sem_l,
                   *, row0, slot0, blk0, nblk_g):
    e = pl.program_id(0)
    load_lst = pltpu.make_async_copy(lst_hbm.at[e, pl.ds(blk0, nblk_g)], lst_smem, sem_l)
    load_lst.start()

    @pl.when(e == 0)
    def _():
        rows = hn_scr.shape[0]
        load = pltpu.make_async_copy(hn_hbm.at[pl.ds(row0, rows)], hn_scr, sem_h)
        load.start()
        load.wait()

    load_lst.wait()

    def block(jb, carry):
        c = cnt_ref[blk0 + jb, e]
        o = off_ref[blk0 + jb, e] - slot0

        def copy_row(q):
            t = lst_smem[jb, q] - row0
            o_ref[pl.ds(o + q, 1), :] = hn_scr[pl.ds(t, 1), :]

        def rows(qq, carry2):
            for u in range(GATHER_UNROLL):
                copy_row(qq * GATHER_UNROLL + u)
            return carry2

        def row(q, carry2):
            copy_row(q)
            return carry2

        full = c // GATHER_UNROLL
        lax.fori_loop(0, full, rows, 0)
        lax.fori_loop(full * GATHER_UNROLL, c, row, 0)
        return carry

    lax.fori_loop(0, nblk_g, block, 0)


def _gather(off, cnt, lst, hn, row0, rows, slot0, cap):
    n_e = lst.shape[0]
    width = hn.shape[1]
    blk0 = row0 // ROUTE_TB
    nblk_g = rows // ROUTE_TB
    grid_spec = pltpu.PrefetchScalarGridSpec(
        num_scalar_prefetch=2,
        grid=(n_e,),
        in_specs=[pl.BlockSpec(memory_space=pl.ANY), pl.BlockSpec(memory_space=pl.ANY)],
        out_specs=pl.BlockSpec((None, cap, width), lambda e, o_, c_: (e, 0, 0)),
        scratch_shapes=[
            pltpu.VMEM((rows, width), F32),
            pltpu.SMEM((nblk_g, LANES), jnp.int32),
            pltpu.SemaphoreType.DMA(()),
            pltpu.SemaphoreType.DMA(()),
        ],
    )
    return pl.pallas_call(
        functools.partial(_gather_kernel, row0=row0, slot0=slot0, blk0=blk0, nblk_g=nblk_g),
        grid_spec=grid_spec,
        out_shape=jax.ShapeDtypeStruct((n_e, cap, width), F32),
        compiler_params=_cparams(("arbitrary",)),
        name=f"gather_rows{rows}",
    )(off, cnt, lst.reshape(n_e, -1, LANES), hn)


COMB_CH = 32
COMB_GRP = 4
COMB_MAXCH = 80


def _combine_kernel(off_ref, cnt_ref, y_hbm, slot_ref, x1_ref, g2_ref, fg_ref, o_ref,
                    buf, acc, desc, sems, *, n_e, final):
    tb = ROUTE_TB
    j = pl.program_id(0)
    nblk = pl.num_programs(0)
    r_total = y_hbm.shape[1]
    par = j % 2

    def issue(jj, p):
        s = jnp.int32(0)
        for e in range(n_e):
            o = off_ref[jj, e]
            c = cnt_ref[jj, e]
            st8 = (o // SUBLANES) * SUBLANES
            nch = jnp.where(c > 0, (o - st8 + c + COMB_CH - 1) // COMB_CH, 0)

            def one(k, s, e=e, st8=st8):
                lo_row = st8 + k * COMB_CH
                base = jnp.minimum(lo_row, r_total - COMB_CH)
                pltpu.make_async_copy(y_hbm.at[e, pl.ds(base, COMB_CH)], buf.at[p, s], sems.at[p]).start()
                desc[p, 0, s] = e
                desc[p, 1, s] = lo_row
                desc[p, 2, s] = base
                return s + 1

            s = lax.fori_loop(0, nch, one, s)
        desc[p, 3, 0] = s

    @pl.when(j == 0)
    def _():
        buf[...] = jnp.zeros_like(buf)

        def clear(i, carry):
            for p in range(2):
                for row in range(4):
                    desc[p, row, i] = 0
            return carry

        lax.fori_loop(0, LANES, clear, 0)
        issue(j, par)

    @pl.when(j + 1 < nblk)
    def _():
        issue(j + 1, 1 - par)

    n_ch = desc[par, 3, 0]

    def drain(k, carry):
        pltpu.make_async_copy(y_hbm.at[0, pl.ds(0, COMB_CH)], buf.at[par, k], sems.at[par]).wait()
        return carry

    lax.fori_loop(0, n_ch, drain, 0)

    slot = slot_ref[...]
    lane = lax.broadcasted_iota(jnp.int32, (tb, LANES), 1)
    lanef = lane.astype(F32)
    acc[...] = jnp.zeros_like(acc)

    def group(gi, carry):
        hit = jnp.zeros((tb, LANES), jnp.bool_)
        for c4 in range(COMB_GRP):
            s = gi * COMB_GRP + c4
            e = desc[par, 0, s]
            lo_row = desc[par, 1, s].astype(F32)
            base = desc[par, 2, s].astype(F32)
            col = jnp.sum(jnp.where(lane == e, slot, 0.0), axis=1, keepdims=True)
            colb = jnp.broadcast_to(col, (tb, LANES))
            in_chunk = (lane >= c4 * COMB_CH) & (lane < (c4 + 1) * COMB_CH)
            match = (colb == base + (lanef - float(c4 * COMB_CH))) & (colb >= lo_row)
            hit = hit | (in_chunk & match & (s < n_ch))
        eb = jnp.where(hit, 1.0, 0.0).astype(BF16)
        yg = buf[par, pl.ds(gi * COMB_GRP, COMB_GRP)].reshape(COMB_GRP * COMB_CH, -1)
        y1 = yg.astype(BF16)
        r1 = yg - y1.astype(F32)
        y2 = r1.astype(BF16)
        y3 = (r1 - y2.astype(F32)).astype(BF16)
        acc[...] += jnp.dot(jnp.concatenate([eb, eb, eb], axis=1), jnp.concatenate([y1, y2, y3], axis=0),
                            preferred_element_type=F32)
        return carry

    lax.fori_loop(0, (n_ch + COMB_GRP - 1) // COMB_GRP, group, 0)

    x2 = x1_ref[...] + g2_ref[...] * acc[...]
    if final:
        x2 = x2 * lax.rsqrt(jnp.mean(x2 * x2, axis=-1, keepdims=True) + EPS) * fg_ref[...]
    o_ref[...] = x2


def _combine(l, off, cnt, ye, slot, x1, mod5, final_g, row_of_tile, n_e, final):
    n, d = x1.shape
    tb = ROUTE_TB
    assert COMB_GRP * COMB_CH == LANES and COMB_MAXCH % COMB_GRP == 0
    assert COMB_MAXCH >= n_e * -(-(tb + SUBLANES - 1) // COMB_CH)
    grid_spec = pltpu.PrefetchScalarGridSpec(
        num_scalar_prefetch=2,
        grid=(n // tb,),
        in_specs=[
            pl.BlockSpec(memory_space=pl.ANY),
            pl.BlockSpec((tb, LANES), lambda j, o_, c_: (j, 0)),
            pl.BlockSpec((tb, d), lambda j, o_, c_: (j, 0)),
            pl.BlockSpec((None, None, None, 1, d), lambda j, o_, c_: (l, row_of_tile(j), 5, 0, 0)),
            pl.BlockSpec((1, d), lambda j, o_, c_: (0, 0)),
        ],
        out_specs=pl.BlockSpec((tb, d), lambda j, o_, c_: (j, 0)),
        scratch_shapes=[
            pltpu.VMEM((2, COMB_MAXCH, COMB_CH, d), F32),
            pltpu.VMEM((tb, d), F32),
            pltpu.SMEM((2, 4, LANES), jnp.int32),
            pltpu.SemaphoreType.DMA((2,)),
        ],
    )
    return pl.pallas_call(
        functools.partial(_combine_kernel, n_e=n_e, final=final),
        grid_spec=grid_spec,
        out_shape=jax.ShapeDtypeStruct((n, d), F32),
        compiler_params=_cparams(("arbitrary",)),
        name="combine",
    )(off, cnt, ye, slot, x1, mod5, final_g.reshape(1, d))


def _pos_embed_2d(n_tokens, d_model):
    rows = n_tokens // GRID_W
    r = jnp.broadcast_to(jnp.arange(rows, dtype=F32)[:, None], (rows, GRID_W)).reshape(-1)
    col = jnp.broadcast_to(jnp.arange(GRID_W, dtype=F32)[None, :], (rows, GRID_W)).reshape(-1)
    quarter = d_model // 4
    freq = jnp.exp(-math.log(10000.0) * jnp.arange(quarter, dtype=F32) / quarter)
    ar = r[:, None] * freq
    ac = col[:, None] * freq
    return jnp.concatenate([jnp.sin(ar), jnp.cos(ar), jnp.sin(ac), jnp.cos(ac)], axis=-1)


def kernel(x_prompt, x_sample, state_delta, state_lru, c, c_ctx, norm1_g, w_mod, b_mod, w_in, conv_qkv, dn_a_log, dn_dt_bias, dn_norm_g, w_dn_out, conv_lru_w, conv_lru_b, lru_wa, lru_ba, lru_wx, lru_bx, lru_lambda, w_lru_out, w_o, norm2_g, w_router, w_gate, w_up, w_down, final_g):
    bp, tp, d = x_prompt.shape
    bs, ts, _ = x_sample.shape
    depth = w_in.shape[0]
    heads, dk, dv = state_delta.shape[3:]
    qk = heads * dk
    vw = heads * dv
    lru_w = state_lru.shape[-1]
    n_experts = w_router.shape[-1]
    n_p, n_s = bp * tp, bs * ts
    n = n_p + n_s
    cap_p = 2 * n_p // n_experts
    cap_s = 2 * n_s // n_experts
    assert dk == LANES and dv == LANES and n_p % ts == 0 and ts % tp == 0

    xs = x_sample + _pos_embed_2d(ts, d)[None]
    x = jnp.concatenate([x_prompt.reshape(n_p, d), xs.reshape(n_s, d)], axis=0)

    cond8 = jnp.zeros((8, d), F32).at[0].set(c_ctx).at[1:1 + bs].set(c)
    mod = _modulation(cond8, w_mod, b_mod)
    mod = mod.reshape(depth, 8, 6, 1, d)

    n_small = 2 * N_DIR * heads
    c0 = 2 * qk + 2 * vw
    w_main, w_ba = _w_in_prep(w_in, c0, n_small)
    lx_blk = c0 // LANES
    ly_blk = lx_blk + lru_w // LANES
    ga_blk = (c0 + 2 * lru_w) // d
    gb_blk = ga_blk + 1
    w_dn_b = w_dn_out.astype(BF16)
    w_lru_b = w_lru_out.astype(BF16)
    w_o_b = w_o.astype(BF16)
    w_r_b = jnp.pad(w_router, ((0, 0), (0, 0), (0, LANES - n_experts))).astype(BF16)
    wa_b = lru_wa.astype(BF16)
    wx_b = lru_wx.astype(BF16)
    lane_pad = lambda a: jnp.pad(a.reshape(depth, 1, N_DIR * heads), ((0, 0), (0, 0), (0, LANES - N_DIR * heads)))
    al_v = lane_pad(dn_a_log)
    dt_v = lane_pad(dn_dt_bias)
    norm1_3 = norm1_g.reshape(depth, 1, d)
    norm2_3 = norm2_g.reshape(depth, 1, d)
    dn_norm_3 = dn_norm_g.reshape(depth, 1, dv)
    conv_lru_b3 = conv_lru_b.reshape(depth, 1, lru_w)

    tm_mg = 512
    tiles_p_mg = n_p // tm_mg
    per_seq = ts // tm_mg
    row_mg = lambda i: jnp.where(i < tiles_p_mg, 0, (i - tiles_p_mg) // per_seq + 1)
    tiles_p_cb = n_p // ROUTE_TB
    per_seq_cb = ts // ROUTE_TB
    row_cb = lambda i: jnp.where(i < tiles_p_cb, 0, (i - tiles_p_cb) // per_seq_cb + 1)
    assert n_p % tm_mg == 0 and ts % tm_mg == 0 and n_p % ROUTE_TB == 0 and ts % ROUTE_TB == 0
    assert n_experts <= LANES and d % LANES == 0

    sd_out, sl_out = [], []
    for l in range(depth):
        cols, ba = _in_proj(l, x, norm1_3, mod, w_main, w_ba, row_mg, tm_mg)
        o_p, sd_p = _gdn_seq(l, cols, ba, conv_qkv, al_v, dt_v, dn_norm_3, None, 0, bp, tp, heads, dk)
        o_s, _ = _gdn_seq(l, cols, ba, conv_qkv, al_v, dt_v, dn_norm_3, state_delta, n_p, bs, ts, heads, dk)
        y_p, sl_p = _lru(l, cols, conv_lru_w, conv_lru_b3, wa_b, wx_b, lru_ba, lru_bx, lru_lambda, None,
                         0, bp, tp, lx_blk, ly_blk)
        y_s, _ = _lru(l, cols, conv_lru_w, conv_lru_b3, wa_b, wx_b, lru_ba, lru_bx, lru_lambda, state_lru,
                      n_p, bs, ts, lx_blk, ly_blk)
        x1, hn_aug, afft = _merge(l, o_p, o_s, y_p, y_s, cols, x, mod, norm2_3, w_dn_b, w_lru_b, w_o_b, w_r_b,
                                  row_mg, tm_mg, ga_blk, gb_blk, n_experts)
        thr_need = _thresholds(afft, ((0, n_p, cap_p), (n_p, n, cap_s)))
        slot, off, cnt, lst = _route(hn_aug, thr_need, n_p, cap_p, n_experts)
        nblk = n // ROUTE_TB
        off = off.reshape(nblk, LANES)
        cnt = cnt.reshape(nblk, LANES)
        xe_p = _gather(off, cnt, lst, hn_aug, 0, n_p, 0, cap_p)
        xe_s = _gather(off, cnt, lst, hn_aug, n_p, n_s, cap_p, cap_s)
        ye = _ffn(l, xe_p, xe_s, w_gate, w_up, w_down, 512)
        x = _combine(l, off, cnt, ye, slot, x1, mod, final_g, row_cb, n_experts, l == depth - 1)
        sd_out.append(sd_p)
        sl_out.append(sl_p)

    y = x
    y_prompt = y[:n_p].reshape(bp, tp, d)
    y_sample = y[n_p:].reshape(bs, ts, d)
    return (y_prompt, y_sample, jnp.stack(sd_out, axis=1), jnp.stack(sl_out, axis=1))
```

```python
import functools
import math

import jax
import jax.numpy as jnp
from jax import lax
from jax.experimental import pallas as pl
from jax.experimental.pallas import tpu as pltpu

F32 = jnp.float32
BF16 = jnp.bfloat16

EPS = 1e-6
CHUNK = 64
SUB = 8
GDN_BATCH = 32
CONV_LEFT = 2
CONV_W = 4
LRU_C = 8.0
LRU_GROUP = 4
N_DIR = 2
GRID_W = 64
LANES = 128
SUBLANES = 8
VMEM_LIMIT = 56 * 1024 * 1024


def _cparams(sem):
    return pltpu.CompilerParams(dimension_semantics=sem, vmem_limit_bytes=VMEM_LIMIT)


def _bdot(a, b):
    return jnp.dot(a.astype(BF16), b.astype(BF16), preferred_element_type=F32)


def _mod_kernel(c_ref, w_ref, b_ref, o_ref):
    c = c_ref[...]
    c = c * jax.nn.sigmoid(c)
    o_ref[...] = _bdot(c, w_ref[...]) + b_ref[...]


def _modulation(cond8, w_mod, b_mod):
    depth, d, n6 = w_mod.shape
    tn = 1536
    return pl.pallas_call(
        _mod_kernel,
        grid=(depth, n6 // tn),
        in_specs=[
            pl.BlockSpec((8, d), lambda l, j: (0, 0)),
            pl.BlockSpec((None, d, tn), lambda l, j: (l, 0, j)),
            pl.BlockSpec((None, 1, tn), lambda l, j: (l, 0, j)),
        ],
        out_specs=pl.BlockSpec((None, 8, tn), lambda l, j: (l, 0, j)),
        out_shape=jax.ShapeDtypeStruct((depth, 8, n6), F32),
        compiler_params=_cparams(("parallel", "parallel")),
        name="modulation",
    )(cond8, w_mod, b_mod.reshape(depth, 1, n6))


def _w_in_prep_kernel(w_ref, main_ref, ba_ref, *, c0, n_small):
    w = w_ref[...]
    rows = w.shape[0]
    main_ref[...] = jnp.concatenate([w[:, :c0], w[:, c0 + n_small:]], axis=1).astype(BF16)
    ba_ref[...] = jnp.concatenate(
        [w[:, c0:c0 + n_small], jnp.zeros((rows, LANES - n_small), F32)], axis=1).astype(BF16)


def _w_in_prep(w_in, c0, n_small):
    depth, d, ncol = w_in.shape
    tk = 256
    return pl.pallas_call(
        functools.partial(_w_in_prep_kernel, c0=c0, n_small=n_small),
        grid=(depth, d // tk),
        in_specs=[pl.BlockSpec((None, tk, ncol), lambda l, i: (l, i, 0))],
        out_specs=[
            pl.BlockSpec((None, tk, ncol - n_small), lambda l, i: (l, i, 0)),
            pl.BlockSpec((None, tk, LANES), lambda l, i: (l, i, 0)),
        ],
        out_shape=[
            jax.ShapeDtypeStruct((depth, d, ncol - n_small), BF16),
            jax.ShapeDtypeStruct((depth, d, LANES), BF16),
        ],
        compiler_params=_cparams(("parallel", "parallel")),
        name="w_in_prep",
    )(w_in)


IN_PROJ_TN = 1536


def _in_proj_kernel(x_ref, g_ref, sh_ref, sc_ref, w_ref, wba_ref, o_ref, oba_ref):
    x = x_ref[...]
    y = x * lax.rsqrt(jnp.mean(x * x, axis=-1, keepdims=True) + EPS) * g_ref[...]
    hn = (y * (1.0 + sc_ref[...]) + sh_ref[...]).astype(BF16)
    oba_ref[...] = jnp.dot(hn, wba_ref[...], preferred_element_type=F32)
    for c in range(0, o_ref.shape[1], IN_PROJ_TN):
        o_ref[:, c:c + IN_PROJ_TN] = jnp.dot(hn, w_ref[:, c:c + IN_PROJ_TN], preferred_element_type=F32)


def _in_proj(l, x, norm_g, mod5, w_main, w_ba, row_of_tile, tm):
    n, d = x.shape
    ncols = w_main.shape[2]
    assert ncols % IN_PROJ_TN == 0
    resident = pl.Buffered(1)
    return pl.pallas_call(
        _in_proj_kernel,
        grid=(n // tm,),
        in_specs=[
            pl.BlockSpec((tm, d), lambda i: (i, 0)),
            pl.BlockSpec((None, 1, d), lambda i: (l, 0, 0)),
            pl.BlockSpec((None, None, None, 1, d), lambda i: (l, row_of_tile(i), 0, 0, 0)),
            pl.BlockSpec((None, None, None, 1, d), lambda i: (l, row_of_tile(i), 1, 0, 0)),
            pl.BlockSpec((None, d, ncols), lambda i: (l, 0, 0), pipeline_mode=resident),
            pl.BlockSpec((None, d, LANES), lambda i: (l, 0, 0), pipeline_mode=resident),
        ],
        out_specs=[
            pl.BlockSpec((tm, ncols), lambda i: (i, 0)),
            pl.BlockSpec((tm, LANES), lambda i: (i, 0)),
        ],
        out_shape=[
            jax.ShapeDtypeStruct((n, ncols), F32),
            jax.ShapeDtypeStruct((n, LANES), F32),
        ],
        compiler_params=_cparams(("parallel",)),
        name="in_proj",
    )(x, norm_g, mod5, mod5, w_main, w_ba)


def _conv_rows(x, w, row):
    t = x.shape[0]
    acc = x * w[CONV_LEFT:CONV_LEFT + 1, :]
    for j in range(CONV_W):
        off = j - CONV_LEFT
        if off == 0:
            continue
        xs = pltpu.roll(x, (-off) % t, axis=0)
        valid = (row + off >= 0) & (row + off < t)
        acc = acc + jnp.where(valid, xs, 0.0) * w[j:j + 1, :]
    return acc


def _softplus(x):
    return jnp.maximum(x, 0.0) + jnp.log1p(jnp.exp(-jnp.abs(x)))


def _bmm(a, b):
    return jnp.einsum('nij,njk->nik', a.astype(BF16), b.astype(BF16), preferred_element_type=F32)


def _bmm_nt(a, b, precision=None):
    return jnp.einsum('nid,njd->nij', a, b, preferred_element_type=F32, precision=precision)


def _unit_tri_inverse(a, ii, jj):
    def same(b):
        return (ii // b) == (jj // b)

    eye = (ii == jj).astype(F32)
    d1 = jnp.where(same(SUB), a, 0.0)
    d2 = _bmm(d1, d1)
    d4 = _bmm(d2, d2)
    x = eye - d1
    x = x + _bmm(x, d2)
    x = x + _bmm(x, d4)
    b = SUB
    while b < CHUNK:
        o = jnp.where(same(2 * b) & jnp.logical_not(same(b)), a, 0.0)
        x = x - _bmm(_bmm(x, o), x)
        b *= 2
    return x


def _gdn_seq_kernel(*refs, t, has_s0, has_acc, heads, dk, hg):
    n_st = N_DIR * heads
    n_in = 12 + int(has_s0) + int(has_acc)
    ins, outs = refs[:n_in], refs[n_in:]
    (q_ref, k_ref, v_ref, z_ref, ba_ref, cq_ref, ck_ref, cv_ref, al_ref, dt_ref, ng_ref, oh_ref) = ins[:12]
    s0_ref = ins[12] if has_s0 else None
    o_ref, s_ref, x_scr, at_scr, kq_scr, kd_scr, vb_scr, cd_scr, o_scr = outs[:9]
    st_scrs = outs[9:9 + n_st]
    n = t // CHUNK
    row = lax.broadcasted_iota(jnp.int32, (t, LANES), 0)
    pos = row % CHUNK
    ii = lax.broadcasted_iota(jnp.int32, (CHUNK, CHUNK), 0)
    jj = lax.broadcasted_iota(jnp.int32, (CHUNK, CHUNK), 1)
    ba = ba_ref[...]
    al = al_ref[...]
    dtb = dt_ref[...]

    def conv_silu(x_ref, w_ref, h):
        y = _conv_rows(x_ref[:, h * LANES:(h + 1) * LANES], w_ref[:, h * LANES:(h + 1) * LANES], row)
        return y * jax.nn.sigmoid(y)

    def l2n(x):
        return x * lax.rsqrt(jnp.sum(x * x, axis=-1, keepdims=True) + EPS)

    nb = hg * n
    ii3 = lax.broadcasted_iota(jnp.int32, (N_DIR * nb, CHUNK, CHUNK), 1)
    jj3 = lax.broadcasted_iota(jnp.int32, (N_DIR * nb, CHUNK, CHUNK), 2)
    fwd = lax.broadcasted_iota(jnp.int32, (N_DIR * nb, CHUNK, CHUNK), 0) < nb
    ahead = jnp.where(fwd, ii3 - jj3, jj3 - ii3)
    tri = ahead >= 0
    strict = ahead > 0
    onehot0 = jnp.broadcast_to(oh_ref[...][None], (N_DIR * nb, CHUNK, LANES))

    for h0 in range(0, heads, hg):
        hs = range(h0, h0 + hg)
        as3 = lambda x: x.reshape(n, CHUNK, LANES)
        q3 = jnp.concatenate([as3(l2n(conv_silu(q_ref, cq_ref, h)) * (dk ** -0.5)) for h in hs], axis=0)
        k3 = jnp.concatenate([as3(l2n(conv_silu(k_ref, ck_ref, h))) for h in hs], axis=0)
        v3 = jnp.concatenate([as3(conv_silu(v_ref, cv_ref, h)) for h in hs], axis=0)
        k3b = k3.astype(BF16)
        gram = _bmm_nt(k3b, k3b)
        qk = _bmm_nt(q3.astype(BF16), k3b)
        decs, tots, betas = [], [], []
        for d in range(N_DIR):
            for h in hs:
                lb = d * heads + h
                la = 2 * heads + lb
                beta = jax.nn.sigmoid(ba[:, lb:lb + 1])
                g = -jnp.exp(al[:, lb:lb + 1]) * _softplus(ba[:, la:la + 1] + dtb[:, lb:lb + 1])
                dec = jnp.broadcast_to(g, (t, LANES))
                step = 1
                while step < CHUNK:
                    if d == 0:
                        sh = pltpu.roll(dec, step, axis=0)
                        dec = dec + jnp.where(pos >= step, sh, 0.0)
                    else:
                        sh = pltpu.roll(dec, t - step, axis=0)
                        dec = dec + jnp.where(pos < CHUNK - step, sh, 0.0)
                    step *= 2
                dec3 = as3(dec)
                decs.append(dec3)
                tots.append(jnp.broadcast_to(dec3[:, CHUNK - 1:CHUNK, :] if d == 0 else dec3[:, 0:1, :],
                                             (n, CHUNK, LANES)))
                betas.append(as3(jnp.broadcast_to(beta, (t, LANES))))
        dec3 = jnp.concatenate(decs, axis=0)
        tot3 = jnp.concatenate(tots, axis=0)
        beta3 = jnp.concatenate(betas, axis=0)
        both = lambda x: jnp.concatenate([x, x], axis=0)
        dec_row = _bmm_nt(onehot0, dec3, precision=lax.Precision.HIGHEST)
        dec_col = dec3[:, :, :CHUNK]
        gamma = jnp.where(tri, jnp.exp(jnp.where(tri, dec_col - dec_row, 0.0)), 0.0)
        a = jnp.where(strict, both(gram) * gamma, 0.0) * beta3[:, :, :CHUNK]
        xinv = _unit_tri_inverse(a, ii, jj).astype(BF16)
        attn = jnp.where(tri, both(qk) * gamma, 0.0).astype(BF16)
        edec = jnp.exp(dec3)
        k32, q32, v32 = both(k3), both(q3), both(v3)
        kbd = (k32 * (beta3 * edec)).astype(BF16)
        qd = (q32 * edec).astype(BF16)
        kd = jnp.swapaxes(k32 * jnp.exp(tot3 - dec3), 1, 2).astype(BF16)
        vb = v32 * beta3
        cd = jnp.exp(tot3[:, :SUBLANES, :])
        for d in range(N_DIR):
            src = slice(d * nb, (d + 1) * nb)
            dst = pl.ds((d * heads + h0) * n, nb)
            x_scr[dst] = xinv[src]
            at_scr[dst] = attn[src]
            kq_scr[dst, :CHUNK, :] = kbd[src]
            kq_scr[dst, CHUNK:, :] = qd[src]
            kd_scr[dst] = kd[src]
            vb_scr[dst] = vb[src]
            cd_scr[dst] = cd[src]

    for d in range(N_DIR):
        for h in range(heads):
            if has_s0:
                st_scrs[d * heads + h][...] = s0_ref[d, h]
            else:
                st_scrs[d * heads + h][...] = jnp.zeros((dk, LANES), F32)

    def scan(it, carry):
        chains = [(d * heads + h, (d * heads + h) * n + (it if d == 0 else n - 1 - it))
                  for h in range(heads) for d in range(N_DIR)]
        ss = [st_scrs[i][...] for i, _ in chains]
        rqs = [jnp.dot(kq_scr[b], s.astype(BF16), preferred_element_type=F32)
               for (_, b), s in zip(chains, ss)]
        vns = [jnp.dot(x_scr[b], (vb_scr[b] - rq[:CHUNK]).astype(BF16), preferred_element_type=F32).astype(BF16)
               for (_, b), rq in zip(chains, rqs)]
        for (i, b), s, rq, vn in zip(chains, ss, rqs, vns):
            o_scr[b] = rq[CHUNK:] + jnp.dot(at_scr[b], vn, preferred_element_type=F32)
            st_scrs[i][...] = s * cd_scr[b][0:1, :] + jnp.dot(kd_scr[b], vn, preferred_element_type=F32)
        return carry

    lax.fori_loop(0, n, scan, 0)

    for h in range(heads):
        for d in range(N_DIR):
            s_ref[d, h] = st_scrs[d * heads + h][...]
        of = o_scr[pl.ds(h * n, n)]
        ob = o_scr[pl.ds((heads + h) * n, n)]
        o = (of + ob).reshape(t, LANES)
        o = o * lax.rsqrt(jnp.mean(o * o, axis=-1, keepdims=True) + EPS) * ng_ref[...]
        z = z_ref[:, h * LANES:(h + 1) * LANES]
        o_ref[:, h * LANES:(h + 1) * LANES] = (o * (z * jax.nn.sigmoid(z))).astype(o_ref.dtype)


def _gdn_seq(l, cols, ba, conv_qkv, al, dtb, norm_g, s0, row0, nb, t, heads, dk, s_acc=None):
    blk0 = row0 // t
    has_s0 = s0 is not None
    has_acc = s_acc is not None
    hw = heads * LANES
    n = t // CHUNK
    col = lambda off: pl.BlockSpec((t, hw), lambda b: (blk0 + b, off))
    cw = lambda off: pl.BlockSpec((None, CONV_W, hw), lambda b: (l, 0, off))
    vec = pl.BlockSpec((None, 1, LANES), lambda b: (l, 0, 0))
    in_specs = [
        col(0), col(1), col(2), col(3),
        pl.BlockSpec((t, LANES), lambda b: (blk0 + b, 0)),
        cw(0), cw(1), cw(2),
        vec, vec, vec,
        pl.BlockSpec((CHUNK, LANES), lambda b: (0, 0)),
    ]
    onehot0 = jnp.zeros((CHUNK, LANES), F32).at[:, 0].set(1.0)
    args = [cols, cols, cols, cols, ba, conv_qkv, conv_qkv, conv_qkv, al, dtb, norm_g, onehot0]
    if has_s0:
        in_specs.append(pl.BlockSpec((None, None, N_DIR, heads, dk, LANES), lambda b: (b, l, 0, 0, 0, 0)))
        args.append(s0)
    if has_acc:
        in_specs.append(pl.BlockSpec(memory_space=pl.ANY))
        args.append(s_acc)
        state_spec = pl.BlockSpec((None, None, N_DIR, heads, dk, LANES), lambda b: (b, l, 0, 0, 0, 0))
        state_shape = jax.ShapeDtypeStruct(s_acc.shape, F32)
        aliases = {len(args) - 1: 1}
    else:
        state_spec = pl.BlockSpec((None, N_DIR, heads, dk, LANES), lambda b: (b, 0, 0, 0, 0))
        state_shape = jax.ShapeDtypeStruct((nb, N_DIR, heads, dk, LANES), F32)
        aliases = {}
    per = (N_DIR * heads * n,)
    hg = max(1, min(heads, GDN_BATCH // (N_DIR * n)))
    assert heads % hg == 0
    return pl.pallas_call(
        functools.partial(_gdn_seq_kernel, t=t, has_s0=has_s0, has_acc=has_acc, heads=heads, dk=dk, hg=hg),
        grid=(nb,),
        in_specs=in_specs,
        out_specs=[pl.BlockSpec((t, hw), lambda b: (b, 0)), state_spec],
        out_shape=[jax.ShapeDtypeStruct((nb * t, hw), BF16), state_shape],
        input_output_aliases=aliases,
        scratch_shapes=[
            pltpu.VMEM(per + (CHUNK, CHUNK), BF16),
            pltpu.VMEM(per + (CHUNK, CHUNK), BF16),
            pltpu.VMEM(per + (2 * CHUNK, LANES), BF16),
            pltpu.VMEM(per + (dk, CHUNK), BF16),
            pltpu.VMEM(per + (CHUNK, LANES), F32),
            pltpu.VMEM(per + (SUBLANES, LANES), F32),
            pltpu.VMEM(per + (CHUNK, LANES), F32),
        ] + [pltpu.VMEM((dk, LANES), F32)] * (N_DIR * heads),
        compiler_params=_cparams(("parallel",)),
        name=f"gdn_t{t}",
    )(*args)


def _lru_kernel(*refs, t, has_h0):
    if has_h0:
        (lx_ref, ly_ref, cw_ref, cb_ref, wa_ref, wx_ref, ba_ref, bx_ref, lam_ref, h0_ref,
         y_ref, last_ref, a_scr, b_scr, h_scr) = refs
    else:
        (lx_ref, ly_ref, cw_ref, cb_ref, wa_ref, wx_ref, ba_ref, bx_ref, lam_ref,
         y_ref, last_ref, a_scr, b_scr, h_scr) = refs
        h0_ref = None
    nblk = t // SUBLANES
    wl = lx_ref.shape[1]
    row = lax.broadcasted_iota(jnp.int32, (t, wl), 0)
    sub = lax.broadcasted_iota(jnp.int32, (nblk, SUBLANES, wl), 1)
    u = _conv_rows(lx_ref[...], cw_ref[...], row) + cb_ref[...]
    ub = u.astype(BF16)

    def block_diag(w_ref, d):
        return jnp.concatenate(
            [jnp.dot(ub[:, c * LANES:(c + 1) * LANES], w_ref[d, c], preferred_element_type=F32)
             for c in range(wl // LANES)], axis=1)

    for d in range(N_DIR):
        r = jax.nn.sigmoid(block_diag(wa_ref, d) + ba_ref[d:d + 1, :])
        i = jax.nn.sigmoid(block_diag(wx_ref, d) + bx_ref[d:d + 1, :])
        log_a = (LRU_C * r) * (-_softplus(-lam_ref[d:d + 1, :]))
        a = jnp.exp(log_a)
        b = jnp.sqrt(1.0 - a * a) * (i * u)
        a3 = a.reshape(nblk, SUBLANES, wl)
        b3 = b.reshape(nblk, SUBLANES, wl)
        step = 1
        while step < SUBLANES:
            if d == 0:
                a_s = pltpu.roll(a3, step, axis=1)
                b_s = pltpu.roll(b3, step, axis=1)
                m = sub >= step
            else:
                a_s = pltpu.roll(a3, SUBLANES - step, axis=1)
                b_s = pltpu.roll(b3, SUBLANES - step, axis=1)
                m = sub < SUBLANES - step
            b3 = b3 + a3 * jnp.where(m, b_s, 0.0)
            a3 = a3 * jnp.where(m, a_s, 1.0)
            step *= 2
        a_scr[d] = a3
        b_scr[d] = b3

    if has_h0:
        h_init = (jnp.broadcast_to(h0_ref[0:1, :], (SUBLANES, wl)),
                  jnp.broadcast_to(h0_ref[1:2, :], (SUBLANES, wl)))
    else:
        h_init = (jnp.zeros((SUBLANES, wl), F32), jnp.zeros((SUBLANES, wl), F32))

    def body(it, carry):
        hf, hb = carry
        kf = it
        kb = nblk - 1 - it
        new_f = b_scr[0, kf] + a_scr[0, kf] * hf
        new_b = b_scr[1, kb] + a_scr[1, kb] * hb
        h_scr[0, kf] = new_f
        h_scr[1, kb] = new_b
        hf = jnp.broadcast_to(new_f[SUBLANES - 1:SUBLANES, :], (SUBLANES, wl))
        hb = jnp.broadcast_to(new_b[0:1, :], (SUBLANES, wl))
        return hf, hb

    hf, hb = lax.fori_loop(0, nblk, body, h_init, unroll=4)
    last_ref[0:1, :] = hf[0:1, :]
    last_ref[1:2, :] = hb[0:1, :]
    rec = (h_scr[0] + h_scr[1]).reshape(t, wl)
    y_ref[...] = (jax.nn.gelu(ly_ref[...]) * rec).astype(y_ref.dtype)


def _lru(l, cols, conv_w, conv_b, wa, wx, ba, bx, lam, h0, row0, nb, t, lx_blk, ly_blk):
    nblocks = wa.shape[2]
    w = nblocks * LANES
    cg = LRU_GROUP
    wl = cg * LANES
    blk0 = row0 // t
    has_h0 = h0 is not None
    in_specs = [
        pl.BlockSpec((t, wl), lambda b, c: (blk0 + b, lx_blk // cg + c)),
        pl.BlockSpec((t, wl), lambda b, c: (blk0 + b, ly_blk // cg + c)),
        pl.BlockSpec((None, CONV_W, wl), lambda b, c: (l, 0, c)),
        pl.BlockSpec((None, 1, wl), lambda b, c: (l, 0, c)),
        pl.BlockSpec((None, N_DIR, cg, LANES, LANES), lambda b, c: (l, 0, c, 0, 0)),
        pl.BlockSpec((None, N_DIR, cg, LANES, LANES), lambda b, c: (l, 0, c, 0, 0)),
        pl.BlockSpec((None, N_DIR, wl), lambda b, c: (l, 0, c)),
        pl.BlockSpec((None, N_DIR, wl), lambda b, c: (l, 0, c)),
        pl.BlockSpec((None, N_DIR, wl), lambda b, c: (l, 0, c)),
    ]
    args = [cols, cols, conv_w, conv_b, wa, wx, ba, bx, lam]
    if has_h0:
        in_specs.append(pl.BlockSpec((None, None, N_DIR, wl), lambda b, c: (b, l, 0, c)))
        args.append(h0)
    nblk = t // SUBLANES
    assert lx_blk % cg == 0 and ly_blk % cg == 0 and nblocks % cg == 0
    return pl.pallas_call(
        functools.partial(_lru_kernel, t=t, has_h0=has_h0),
        grid=(nb, nblocks // cg),
        in_specs=in_specs,
        out_specs=[
            pl.BlockSpec((t, wl), lambda b, c: (b, c)),
            pl.BlockSpec((None, N_DIR, wl), lambda b, c: (b, 0, c)),
        ],
        out_shape=[
            jax.ShapeDtypeStruct((nb * t, w), BF16),
            jax.ShapeDtypeStruct((nb, N_DIR, w), F32),
        ],
        scratch_shapes=[pltpu.VMEM((N_DIR, nblk, SUBLANES, wl), F32)] * 3,
        compiler_params=_cparams(("parallel", "parallel")),
        name=f"lru_t{t}",
    )(*args)


def _merge_kernel(oap_ref, oas_ref, obp_ref, obs_ref, ga_ref, gb_ref, x_ref, g1_ref, sh2_ref, sc2_ref,
                  n2_ref, wdn_ref, wlru_ref, wo_ref, wr_ref, x1_ref, hn_ref, afft_ref, *, n_experts, tiles_p):
    d = x_ref.shape[1]
    is_p = pl.program_id(0) < tiles_p
    oa = jnp.where(is_p, oap_ref[...], oas_ref[...])
    ob = jnp.where(is_p, obp_ref[...], obs_ref[...])
    ya = jnp.dot(oa, wdn_ref[...], preferred_element_type=F32)
    yb = jnp.dot(ob, wlru_ref[...], preferred_element_type=F32)
    mix = jax.nn.sigmoid(ga_ref[...]) * ya + jax.nn.sigmoid(gb_ref[...]) * yb
    mix = jnp.dot(mix.astype(BF16), wo_ref[...], preferred_element_type=F32)
    x1 = x_ref[...] + g1_ref[...] * mix
    x1_ref[...] = x1
    y = x1 * lax.rsqrt(jnp.mean(x1 * x1, axis=-1, keepdims=True) + EPS) * n2_ref[...]
    hn = y * (1.0 + sc2_ref[...]) + sh2_ref[...]
    logits = jnp.dot(hn.astype(BF16), wr_ref[...], preferred_element_type=F32)
    lane = lax.broadcasted_iota(jnp.int32, logits.shape, 1)
    logits = jnp.where(lane < n_experts, logits, -jnp.inf)
    e = jnp.exp(logits - jnp.max(logits, axis=-1, keepdims=True))
    aff = e / jnp.sum(e, axis=-1, keepdims=True)
    hn_ref[:, :d] = hn
    hn_ref[:, d:] = aff
    afft_ref[...] = aff.T[:n_experts, :]


def _merge(l, o_p, o_s, y_p, y_s, cols, x, mod5, norm2_g, w_dn, w_lru, w_o, w_r, row_of_tile, tm,
           ga_blk, gb_blk, n_experts):
    n, d = x.shape
    dv = o_p.shape[1]
    w = y_p.shape[1]
    tiles_p = o_p.shape[0] // tm
    modspec = lambda k: pl.BlockSpec((None, None, None, 1, d), lambda i: (l, row_of_tile(i), k, 0, 0))
    layer = lambda a: pl.BlockSpec((None,) + a.shape[1:], lambda i: (l,) + (0,) * (a.ndim - 1))
    p_tile = lambda width: pl.BlockSpec((tm, width), lambda i: (jnp.minimum(i, tiles_p - 1), 0))
    s_tile = lambda width: pl.BlockSpec((tm, width), lambda i: (jnp.maximum(i - tiles_p, 0), 0))
    return pl.pallas_call(
        functools.partial(_merge_kernel, n_experts=n_experts, tiles_p=tiles_p),
        grid=(n // tm,),
        in_specs=[
            p_tile(dv), s_tile(dv), p_tile(w), s_tile(w),
            pl.BlockSpec((tm, d), lambda i: (i, ga_blk)),
            pl.BlockSpec((tm, d), lambda i: (i, gb_blk)),
            pl.BlockSpec((tm, d), lambda i: (i, 0)),
            modspec(2), modspec(3), modspec(4),
            layer(norm2_g), layer(w_dn), layer(w_lru), layer(w_o), layer(w_r),
        ],
        out_specs=[
            pl.BlockSpec((tm, d), lambda i: (i, 0)),
            pl.BlockSpec((tm, d + LANES), lambda i: (i, 0)),
            pl.BlockSpec((n_experts, tm), lambda i: (0, i)),
        ],
        out_shape=[
            jax.ShapeDtypeStruct((n, d), F32),
            jax.ShapeDtypeStruct((n, d + LANES), F32),
            jax.ShapeDtypeStruct((n_experts, n), F32),
        ],
        compiler_params=_cparams(("parallel",)),
        name="merge",
    )(o_p, o_s, y_p, y_s, cols, cols, x, mod5, mod5, mod5, norm2_g, w_dn, w_lru, w_o, w_r)


def _ffn_kernel(xp_ref, xs_ref, wg_ref, wu_ref, wd_ref, o_ref, xb_scr):
    f = pl.program_id(1)
    cap_p = xp_ref.shape[0]
    d = xb_scr.shape[1]

    @pl.when(f == 0)
    def _():
        xb_scr[:cap_p, :] = xp_ref[:, :d].astype(BF16)
        xb_scr[cap_p:, :] = xs_ref[:, :d].astype(BF16)
        o_ref[...] = jnp.zeros_like(o_ref)

    x = xb_scr[...]
    g = jnp.dot(x, wg_ref[...].astype(BF16), preferred_element_type=F32)
    u = jnp.dot(x, wu_ref[...].astype(BF16), preferred_element_type=F32)
    hid = ((g * jax.nn.sigmoid(g)) * u).astype(BF16)
    o_ref[...] += jnp.dot(hid, wd_ref[...].astype(BF16), preferred_element_type=F32)

    @pl.when(f == pl.num_programs(1) - 1)
    def _():
        e = pl.program_id(0)
        for ref, r0 in ((xp_ref, 0), (xs_ref, cap_p)):
            aff = ref[:, d:]
            lane = lax.broadcasted_iota(jnp.int32, aff.shape, 1)
            gv = jnp.sum(jnp.where(lane == e, aff, 0.0), axis=1, keepdims=True)
            rows = ref.shape[0]
            o_ref[r0:r0 + rows, :] = o_ref[r0:r0 + rows, :] * gv


def _ffn(l, xe_p, xe_s, w_gate, w_up, w_down, tf):
    e, cap_p, da = xe_p.shape
    d = da - LANES
    cap_s = xe_s.shape[1]
    r = cap_p + cap_s
    ff = w_gate.shape[3]
    return pl.pallas_call(
        _ffn_kernel,
        grid=(e, ff // tf),
        in_specs=[
            pl.BlockSpec((None, cap_p, da), lambda i, f: (i, 0, 0)),
            pl.BlockSpec((None, cap_s, da), lambda i, f: (i, 0, 0)),
            pl.BlockSpec((None, None, d, tf), lambda i, f: (l, i, 0, f)),
            pl.BlockSpec((None, None, d, tf), lambda i, f: (l, i, 0, f)),
            pl.BlockSpec((None, None, tf, d), lambda i, f: (l, i, f, 0)),
        ],
        out_specs=pl.BlockSpec((None, r, d), lambda i, f: (i, 0, 0)),
        out_shape=jax.ShapeDtypeStruct((e, r, d), F32),
        scratch_shapes=[pltpu.VMEM((r, d), BF16)],
        compiler_params=_cparams(("parallel", "arbitrary")),
        name="expert_ffn",
    )(xe_p, xe_s, w_gate, w_up, w_down)


ROUTE_TB = 128


def _thr_kernel(afft_ref, o_ref, *, groups):
    n_e = afft_ref.shape[0]
    sub = lax.broadcasted_iota(jnp.int32, (n_e, LANES), 0)
    lane = lax.broadcasted_iota(jnp.int32, (n_e, LANES), 1)
    rows = []
    for lo, hi, cap in groups:
        a = afft_ref[:, lo:hi]
        above_all = 4.0

        def count_ge(v, a=a):
            return jnp.sum((a >= v).astype(F32), axis=1, keepdims=True)

        def count_gt(v, a=a):
            return jnp.sum((a > v).astype(F32), axis=1, keepdims=True)

        def bisect(i, lh, cap=cap, count_ge=count_ge):
            lo_v, hi_v = lh
            mid = 0.5 * (lo_v + hi_v)
            ok = count_ge(mid) >= cap
            return jnp.where(ok, mid, lo_v), jnp.where(ok, hi_v, mid)

        lo_v, _ = lax.fori_loop(0, 48, bisect, (jnp.zeros((n_e, 1), F32), jnp.full((n_e, 1), 2.0, F32)))
        thr = jnp.min(jnp.where(a >= lo_v, a, above_all), axis=1, keepdims=True)

        def not_done(thr, cap=cap, count_gt=count_gt):
            return jnp.max(count_gt(thr)) >= cap

        def step_up(thr, a=a, cap=cap, count_gt=count_gt):
            nxt = jnp.min(jnp.where(a > thr, a, above_all), axis=1, keepdims=True)
            return jnp.where(count_gt(thr) >= cap, nxt, thr)

        thr = lax.while_loop(not_done, step_up, thr)
        need = cap - count_gt(thr)
        for col in (thr, need):
            m = jnp.where(sub == lane, jnp.broadcast_to(col, (n_e, LANES)), 0.0)
            rows.append(jnp.sum(m, axis=0, keepdims=True))
    rows.append(jnp.zeros((SUBLANES - len(rows), LANES), F32))
    o_ref[...] = jnp.concatenate(rows, axis=0)


def _thresholds(afft, groups):
    n_e, n = afft.shape
    return pl.pallas_call(
        functools.partial(_thr_kernel, groups=groups),
        grid=(1,),
        in_specs=[pl.BlockSpec((n_e, n), lambda i: (0, 0))],
        out_specs=pl.BlockSpec((SUBLANES, LANES), lambda i: (0, 0)),
        out_shape=jax.ShapeDtypeStruct((SUBLANES, LANES), F32),
        compiler_params=_cparams(("arbitrary",)),
        name="route_thresholds",
    )(afft)


def _route_kernel(aff_ref, tn_ref, ls_ref, slot_ref, off_ref, cnt_ref, lst_ref,
                  run_eq, run_sel, *, nblk_p, cap_p, n_e):
    tb = ROUTE_TB
    j = pl.program_id(0)

    @pl.when(j == 0)
    def _():
        run_eq[...] = jnp.zeros_like(run_eq)
        run_sel[...] = jnp.zeros_like(run_sel)

    @pl.when(j == nblk_p)
    def _():
        run_eq[...] = jnp.zeros_like(run_eq)
        run_sel[...] = jnp.full_like(run_sel, float(cap_p))

    g = (j >= nblk_p).astype(jnp.int32)
    thr = tn_ref[pl.ds(2 * g, 1), :]
    need = tn_ref[pl.ds(2 * g + 1, 1), :]
    a = aff_ref[...]
    lane = lax.broadcasted_iota(jnp.int32, (tb, LANES), 1)
    valid = lane < n_e
    gt = (a > thr) & valid
    eq = (a == thr) & valid
    eqf = eq.astype(F32)
    ls = ls_ref[...]
    eq_rank = run_eq[...] + jnp.dot(ls, eqf.astype(BF16), preferred_element_type=F32)
    sel = gt | (eq & (eq_rank < need))
    self_ = sel.astype(F32)
    pos = jnp.dot(ls, self_.astype(BF16), preferred_element_type=F32)
    cnt = jnp.sum(self_, axis=0, keepdims=True)
    off = run_sel[...]
    run_eq[...] = run_eq[...] + jnp.sum(eqf, axis=0, keepdims=True)
    run_sel[...] = off + cnt
    slot_ref[...] = jnp.where(sel, off + pos, -1.0)
    off_ref[...] = off.astype(jnp.int32)
    cnt_ref[...] = cnt.astype(jnp.int32)

    tok = (lax.broadcasted_iota(jnp.int32, (tb, LANES), 0) + j * tb).astype(F32)
    lanef = lane.astype(F32)
    for e in range(n_e):
        pe = jnp.broadcast_to(pos[:, e:e + 1], (tb, LANES))
        se = jnp.broadcast_to(self_[:, e:e + 1], (tb, LANES))
        hit = (pe == lanef) & (se > 0.0)
        lst_ref[e] = jnp.sum(jnp.where(hit, tok, 0.0), axis=0, keepdims=True).astype(jnp.int32)


def _route(hn_aug, thr_need, n_p, cap_p, n_e):
    n = hn_aug.shape[0]
    aff_blk = hn_aug.shape[1] // LANES - 1
    tb = ROUTE_TB
    nblk = n // tb
    ls = jnp.tril(jnp.ones((tb, tb), F32), -1).astype(BF16)
    blk_row = pl.BlockSpec((None, 1, LANES), lambda j: (j, 0, 0))
    return pl.pallas_call(
        functools.partial(_route_kernel, nblk_p=n_p // tb, cap_p=cap_p, n_e=n_e),
        grid=(nblk,),
        in_specs=[
            pl.BlockSpec((tb, LANES), lambda j: (j, aff_blk)),
            pl.BlockSpec((SUBLANES, LANES), lambda j: (0, 0)),
            pl.BlockSpec((tb, tb), lambda j: (0, 0)),
        ],
        out_specs=[
            pl.BlockSpec((tb, LANES), lambda j: (j, 0)),
            blk_row,
            blk_row,
            pl.BlockSpec((n_e, None, 1, LANES), lambda j: (0, j, 0, 0)),
        ],
        out_shape=[
            jax.ShapeDtypeStruct((n, LANES), F32),
            jax.ShapeDtypeStruct((nblk, 1, LANES), jnp.int32),
            jax.ShapeDtypeStruct((nblk, 1, LANES), jnp.int32),
            jax.ShapeDtypeStruct((n_e, nblk, 1, LANES), jnp.int32),
        ],
        scratch_shapes=[pltpu.VMEM((1, LANES), F32), pltpu.VMEM((1, LANES), F32)],
        compiler_params=_cparams(("arbitrary",)),
        name="route_slots",
    )(hn_aug, thr_need, ls)


GATHER_UNROLL = 4


def _gather_kernel(off_ref, cnt_ref, lst_hbm, hn_hbm, o_ref, hn_scr, lst_smem, sem_h, sem_l,
                   *, row0, slot0, blk0, nblk_g):
    e = pl.program_id(0)
    load_lst = pltpu.make_async_copy(lst_hbm.at[e, pl.ds(blk0, nblk_g)], lst_smem, sem_l)
    load_lst.start()

    @pl.when(e == 0)
    def _():
        rows = hn_scr.shape[0]
        load = pltpu.make_async_copy(hn_hbm.at[pl.ds(row0, rows)], hn_scr, sem_h)
        load.start()
        load.wait()

    load_lst.wait()

    def block(jb, carry):
        c = cnt_ref[blk0 + jb, e]
        o = off_ref[blk0 + jb, e] - slot0

        def copy_row(q):
            t = lst_smem[jb, q] - row0
            o_ref[pl.ds(o + q, 1), :] = hn_scr[pl.ds(t, 1), :]

        def rows(qq, carry2):
            for u in range(GATHER_UNROLL):
                copy_row(qq * GATHER_UNROLL + u)
            return carry2

        def row(q, carry2):
            copy_row(q)
            return carry2

        full = c // GATHER_UNROLL
        lax.fori_loop(0, full, rows, 0)
        lax.fori_loop(full * GATHER_UNROLL, c, row, 0)
        return carry

    lax.fori_loop(0, nblk_g, block, 0)


def _gather(off, cnt, lst, hn, row0, rows, slot0, cap):
    n_e = lst.shape[0]
    width = hn.shape[1]
    blk0 = row0 // ROUTE_TB
    nblk_g = rows // ROUTE_TB
    grid_spec = pltpu.PrefetchScalarGridSpec(
        num_scalar_prefetch=2,
        grid=(n_e,),
        in_specs=[pl.BlockSpec(memory_space=pl.ANY), pl.BlockSpec(memory_space=pl.ANY)],
        out_specs=pl.BlockSpec((None, cap, width), lambda e, o_, c_: (e, 0, 0)),
        scratch_shapes=[
            pltpu.VMEM((rows, width), F32),
            pltpu.SMEM((nblk_g, LANES), jnp.int32),
            pltpu.SemaphoreType.DMA(()),
            pltpu.SemaphoreType.DMA(()),
        ],
    )
    return pl.pallas_call(
        functools.partial(_gather_kernel, row0=row0, slot0=slot0, blk0=blk0, nblk_g=nblk_g),
        grid_spec=grid_spec,
        out_shape=jax.ShapeDtypeStruct((n_e, cap, width), F32),
        compiler_params=_cparams(("arbitrary",)),
        name=f"gather_rows{rows}",
    )(off, cnt, lst.reshape(n_e, -1, LANES), hn)


COMB_CH = 32
COMB_GRP = 8
COMB_MAXCH = 80
COMB_DESC = 128


def _combine_kernel(off_ref, cnt_ref, y_hbm, slot_ref, x1_ref, g2_ref, fg_ref, o_ref,
                    buf, acc, desc, sems, *, n_e, final):
    tb = ROUTE_TB
    j = pl.program_id(0)
    nblk = pl.num_programs(0)
    r_total = y_hbm.shape[1]
    par = j % 2

    def issue(jj, p):
        s = jnp.int32(0)
        for e in range(n_e):
            o = off_ref[jj, e]
            c = cnt_ref[jj, e]
            st8 = (o // SUBLANES) * SUBLANES
            nch = jnp.where(c > 0, (o - st8 + c + COMB_CH - 1) // COMB_CH, 0)

            def one(k, s, e=e, st8=st8):
                lo_row = st8 + k * COMB_CH
                base = jnp.minimum(lo_row, r_total - COMB_CH)
                pltpu.make_async_copy(y_hbm.at[e, pl.ds(base, COMB_CH)], buf.at[p, s], sems.at[p]).start()
                desc[p, 0, s] = e
                desc[p, 1, s] = lo_row
                desc[p, 2, s] = base
                return s + 1

            s = lax.fori_loop(0, nch, one, s)
        desc[p, 3, 0] = s

    @pl.when(j == 0)
    def _():
        buf[...] = jnp.zeros_like(buf)

        def clear(i, carry):
            for p in range(2):
                for row in range(4):
                    desc[p, row, i] = 0
            return carry

        lax.fori_loop(0, COMB_DESC, clear, 0)
        issue(j, par)

    @pl.when(j + 1 < nblk)
    def _():
        issue(j + 1, 1 - par)

    n_ch = desc[par, 3, 0]

    def drain(k, carry):
        pltpu.make_async_copy(y_hbm.at[0, pl.ds(0, COMB_CH)], buf.at[par, k], sems.at[par]).wait()
        return carry

    lax.fori_loop(0, n_ch, drain, 0)

    slot = slot_ref[...]
    lane = lax.broadcasted_iota(jnp.int32, (tb, LANES), 1)
    lanef = lane.astype(F32)
    acc[...] = jnp.zeros_like(acc)

    per_lane = LANES // COMB_CH
    n_sub = COMB_GRP // per_lane

    def group(gi, carry):
        ebs = []
        for sub in range(n_sub):
            hit = jnp.zeros((tb, LANES), jnp.bool_)
            for c4 in range(per_lane):
                s = gi * COMB_GRP + sub * per_lane + c4
                e = desc[par, 0, s]
                lo_row = desc[par, 1, s].astype(F32)
                base = desc[par, 2, s].astype(F32)
                col = jnp.sum(jnp.where(lane == e, slot, 0.0), axis=1, keepdims=True)
                colb = jnp.broadcast_to(col, (tb, LANES))
                in_chunk = (lane >= c4 * COMB_CH) & (lane < (c4 + 1) * COMB_CH)
                match = (colb == base + (lanef - float(c4 * COMB_CH))) & (colb >= lo_row)
                hit = hit | (in_chunk & match & (s < n_ch))
            ebs.append(jnp.where(hit, 1.0, 0.0).astype(BF16))
        eb = jnp.concatenate(ebs, axis=1)
        yg = buf[par, pl.ds(gi * COMB_GRP, COMB_GRP)].reshape(COMB_GRP * COMB_CH, -1)
        y1 = yg.astype(BF16)
        r1 = yg - y1.astype(F32)
        y2 = r1.astype(BF16)
        y3 = (r1 - y2.astype(F32)).astype(BF16)
        acc[...] += jnp.dot(jnp.concatenate([eb, eb, eb], axis=1), jnp.concatenate([y1, y2, y3], axis=0),
                            preferred_element_type=F32)
        return carry

    lax.fori_loop(0, (n_ch + COMB_GRP - 1) // COMB_GRP, group, 0)

    x2 = x1_ref[...] + g2_ref[...] * acc[...]
    if final:
        x2 = x2 * lax.rsqrt(jnp.mean(x2 * x2, axis=-1, keepdims=True) + EPS) * fg_ref[...]
    o_ref[...] = x2


def _combine(l, off, cnt, ye, slot, x1, mod5, final_g, row_of_tile, n_e, final):
    n, d = x1.shape
    tb = ROUTE_TB
    assert LANES % COMB_CH == 0 and (COMB_GRP * COMB_CH) % LANES == 0 and COMB_MAXCH % COMB_GRP == 0
    assert COMB_DESC >= COMB_MAXCH >= n_e * -(-(tb + SUBLANES - 1) // COMB_CH)
    grid_spec = pltpu.PrefetchScalarGridSpec(
        num_scalar_prefetch=2,
        grid=(n // tb,),
        in_specs=[
            pl.BlockSpec(memory_space=pl.ANY),
            pl.BlockSpec((tb, LANES), lambda j, o_, c_: (j, 0)),
            pl.BlockSpec((tb, d), lambda j, o_, c_: (j, 0)),
            pl.BlockSpec((None, None, None, 1, d), lambda j, o_, c_: (l, row_of_tile(j), 5, 0, 0)),
            pl.BlockSpec((1, d), lambda j, o_, c_: (0, 0)),
        ],
        out_specs=pl.BlockSpec((tb, d), lambda j, o_, c_: (j, 0)),
        scratch_shapes=[
            pltpu.VMEM((2, COMB_MAXCH, COMB_CH, d), F32),
            pltpu.VMEM((tb, d), F32),
            pltpu.SMEM((2, 4, COMB_DESC), jnp.int32),
            pltpu.SemaphoreType.DMA((2,)),
        ],
    )
    return pl.pallas_call(
        functools.partial(_combine_kernel, n_e=n_e, final=final),
        grid_spec=grid_spec,
        out_shape=jax.ShapeDtypeStruct((n, d), F32),
        compiler_params=_cparams(("arbitrary",)),
        name="combine",
    )(off, cnt, ye, slot, x1, mod5, final_g.reshape(1, d))


def _pos_embed_2d(n_tokens, d_model):
    rows = n_tokens // GRID_W
    r = jnp.broadcast_to(jnp.arange(rows, dtype=F32)[:, None], (rows, GRID_W)).reshape(-1)
    col = jnp.broadcast_to(jnp.arange(GRID_W, dtype=F32)[None, :], (rows, GRID_W)).reshape(-1)
    quarter = d_model // 4
    freq = jnp.exp(-math.log(10000.0) * jnp.arange(quarter, dtype=F32) / quarter)
    ar = r[:, None] * freq
    ac = col[:, None] * freq
    return jnp.concatenate([jnp.sin(ar), jnp.cos(ar), jnp.sin(ac), jnp.cos(ac)], axis=-1)


def kernel(x_prompt, x_sample, state_delta, state_lru, c, c_ctx, norm1_g, w_mod, b_mod, w_in, conv_qkv, dn_a_log, dn_dt_bias, dn_norm_g, w_dn_out, conv_lru_w, conv_lru_b, lru_wa, lru_ba, lru_wx, lru_bx, lru_lambda, w_lru_out, w_o, norm2_g, w_router, w_gate, w_up, w_down, final_g):
    bp, tp, d = x_prompt.shape
    bs, ts, _ = x_sample.shape
    depth = w_in.shape[0]
    heads, dk, dv = state_delta.shape[3:]
    qk = heads * dk
    vw = heads * dv
    lru_w = state_lru.shape[-1]
    n_experts = w_router.shape[-1]
    n_p, n_s = bp * tp, bs * ts
    n = n_p + n_s
    cap_p = 2 * n_p // n_experts
    cap_s = 2 * n_s // n_experts
    assert dk == LANES and dv == LANES and n_p % ts == 0 and ts % tp == 0

    xs = x_sample + _pos_embed_2d(ts, d)[None]
    x = jnp.concatenate([x_prompt.reshape(n_p, d), xs.reshape(n_s, d)], axis=0)

    cond8 = jnp.zeros((8, d), F32).at[0].set(c_ctx).at[1:1 + bs].set(c)
    mod = _modulation(cond8, w_mod, b_mod)
    mod = mod.reshape(depth, 8, 6, 1, d)

    n_small = 2 * N_DIR * heads
    c0 = 2 * qk + 2 * vw
    w_main, w_ba = _w_in_prep(w_in, c0, n_small)
    lx_blk = c0 // LANES
    ly_blk = lx_blk + lru_w // LANES
    ga_blk = (c0 + 2 * lru_w) // d
    gb_blk = ga_blk + 1
    w_dn_b = w_dn_out.astype(BF16)
    w_lru_b = w_lru_out.astype(BF16)
    w_o_b = w_o.astype(BF16)
    w_r_b = jnp.pad(w_router, ((0, 0), (0, 0), (0, LANES - n_experts))).astype(BF16)
    wa_b = lru_wa.astype(BF16)
    wx_b = lru_wx.astype(BF16)
    lane_pad = lambda a: jnp.pad(a.reshape(depth, 1, N_DIR * heads), ((0, 0), (0, 0), (0, LANES - N_DIR * heads)))
    al_v = lane_pad(dn_a_log)
    dt_v = lane_pad(dn_dt_bias)
    norm1_3 = norm1_g.reshape(depth, 1, d)
    norm2_3 = norm2_g.reshape(depth, 1, d)
    dn_norm_3 = dn_norm_g.reshape(depth, 1, dv)
    conv_lru_b3 = conv_lru_b.reshape(depth, 1, lru_w)

    tm_mg = 512
    tiles_p_mg = n_p // tm_mg
    per_seq = ts // tm_mg
    row_mg = lambda i: jnp.where(i < tiles_p_mg, 0, (i - tiles_p_mg) // per_seq + 1)
    tiles_p_cb = n_p // ROUTE_TB
    per_seq_cb = ts // ROUTE_TB
    row_cb = lambda i: jnp.where(i < tiles_p_cb, 0, (i - tiles_p_cb) // per_seq_cb + 1)
    assert n_p % tm_mg == 0 and ts % tm_mg == 0 and n_p % ROUTE_TB == 0 and ts % ROUTE_TB == 0
    assert n_experts <= LANES and d % LANES == 0

    sd_acc = jnp.zeros((bp, depth, N_DIR, heads, dk, dv), F32)
    sl_out = []
    for l in range(depth):
        cols, ba = _in_proj(l, x, norm1_3, mod, w_main, w_ba, row_mg, tm_mg)
        o_p, sd_acc = _gdn_seq(l, cols, ba, conv_qkv, al_v, dt_v, dn_norm_3, None, 0, bp, tp, heads, dk,
                               s_acc=sd_acc)
        o_s, _ = _gdn_seq(l, cols, ba, conv_qkv, al_v, dt_v, dn_norm_3, state_delta, n_p, bs, ts, heads, dk)
        y_p, sl_p = _lru(l, cols, conv_lru_w, conv_lru_b3, wa_b, wx_b, lru_ba, lru_bx, lru_lambda, None,
                         0, bp, tp, lx_blk, ly_blk)
        y_s, _ = _lru(l, cols, conv_lru_w, conv_lru_b3, wa_b, wx_b, lru_ba, lru_bx, lru_lambda, state_lru,
                      n_p, bs, ts, lx_blk, ly_blk)
        x1, hn_aug, afft = _merge(l, o_p, o_s, y_p, y_s, cols, x, mod, norm2_3, w_dn_b, w_lru_b, w_o_b, w_r_b,
                                  row_mg, tm_mg, ga_blk, gb_blk, n_experts)
        thr_need = _thresholds(afft, ((0, n_p, cap_p), (n_p, n, cap_s)))
        slot, off, cnt, lst = _route(hn_aug, thr_need, n_p, cap_p, n_experts)
        nblk = n // ROUTE_TB
        off = off.reshape(nblk, LANES)
        cnt = cnt.reshape(nblk, LANES)
        xe_p = _gather(off, cnt, lst, hn_aug, 0, n_p, 0, cap_p)
        xe_s = _gather(off, cnt, lst, hn_aug, n_p, n_s, cap_p, cap_s)
        ye = _ffn(l, xe_p, xe_s, w_gate, w_up, w_down, 512)
        x = _combine(l, off, cnt, ye, slot, x1, mod, final_g, row_cb, n_experts, l == depth - 1)
        sl_out.append(sl_p)

    y = x
    y_prompt = y[:n_p].reshape(bp, tp, d)
    y_sample = y[n_p:].reshape(bs, ts, d)
    return (y_prompt, y_sample, sd_acc, jnp.stack(sl_out, axis=1))
```

```python
import functools
import math

import jax
import jax.numpy as jnp
from jax import lax
from jax.experimental import pallas as pl
from jax.experimental.pallas import tpu as pltpu

F32 = jnp.float32
BF16 = jnp.bfloat16

EPS = 1e-6
CHUNK = 64
SUB = 8
GDN_BATCH = 32
CONV_LEFT = 2
CONV_W = 4
LRU_C = 8.0
LRU_GROUP = 4
N_DIR = 2
GRID_W = 64
LANES = 128
SUBLANES = 8
VMEM_LIMIT = 56 * 1024 * 1024


def _cparams(sem):
    return pltpu.CompilerParams(dimension_semantics=sem, vmem_limit_bytes=VMEM_LIMIT)


def _bdot(a, b):
    return jnp.dot(a.astype(BF16), b.astype(BF16), preferred_element_type=F32)


def _mod_kernel(c_ref, w_ref, b_ref, o_ref):
    c = c_ref[...]
    c = c * jax.nn.sigmoid(c)
    o_ref[...] = _bdot(c, w_ref[...]) + b_ref[...]


def _modulation(cond8, w_mod, b_mod):
    depth, d, n6 = w_mod.shape
    tn = 1536
    return pl.pallas_call(
        _mod_kernel,
        grid=(depth, n6 // tn),
        in_specs=[
            pl.BlockSpec((8, d), lambda l, j: (0, 0)),
            pl.BlockSpec((None, d, tn), lambda l, j: (l, 0, j)),
            pl.BlockSpec((None, 1, tn), lambda l, j: (l, 0, j)),
        ],
        out_specs=pl.BlockSpec((None, 8, tn), lambda l, j: (l, 0, j)),
        out_shape=jax.ShapeDtypeStruct((depth, 8, n6), F32),
        compiler_params=_cparams(("parallel", "parallel")),
        name="modulation",
    )(cond8, w_mod, b_mod.reshape(depth, 1, n6))


def _w_in_prep_kernel(w_ref, main_ref, ba_ref, *, c0, n_small):
    w = w_ref[...]
    rows = w.shape[0]
    main_ref[...] = jnp.concatenate([w[:, :c0], w[:, c0 + n_small:]], axis=1).astype(BF16)
    ba_ref[...] = jnp.concatenate(
        [w[:, c0:c0 + n_small], jnp.zeros((rows, LANES - n_small), F32)], axis=1).astype(BF16)


def _w_in_prep(w_in, c0, n_small):
    depth, d, ncol = w_in.shape
    tk = 256
    return pl.pallas_call(
        functools.partial(_w_in_prep_kernel, c0=c0, n_small=n_small),
        grid=(depth, d // tk),
        in_specs=[pl.BlockSpec((None, tk, ncol), lambda l, i: (l, i, 0))],
        out_specs=[
            pl.BlockSpec((None, tk, ncol - n_small), lambda l, i: (l, i, 0)),
            pl.BlockSpec((None, tk, LANES), lambda l, i: (l, i, 0)),
        ],
        out_shape=[
            jax.ShapeDtypeStruct((depth, d, ncol - n_small), BF16),
            jax.ShapeDtypeStruct((depth, d, LANES), BF16),
        ],
        compiler_params=_cparams(("parallel", "parallel")),
        name="w_in_prep",
    )(w_in)


IN_PROJ_TN = 1536


def _in_proj_kernel(x_ref, g_ref, sh_ref, sc_ref, w_ref, wba_ref, o_ref, oba_ref):
    x = x_ref[...]
    y = x * lax.rsqrt(jnp.mean(x * x, axis=-1, keepdims=True) + EPS) * g_ref[...]
    hn = (y * (1.0 + sc_ref[...]) + sh_ref[...]).astype(BF16)
    oba_ref[...] = jnp.dot(hn, wba_ref[...], preferred_element_type=F32)
    for c in range(0, o_ref.shape[1], IN_PROJ_TN):
        o_ref[:, c:c + IN_PROJ_TN] = jnp.dot(hn, w_ref[:, c:c + IN_PROJ_TN], preferred_element_type=F32)


def _in_proj(l, x, norm_g, mod5, w_main, w_ba, row_of_tile, tm):
    n, d = x.shape
    ncols = w_main.shape[2]
    assert ncols % IN_PROJ_TN == 0
    resident = pl.Buffered(1)
    return pl.pallas_call(
        _in_proj_kernel,
        grid=(n // tm,),
        in_specs=[
            pl.BlockSpec((tm, d), lambda i: (i, 0)),
            pl.BlockSpec((None, 1, d), lambda i: (l, 0, 0)),
            pl.BlockSpec((None, None, None, 1, d), lambda i: (l, row_of_tile(i), 0, 0, 0)),
            pl.BlockSpec((None, None, None, 1, d), lambda i: (l, row_of_tile(i), 1, 0, 0)),
            pl.BlockSpec((None, d, ncols), lambda i: (l, 0, 0), pipeline_mode=resident),
            pl.BlockSpec((None, d, LANES), lambda i: (l, 0, 0), pipeline_mode=resident),
        ],
        out_specs=[
            pl.BlockSpec((tm, ncols), lambda i: (i, 0)),
            pl.BlockSpec((tm, LANES), lambda i: (i, 0)),
        ],
        out_shape=[
            jax.ShapeDtypeStruct((n, ncols), F32),
            jax.ShapeDtypeStruct((n, LANES), F32),
        ],
        compiler_params=_cparams(("parallel",)),
        name="in_proj",
    )(x, norm_g, mod5, mod5, w_main, w_ba)


def _conv_rows(x, w, row):
    t = x.shape[0]
    acc = x * w[CONV_LEFT:CONV_LEFT + 1, :]
    for j in range(CONV_W):
        off = j - CONV_LEFT
        if off == 0:
            continue
        xs = pltpu.roll(x, (-off) % t, axis=0)
        valid = (row + off >= 0) & (row + off < t)
        acc = acc + jnp.where(valid, xs, 0.0) * w[j:j + 1, :]
    return acc


def _softplus(x):
    return jnp.maximum(x, 0.0) + jnp.log1p(jnp.exp(-jnp.abs(x)))


def _bmm(a, b):
    return jnp.einsum('nij,njk->nik', a.astype(BF16), b.astype(BF16), preferred_element_type=F32)


def _bmm_nt(a, b, precision=None):
    return jnp.einsum('nid,njd->nij', a, b, preferred_element_type=F32, precision=precision)


def _unit_tri_inverse(a, ii, jj):
    def same(b):
        return (ii // b) == (jj // b)

    eye = (ii == jj).astype(F32)
    d1 = jnp.where(same(SUB), a, 0.0)
    d2 = _bmm(d1, d1)
    d4 = _bmm(d2, d2)
    x = eye - d1
    x = x + _bmm(x, d2)
    x = x + _bmm(x, d4)
    b = SUB
    while b < CHUNK:
        o = jnp.where(same(2 * b) & jnp.logical_not(same(b)), a, 0.0)
        x = x - _bmm(_bmm(x, o), x)
        b *= 2
    return x


def _gdn_seq_kernel(*refs, t, has_s0, has_acc, heads, dk, hg):
    n_st = N_DIR * heads
    n_in = 12 + int(has_s0) + int(has_acc)
    ins, outs = refs[:n_in], refs[n_in:]
    (q_ref, k_ref, v_ref, z_ref, ba_ref, cq_ref, ck_ref, cv_ref, al_ref, dt_ref, ng_ref, oh_ref) = ins[:12]
    s0_ref = ins[12] if has_s0 else None
    o_ref, s_ref, x_scr, at_scr, kq_scr, kd_scr, vb_scr, cd_scr, o_scr = outs[:9]
    st_scrs = outs[9:9 + n_st]
    n = t // CHUNK
    row = lax.broadcasted_iota(jnp.int32, (t, LANES), 0)
    pos = row % CHUNK
    ii = lax.broadcasted_iota(jnp.int32, (CHUNK, CHUNK), 0)
    jj = lax.broadcasted_iota(jnp.int32, (CHUNK, CHUNK), 1)
    ba = ba_ref[...]
    al = al_ref[...]
    dtb = dt_ref[...]

    def conv_silu(x_ref, w_ref, h):
        y = _conv_rows(x_ref[:, h * LANES:(h + 1) * LANES], w_ref[:, h * LANES:(h + 1) * LANES], row)
        return y * jax.nn.sigmoid(y)

    def l2n(x):
        return x * lax.rsqrt(jnp.sum(x * x, axis=-1, keepdims=True) + EPS)

    nb = hg * n
    ii3 = lax.broadcasted_iota(jnp.int32, (N_DIR * nb, CHUNK, CHUNK), 1)
    jj3 = lax.broadcasted_iota(jnp.int32, (N_DIR * nb, CHUNK, CHUNK), 2)
    fwd = lax.broadcasted_iota(jnp.int32, (N_DIR * nb, CHUNK, CHUNK), 0) < nb
    ahead = jnp.where(fwd, ii3 - jj3, jj3 - ii3)
    tri = ahead >= 0
    strict = ahead > 0
    onehot0 = jnp.broadcast_to(oh_ref[...][None], (N_DIR * nb, CHUNK, LANES))

    for h0 in range(0, heads, hg):
        hs = range(h0, h0 + hg)
        as3 = lambda x: x.reshape(n, CHUNK, LANES)
        q3 = jnp.concatenate([as3(l2n(conv_silu(q_ref, cq_ref, h)) * (dk ** -0.5)) for h in hs], axis=0)
        k3 = jnp.concatenate([as3(l2n(conv_silu(k_ref, ck_ref, h))) for h in hs], axis=0)
        v3 = jnp.concatenate([as3(conv_silu(v_ref, cv_ref, h)) for h in hs], axis=0)
        k3b = k3.astype(BF16)
        gram = _bmm_nt(k3b, k3b)
        qk = _bmm_nt(q3.astype(BF16), k3b)
        decs, tots, betas = [], [], []
        for d in range(N_DIR):
            for h in hs:
                lb = d * heads + h
                la = 2 * heads + lb
                beta = jax.nn.sigmoid(ba[:, lb:lb + 1])
                g = -jnp.exp(al[:, lb:lb + 1]) * _softplus(ba[:, la:la + 1] + dtb[:, lb:lb + 1])
                dec = jnp.broadcast_to(g, (t, LANES))
                step = 1
                while step < CHUNK:
                    if d == 0:
                        sh = pltpu.roll(dec, step, axis=0)
                        dec = dec + jnp.where(pos >= step, sh, 0.0)
                    else:
                        sh = pltpu.roll(dec, t - step, axis=0)
                        dec = dec + jnp.where(pos < CHUNK - step, sh, 0.0)
                    step *= 2
                dec3 = as3(dec)
                decs.append(dec3)
                tots.append(jnp.broadcast_to(dec3[:, CHUNK - 1:CHUNK, :] if d == 0 else dec3[:, 0:1, :],
                                             (n, CHUNK, LANES)))
                betas.append(as3(jnp.broadcast_to(beta, (t, LANES))))
        dec3 = jnp.concatenate(decs, axis=0)
        tot3 = jnp.concatenate(tots, axis=0)
        beta3 = jnp.concatenate(betas, axis=0)
        both = lambda x: jnp.concatenate([x, x], axis=0)
        dec_row = _bmm_nt(onehot0, dec3, precision=lax.Precision.HIGHEST)
        dec_col = dec3[:, :, :CHUNK]
        gamma = jnp.where(tri, jnp.exp(jnp.where(tri, dec_col - dec_row, 0.0)), 0.0)
        a = jnp.where(strict, both(gram) * gamma, 0.0) * beta3[:, :, :CHUNK]
        xinv = _unit_tri_inverse(a, ii, jj).astype(BF16)
        attn = jnp.where(tri, both(qk) * gamma, 0.0).astype(BF16)
        edec = jnp.exp(dec3)
        k32, q32, v32 = both(k3), both(q3), both(v3)
        kbd = (k32 * (beta3 * edec)).astype(BF16)
        qd = (q32 * edec).astype(BF16)
        kd = jnp.swapaxes(k32 * jnp.exp(tot3 - dec3), 1, 2).astype(BF16)
        vb = v32 * beta3
        cd = jnp.exp(tot3[:, :SUBLANES, :])
        for d in range(N_DIR):
            src = slice(d * nb, (d + 1) * nb)
            dst = pl.ds((d * heads + h0) * n, nb)
            x_scr[dst] = xinv[src]
            at_scr[dst] = attn[src]
            kq_scr[dst, :CHUNK, :] = kbd[src]
            kq_scr[dst, CHUNK:, :] = qd[src]
            kd_scr[dst] = kd[src]
            vb_scr[dst] = vb[src]
            cd_scr[dst] = cd[src]

    for d in range(N_DIR):
        for h in range(heads):
            if has_s0:
                st_scrs[d * heads + h][...] = s0_ref[d, h]
            else:
                st_scrs[d * heads + h][...] = jnp.zeros((dk, LANES), F32)

    def scan(it, carry):
        chains = [(d * heads + h, (d * heads + h) * n + (it if d == 0 else n - 1 - it))
                  for h in range(heads) for d in range(N_DIR)]
        ss = [st_scrs[i][...] for i, _ in chains]
        rqs = [jnp.dot(kq_scr[b], s.astype(BF16), preferred_element_type=F32)
               for (_, b), s in zip(chains, ss)]
        vns = [jnp.dot(x_scr[b], (vb_scr[b] - rq[:CHUNK]).astype(BF16), preferred_element_type=F32).astype(BF16)
               for (_, b), rq in zip(chains, rqs)]
        for (i, b), s, rq, vn in zip(chains, ss, rqs, vns):
            o_scr[b] = rq[CHUNK:] + jnp.dot(at_scr[b], vn, preferred_element_type=F32)
            st_scrs[i][...] = s * cd_scr[b][0:1, :] + jnp.dot(kd_scr[b], vn, preferred_element_type=F32)
        return carry

    lax.fori_loop(0, n, scan, 0)

    for h in range(heads):
        for d in range(N_DIR):
            s_ref[d, h] = st_scrs[d * heads + h][...]
        of = o_scr[pl.ds(h * n, n)]
        ob = o_scr[pl.ds((heads + h) * n, n)]
        o = (of + ob).reshape(t, LANES)
        o = o * lax.rsqrt(jnp.mean(o * o, axis=-1, keepdims=True) + EPS) * ng_ref[...]
        z = z_ref[:, h * LANES:(h + 1) * LANES]
        o_ref[:, h * LANES:(h + 1) * LANES] = (o * (z * jax.nn.sigmoid(z))).astype(o_ref.dtype)


def _gdn_seq(l, cols, ba, conv_qkv, al, dtb, norm_g, s0, row0, nb, t, heads, dk, s_acc=None):
    blk0 = row0 // t
    has_s0 = s0 is not None
    has_acc = s_acc is not None
    hw = heads * LANES
    n = t // CHUNK
    col = lambda off: pl.BlockSpec((t, hw), lambda b: (blk0 + b, off))
    cw = lambda off: pl.BlockSpec((None, CONV_W, hw), lambda b: (l, 0, off))
    vec = pl.BlockSpec((None, 1, LANES), lambda b: (l, 0, 0))
    in_specs = [
        col(0), col(1), col(2), col(3),
        pl.BlockSpec((t, LANES), lambda b: (blk0 + b, 0)),
        cw(0), cw(1), cw(2),
        vec, vec, vec,
        pl.BlockSpec((CHUNK, LANES), lambda b: (0, 0)),
    ]
    onehot0 = jnp.zeros((CHUNK, LANES), F32).at[:, 0].set(1.0)
    args = [cols, cols, cols, cols, ba, conv_qkv, conv_qkv, conv_qkv, al, dtb, norm_g, onehot0]
    if has_s0:
        in_specs.append(pl.BlockSpec((None, None, N_DIR, heads, dk, LANES), lambda b: (b, l, 0, 0, 0, 0)))
        args.append(s0)
    if has_acc:
        in_specs.append(pl.BlockSpec(memory_space=pl.ANY))
        args.append(s_acc)
        state_spec = pl.BlockSpec((None, None, N_DIR, heads, dk, LANES), lambda b: (b, l, 0, 0, 0, 0))
        state_shape = jax.ShapeDtypeStruct(s_acc.shape, F32)
        aliases = {len(args) - 1: 1}
    else:
        state_spec = pl.BlockSpec((None, N_DIR, heads, dk, LANES), lambda b: (b, 0, 0, 0, 0))
        state_shape = jax.ShapeDtypeStruct((nb, N_DIR, heads, dk, LANES), F32)
        aliases = {}
    per = (N_DIR * heads * n,)
    hg = max(1, min(heads, GDN_BATCH // (N_DIR * n)))
    assert heads % hg == 0
    return pl.pallas_call(
        functools.partial(_gdn_seq_kernel, t=t, has_s0=has_s0, has_acc=has_acc, heads=heads, dk=dk, hg=hg),
        grid=(nb,),
        in_specs=in_specs,
        out_specs=[pl.BlockSpec((t, hw), lambda b: (b, 0)), state_spec],
        out_shape=[jax.ShapeDtypeStruct((nb * t, hw), BF16), state_shape],
        input_output_aliases=aliases,
        scratch_shapes=[
            pltpu.VMEM(per + (CHUNK, CHUNK), BF16),
            pltpu.VMEM(per + (CHUNK, CHUNK), BF16),
            pltpu.VMEM(per + (2 * CHUNK, LANES), BF16),
            pltpu.VMEM(per + (dk, CHUNK), BF16),
            pltpu.VMEM(per + (CHUNK, LANES), F32),
            pltpu.VMEM(per + (SUBLANES, LANES), F32),
            pltpu.VMEM(per + (CHUNK, LANES), F32),
        ] + [pltpu.VMEM((dk, LANES), F32)] * (N_DIR * heads),
        compiler_params=_cparams(("parallel",)),
        name=f"gdn_t{t}",
    )(*args)


def _lru_kernel(*refs, t, has_h0):
    if has_h0:
        (lx_ref, ly_ref, cw_ref, cb_ref, wa_ref, wx_ref, ba_ref, bx_ref, lam_ref, h0_ref,
         y_ref, last_ref, a_scr, b_scr, h_scr) = refs
    else:
        (lx_ref, ly_ref, cw_ref, cb_ref, wa_ref, wx_ref, ba_ref, bx_ref, lam_ref,
         y_ref, last_ref, a_scr, b_scr, h_scr) = refs
        h0_ref = None
    nblk = t // SUBLANES
    wl = lx_ref.shape[1]
    row = lax.broadcasted_iota(jnp.int32, (t, wl), 0)
    sub = lax.broadcasted_iota(jnp.int32, (nblk, SUBLANES, wl), 1)
    u = _conv_rows(lx_ref[...], cw_ref[...], row) + cb_ref[...]
    ub = u.astype(BF16)

    def block_diag(w_ref, d):
        return jnp.concatenate(
            [jnp.dot(ub[:, c * LANES:(c + 1) * LANES], w_ref[d, c], preferred_element_type=F32)
             for c in range(wl // LANES)], axis=1)

    for d in range(N_DIR):
        r = jax.nn.sigmoid(block_diag(wa_ref, d) + ba_ref[d:d + 1, :])
        i = jax.nn.sigmoid(block_diag(wx_ref, d) + bx_ref[d:d + 1, :])
        log_a = (LRU_C * r) * (-_softplus(-lam_ref[d:d + 1, :]))
        a = jnp.exp(log_a)
        b = jnp.sqrt(1.0 - a * a) * (i * u)
        a3 = a.reshape(nblk, SUBLANES, wl)
        b3 = b.reshape(nblk, SUBLANES, wl)
        step = 1
        while step < SUBLANES:
            if d == 0:
                a_s = pltpu.roll(a3, step, axis=1)
                b_s = pltpu.roll(b3, step, axis=1)
                m = sub >= step
            else:
                a_s = pltpu.roll(a3, SUBLANES - step, axis=1)
                b_s = pltpu.roll(b3, SUBLANES - step, axis=1)
                m = sub < SUBLANES - step
            b3 = b3 + a3 * jnp.where(m, b_s, 0.0)
            a3 = a3 * jnp.where(m, a_s, 1.0)
            step *= 2
        a_scr[d] = a3
        b_scr[d] = b3

    if has_h0:
        h_init = (jnp.broadcast_to(h0_ref[0:1, :], (SUBLANES, wl)),
                  jnp.broadcast_to(h0_ref[1:2, :], (SUBLANES, wl)))
    else:
        h_init = (jnp.zeros((SUBLANES, wl), F32), jnp.zeros((SUBLANES, wl), F32))

    def body(it, carry):
        hf, hb = carry
        kf = it
        kb = nblk - 1 - it
        new_f = b_scr[0, kf] + a_scr[0, kf] * hf
        new_b = b_scr[1, kb] + a_scr[1, kb] * hb
        h_scr[0, kf] = new_f
        h_scr[1, kb] = new_b
        hf = jnp.broadcast_to(new_f[SUBLANES - 1:SUBLANES, :], (SUBLANES, wl))
        hb = jnp.broadcast_to(new_b[0:1, :], (SUBLANES, wl))
        return hf, hb

    hf, hb = lax.fori_loop(0, nblk, body, h_init, unroll=4)
    last_ref[0:1, :] = hf[0:1, :]
    last_ref[1:2, :] = hb[0:1, :]
    rec = (h_scr[0] + h_scr[1]).reshape(t, wl)
    y_ref[...] = (jax.nn.gelu(ly_ref[...]) * rec).astype(y_ref.dtype)


def _lru(l, cols, conv_w, conv_b, wa, wx, ba, bx, lam, h0, row0, nb, t, lx_blk, ly_blk):
    nblocks = wa.shape[2]
    w = nblocks * LANES
    cg = LRU_GROUP
    wl = cg * LANES
    blk0 = row0 // t
    has_h0 = h0 is not None
    in_specs = [
        pl.BlockSpec((t, wl), lambda b, c: (blk0 + b, lx_blk // cg + c)),
        pl.BlockSpec((t, wl), lambda b, c: (blk0 + b, ly_blk // cg + c)),
        pl.BlockSpec((None, CONV_W, wl), lambda b, c: (l, 0, c)),
        pl.BlockSpec((None, 1, wl), lambda b, c: (l, 0, c)),
        pl.BlockSpec((None, N_DIR, cg, LANES, LANES), lambda b, c: (l, 0, c, 0, 0)),
        pl.BlockSpec((None, N_DIR, cg, LANES, LANES), lambda b, c: (l, 0, c, 0, 0)),
        pl.BlockSpec((None, N_DIR, wl), lambda b, c: (l, 0, c)),
        pl.BlockSpec((None, N_DIR, wl), lambda b, c: (l, 0, c)),
        pl.BlockSpec((None, N_DIR, wl), lambda b, c: (l, 0, c)),
    ]
    args = [cols, cols, conv_w, conv_b, wa, wx, ba, bx, lam]
    if has_h0:
        in_specs.append(pl.BlockSpec((None, None, N_DIR, wl), lambda b, c: (b, l, 0, c)))
        args.append(h0)
    nblk = t // SUBLANES
    assert lx_blk % cg == 0 and ly_blk % cg == 0 and nblocks % cg == 0
    return pl.pallas_call(
        functools.partial(_lru_kernel, t=t, has_h0=has_h0),
        grid=(nb, nblocks // cg),
        in_specs=in_specs,
        out_specs=[
            pl.BlockSpec((t, wl), lambda b, c: (b, c)),
            pl.BlockSpec((None, N_DIR, wl), lambda b, c: (b, 0, c)),
        ],
        out_shape=[
            jax.ShapeDtypeStruct((nb * t, w), BF16),
            jax.ShapeDtypeStruct((nb, N_DIR, w), F32),
        ],
        scratch_shapes=[pltpu.VMEM((N_DIR, nblk, SUBLANES, wl), F32)] * 3,
        compiler_params=_cparams(("parallel", "parallel")),
        name=f"lru_t{t}",
    )(*args)


MERGE_PARTS = 2


def _merge_kernel(oap_ref, oas_ref, obp_ref, obs_ref, ga_ref, gb_ref, x_ref, g1_ref, sh2_ref, sc2_ref,
                  n2_ref, wdn_ref, wlru_ref, wo_ref, wr_ref, x1_ref, hn_ref, afft_ref, *, n_experts, tiles_p):
    d = x_ref.shape[1]
    tm = x_ref.shape[0]
    is_p = pl.program_id(0) < tiles_p
    parts = [slice(r, r + tm // MERGE_PARTS) for r in range(0, tm, tm // MERGE_PARTS)]
    yas = [jnp.dot(jnp.where(is_p, oap_ref[p, :], oas_ref[p, :]), wdn_ref[...], preferred_element_type=F32)
           for p in parts]
    ybs = [jnp.dot(jnp.where(is_p, obp_ref[p, :], obs_ref[p, :]), wlru_ref[...], preferred_element_type=F32)
           for p in parts]
    mixes = [(jax.nn.sigmoid(ga_ref[p, :]) * ya + jax.nn.sigmoid(gb_ref[p, :]) * yb).astype(BF16)
             for p, ya, yb in zip(parts, yas, ybs)]
    outs = [jnp.dot(m, wo_ref[...], preferred_element_type=F32) for m in mixes]
    hns = []
    for p, mo in zip(parts, outs):
        x1 = x_ref[p, :] + g1_ref[...] * mo
        x1_ref[p, :] = x1
        y = x1 * lax.rsqrt(jnp.mean(x1 * x1, axis=-1, keepdims=True) + EPS) * n2_ref[...]
        hns.append(y * (1.0 + sc2_ref[...]) + sh2_ref[...])
    logit_parts = [jnp.dot(hn.astype(BF16), wr_ref[...], preferred_element_type=F32) for hn in hns]
    for p, hn, logits in zip(parts, hns, logit_parts):
        lane = lax.broadcasted_iota(jnp.int32, logits.shape, 1)
        logits = jnp.where(lane < n_experts, logits, -jnp.inf)
        e = jnp.exp(logits - jnp.max(logits, axis=-1, keepdims=True))
        aff = e / jnp.sum(e, axis=-1, keepdims=True)
        hn_ref[p, :d] = hn
        hn_ref[p, d:] = aff
        afft_ref[:, p] = aff.T[:n_experts, :]


def _merge(l, o_p, o_s, y_p, y_s, cols, x, mod5, norm2_g, w_dn, w_lru, w_o, w_r, row_of_tile, tm,
           ga_blk, gb_blk, n_experts):
    n, d = x.shape
    dv = o_p.shape[1]
    w = y_p.shape[1]
    tiles_p = o_p.shape[0] // tm
    modspec = lambda k: pl.BlockSpec((None, None, None, 1, d), lambda i: (l, row_of_tile(i), k, 0, 0))
    layer = lambda a: pl.BlockSpec((None,) + a.shape[1:], lambda i: (l,) + (0,) * (a.ndim - 1))
    p_tile = lambda width: pl.BlockSpec((tm, width), lambda i: (jnp.minimum(i, tiles_p - 1), 0))
    s_tile = lambda width: pl.BlockSpec((tm, width), lambda i: (jnp.maximum(i - tiles_p, 0), 0))
    return pl.pallas_call(
        functools.partial(_merge_kernel, n_experts=n_experts, tiles_p=tiles_p),
        grid=(n // tm,),
        in_specs=[
            p_tile(dv), s_tile(dv), p_tile(w), s_tile(w),
            pl.BlockSpec((tm, d), lambda i: (i, ga_blk)),
            pl.BlockSpec((tm, d), lambda i: (i, gb_blk)),
            pl.BlockSpec((tm, d), lambda i: (i, 0)),
            modspec(2), modspec(3), modspec(4),
            layer(norm2_g), layer(w_dn), layer(w_lru), layer(w_o), layer(w_r),
        ],
        out_specs=[
            pl.BlockSpec((tm, d), lambda i: (i, 0)),
            pl.BlockSpec((tm, d + LANES), lambda i: (i, 0)),
            pl.BlockSpec((n_experts, tm), lambda i: (0, i)),
        ],
        out_shape=[
            jax.ShapeDtypeStruct((n, d), F32),
            jax.ShapeDtypeStruct((n, d + LANES), F32),
            jax.ShapeDtypeStruct((n_experts, n), F32),
        ],
        compiler_params=_cparams(("parallel",)),
        name="merge",
    )(o_p, o_s, y_p, y_s, cols, cols, x, mod5, mod5, mod5, norm2_g, w_dn, w_lru, w_o, w_r)


def _ffn_kernel(xp_ref, xs_ref, wg_ref, wu_ref, wd_ref, o_ref, xb_scr):
    f = pl.program_id(1)
    cap_p = xp_ref.shape[0]
    d = xb_scr.shape[1]

    @pl.when(f == 0)
    def _():
        xb_scr[:cap_p, :] = xp_ref[:, :d].astype(BF16)
        xb_scr[cap_p:, :] = xs_ref[:, :d].astype(BF16)
        o_ref[...] = jnp.zeros_like(o_ref)

    x = xb_scr[...]
    g = jnp.dot(x, wg_ref[...].astype(BF16), preferred_element_type=F32)
    u = jnp.dot(x, wu_ref[...].astype(BF16), preferred_element_type=F32)
    hid = ((g * jax.nn.sigmoid(g)) * u).astype(BF16)
    o_ref[...] += jnp.dot(hid, wd_ref[...].astype(BF16), preferred_element_type=F32)

    @pl.when(f == pl.num_programs(1) - 1)
    def _():
        e = pl.program_id(0)
        for ref, r0 in ((xp_ref, 0), (xs_ref, cap_p)):
            aff = ref[:, d:]
            lane = lax.broadcasted_iota(jnp.int32, aff.shape, 1)
            gv = jnp.sum(jnp.where(lane == e, aff, 0.0), axis=1, keepdims=True)
            rows = ref.shape[0]
            o_ref[r0:r0 + rows, :] = o_ref[r0:r0 + rows, :] * gv


def _ffn(l, xe_p, xe_s, w_gate, w_up, w_down, tf):
    e, cap_p, da = xe_p.shape
    d = da - LANES
    cap_s = xe_s.shape[1]
    r = cap_p + cap_s
    ff = w_gate.shape[3]
    return pl.pallas_call(
        _ffn_kernel,
        grid=(e, ff // tf),
        in_specs=[
            pl.BlockSpec((None, cap_p, da), lambda i, f: (i, 0, 0)),
            pl.BlockSpec((None, cap_s, da), lambda i, f: (i, 0, 0)),
            pl.BlockSpec((None, None, d, tf), lambda i, f: (l, i, 0, f)),
            pl.BlockSpec((None, None, d, tf), lambda i, f: (l, i, 0, f)),
            pl.BlockSpec((None, None, tf, d), lambda i, f: (l, i, f, 0)),
        ],
        out_specs=pl.BlockSpec((None, r, d), lambda i, f: (i, 0, 0)),
        out_shape=jax.ShapeDtypeStruct((e, r, d), F32),
        scratch_shapes=[pltpu.VMEM((r, d), BF16)],
        compiler_params=_cparams(("parallel", "arbitrary")),
        name="expert_ffn",
    )(xe_p, xe_s, w_gate, w_up, w_down)


ROUTE_TB = 128


def _thr_kernel(afft_ref, o_ref, *, groups):
    n_e = afft_ref.shape[0]
    sub = lax.broadcasted_iota(jnp.int32, (n_e, LANES), 0)
    lane = lax.broadcasted_iota(jnp.int32, (n_e, LANES), 1)
    rows = []
    for lo, hi, cap in groups:
        a = afft_ref[:, lo:hi]
        above_all = 4.0

        def count_ge(v, a=a):
            return jnp.sum((a >= v).astype(F32), axis=1, keepdims=True)

        def count_gt(v, a=a):
            return jnp.sum((a > v).astype(F32), axis=1, keepdims=True)

        def bisect(i, lh, cap=cap, count_ge=count_ge):
            lo_v, hi_v = lh
            mid = 0.5 * (lo_v + hi_v)
            ok = count_ge(mid) >= cap
            return jnp.where(ok, mid, lo_v), jnp.where(ok, hi_v, mid)

        lo_v, _ = lax.fori_loop(0, 48, bisect, (jnp.zeros((n_e, 1), F32), jnp.full((n_e, 1), 2.0, F32)))
        thr = jnp.min(jnp.where(a >= lo_v, a, above_all), axis=1, keepdims=True)

        def not_done(thr, cap=cap, count_gt=count_gt):
            return jnp.max(count_gt(thr)) >= cap

        def step_up(thr, a=a, cap=cap, count_gt=count_gt):
            nxt = jnp.min(jnp.where(a > thr, a, above_all), axis=1, keepdims=True)
            return jnp.where(count_gt(thr) >= cap, nxt, thr)

        thr = lax.while_loop(not_done, step_up, thr)
        need = cap - count_gt(thr)
        for col in (thr, need):
            m = jnp.where(sub == lane, jnp.broadcast_to(col, (n_e, LANES)), 0.0)
            rows.append(jnp.sum(m, axis=0, keepdims=True))
    rows.append(jnp.zeros((SUBLANES - len(rows), LANES), F32))
    o_ref[...] = jnp.concatenate(rows, axis=0)


def _thresholds(afft, groups):
    n_e, n = afft.shape
    return pl.pallas_call(
        functools.partial(_thr_kernel, groups=groups),
        grid=(1,),
        in_specs=[pl.BlockSpec((n_e, n), lambda i: (0, 0))],
        out_specs=pl.BlockSpec((SUBLANES, LANES), lambda i: (0, 0)),
        out_shape=jax.ShapeDtypeStruct((SUBLANES, LANES), F32),
        compiler_params=_cparams(("arbitrary",)),
        name="route_thresholds",
    )(afft)


def _route_kernel(aff_ref, tn_ref, ls_ref, slot_ref, off_ref, cnt_ref, lst_ref,
                  run_eq, run_sel, *, nblk_p, cap_p, n_e):
    tb = ROUTE_TB
    j = pl.program_id(0)

    @pl.when(j == 0)
    def _():
        run_eq[...] = jnp.zeros_like(run_eq)
        run_sel[...] = jnp.zeros_like(run_sel)

    @pl.when(j == nblk_p)
    def _():
        run_eq[...] = jnp.zeros_like(run_eq)
        run_sel[...] = jnp.full_like(run_sel, float(cap_p))

    g = (j >= nblk_p).astype(jnp.int32)
    thr = tn_ref[pl.ds(2 * g, 1), :]
    need = tn_ref[pl.ds(2 * g + 1, 1), :]
    a = aff_ref[...]
    lane = lax.broadcasted_iota(jnp.int32, (tb, LANES), 1)
    valid = lane < n_e
    gt = (a > thr) & valid
    eq = (a == thr) & valid
    eqf = eq.astype(F32)
    ls = ls_ref[...]
    eq_rank = run_eq[...] + jnp.dot(ls, eqf.astype(BF16), preferred_element_type=F32)
    sel = gt | (eq & (eq_rank < need))
    self_ = sel.astype(F32)
    pos = jnp.dot(ls, self_.astype(BF16), preferred_element_type=F32)
    cnt = jnp.sum(self_, axis=0, keepdims=True)
    off = run_sel[...]
    run_eq[...] = run_eq[...] + jnp.sum(eqf, axis=0, keepdims=True)
    run_sel[...] = off + cnt
    slot_ref[...] = jnp.where(sel, off + pos, -1.0)
    off_ref[...] = off.astype(jnp.int32)
    cnt_ref[...] = cnt.astype(jnp.int32)

    tok = (lax.broadcasted_iota(jnp.int32, (tb, LANES), 0) + j * tb).astype(F32)
    lanef = lane.astype(F32)
    for e in range(n_e):
        pe = jnp.broadcast_to(pos[:, e:e + 1], (tb, LANES))
        se = jnp.broadcast_to(self_[:, e:e + 1], (tb, LANES))
        hit = (pe == lanef) & (se > 0.0)
        lst_ref[e] = jnp.sum(jnp.where(hit, tok, 0.0), axis=0, keepdims=True).astype(jnp.int32)


def _route(hn_aug, thr_need, n_p, cap_p, n_e):
    n = hn_aug.shape[0]
    aff_blk = hn_aug.shape[1] // LANES - 1
    tb = ROUTE_TB
    nblk = n // tb
    ls = jnp.tril(jnp.ones((tb, tb), F32), -1).astype(BF16)
    blk_row = pl.BlockSpec((None, 1, LANES), lambda j: (j, 0, 0))
    return pl.pallas_call(
        functools.partial(_route_kernel, nblk_p=n_p // tb, cap_p=cap_p, n_e=n_e),
        grid=(nblk,),
        in_specs=[
            pl.BlockSpec((tb, LANES), lambda j: (j, aff_blk)),
            pl.BlockSpec((SUBLANES, LANES), lambda j: (0, 0)),
            pl.BlockSpec((tb, tb), lambda j: (0, 0)),
        ],
        out_specs=[
            pl.BlockSpec((tb, LANES), lambda j: (j, 0)),
            blk_row,
            blk_row,
            pl.BlockSpec((n_e, None, 1, LANES), lambda j: (0, j, 0, 0)),
        ],
        out_shape=[
            jax.ShapeDtypeStruct((n, LANES), F32),
            jax.ShapeDtypeStruct((nblk, 1, LANES), jnp.int32),
            jax.ShapeDtypeStruct((nblk, 1, LANES), jnp.int32),
            jax.ShapeDtypeStruct((n_e, nblk, 1, LANES), jnp.int32),
        ],
        scratch_shapes=[pltpu.VMEM((1, LANES), F32), pltpu.VMEM((1, LANES), F32)],
        compiler_params=_cparams(("arbitrary",)),
        name="route_slots",
    )(hn_aug, thr_need, ls)


GATHER_UNROLL = 4


def _gather_kernel(off_ref, cnt_ref, lst_hbm, hn_hbm, o_ref, hn_scr, lst_smem, sem_h, sem_l,
                   *, row0, slot0, blk0, nblk_g):
    e = pl.program_id(0)
    load_lst = pltpu.make_async_copy(lst_hbm.at[e, pl.ds(blk0, nblk_g)], lst_smem, sem_l)
    load_lst.start()

    @pl.when(e == 0)
    def _():
        rows = hn_scr.shape[0]
        load = pltpu.make_async_copy(hn_hbm.at[pl.ds(row0, rows)], hn_scr, sem_h)
        load.start()
        load.wait()

    load_lst.wait()

    def block(jb, carry):
        c = cnt_ref[blk0 + jb, e]
        o = off_ref[blk0 + jb, e] - slot0

        def copy_row(q):
            t = lst_smem[jb, q] - row0
            o_ref[pl.ds(o + q, 1), :] = hn_scr[pl.ds(t, 1), :]

        def rows(qq, carry2):
            for u in range(GATHER_UNROLL):
                copy_row(qq * GATHER_UNROLL + u)
            return carry2

        def row(q, carry2):
            copy_row(q)
            return carry2

        full = c // GATHER_UNROLL
        lax.fori_loop(0, full, rows, 0)
        lax.fori_loop(full * GATHER_UNROLL, c, row, 0)
        return carry

    lax.fori_loop(0, nblk_g, block, 0)


def _gather(off, cnt, lst, hn, row0, rows, slot0, cap):
    n_e = lst.shape[0]
    width = hn.shape[1]
    blk0 = row0 // ROUTE_TB
    nblk_g = rows // ROUTE_TB
    grid_spec = pltpu.PrefetchScalarGridSpec(
        num_scalar_prefetch=2,
        grid=(n_e,),
        in_specs=[pl.BlockSpec(memory_space=pl.ANY), pl.BlockSpec(memory_space=pl.ANY)],
        out_specs=pl.BlockSpec((None, cap, width), lambda e, o_, c_: (e, 0, 0)),
        scratch_shapes=[
            pltpu.VMEM((rows, width), F32),
            pltpu.SMEM((nblk_g, LANES), jnp.int32),
            pltpu.SemaphoreType.DMA(()),
            pltpu.SemaphoreType.DMA(()),
        ],
    )
    return pl.pallas_call(
        functools.partial(_gather_kernel, row0=row0, slot0=slot0, blk0=blk0, nblk_g=nblk_g),
        grid_spec=grid_spec,
        out_shape=jax.ShapeDtypeStruct((n_e, cap, width), F32),
        compiler_params=_cparams(("arbitrary",)),
        name=f"gather_rows{rows}",
    )(off, cnt, lst.reshape(n_e, -1, LANES), hn)


COMB_CH = 32
COMB_GRP = 8
COMB_MAXCH = 80
COMB_DESC = 128


def _combine_kernel(off_ref, cnt_ref, y_hbm, slot_ref, x1_ref, g2_ref, fg_ref, o_ref,
                    buf, acc, desc, sems, *, n_e, final):
    tb = ROUTE_TB
    j = pl.program_id(0)
    nblk = pl.num_programs(0)
    r_total = y_hbm.shape[1]
    par = j % 2

    def issue(jj, p):
        s = jnp.int32(0)
        for e in range(n_e):
            o = off_ref[jj, e]
            c = cnt_ref[jj, e]
            st8 = (o // SUBLANES) * SUBLANES
            nch = jnp.where(c > 0, (o - st8 + c + COMB_CH - 1) // COMB_CH, 0)

            def one(k, s, e=e, st8=st8):
                lo_row = st8 + k * COMB_CH
                base = jnp.minimum(lo_row, r_total - COMB_CH)
                pltpu.make_async_copy(y_hbm.at[e, pl.ds(base, COMB_CH)], buf.at[p, s], sems.at[p]).start()
                desc[p, 0, s] = e
                desc[p, 1, s] = lo_row
                desc[p, 2, s] = base
                return s + 1

            s = lax.fori_loop(0, nch, one, s)
        desc[p, 3, 0] = s

    @pl.when(j == 0)
    def _():
        buf[...] = jnp.zeros_like(buf)

        def clear(i, carry):
            for p in range(2):
                for row in range(4):
                    desc[p, row, i] = 0
            return carry

        lax.fori_loop(0, COMB_DESC, clear, 0)
        issue(j, par)

    @pl.when(j + 1 < nblk)
    def _():
        issue(j + 1, 1 - par)

    n_ch = desc[par, 3, 0]

    def drain(k, carry):
        pltpu.make_async_copy(y_hbm.at[0, pl.ds(0, COMB_CH)], buf.at[par, k], sems.at[par]).wait()
        return carry

    lax.fori_loop(0, n_ch, drain, 0)

    slot = slot_ref[...]
    lane = lax.broadcasted_iota(jnp.int32, (tb, LANES), 1)
    lanef = lane.astype(F32)
    acc[...] = jnp.zeros_like(acc)

    per_lane = LANES // COMB_CH
    n_sub = COMB_GRP // per_lane

    def group(gi, carry):
        ebs = []
        for sub in range(n_sub):
            hit = jnp.zeros((tb, LANES), jnp.bool_)
            for c4 in range(per_lane):
                s = gi * COMB_GRP + sub * per_lane + c4
                e = desc[par, 0, s]
                lo_row = desc[par, 1, s].astype(F32)
                base = desc[par, 2, s].astype(F32)
                col = jnp.sum(jnp.where(lane == e, slot, 0.0), axis=1, keepdims=True)
                colb = jnp.broadcast_to(col, (tb, LANES))
                in_chunk = (lane >= c4 * COMB_CH) & (lane < (c4 + 1) * COMB_CH)
                match = (colb == base + (lanef - float(c4 * COMB_CH))) & (colb >= lo_row)
                hit = hit | (in_chunk & match & (s < n_ch))
            ebs.append(jnp.where(hit, 1.0, 0.0).astype(BF16))
        eb = jnp.concatenate(ebs, axis=1)
        yg = buf[par, pl.ds(gi * COMB_GRP, COMB_GRP)].reshape(COMB_GRP * COMB_CH, -1)
        y1 = yg.astype(BF16)
        r1 = yg - y1.astype(F32)
        y2 = r1.astype(BF16)
        y3 = (r1 - y2.astype(F32)).astype(BF16)
        acc[...] += jnp.dot(jnp.concatenate([eb, eb, eb], axis=1), jnp.concatenate([y1, y2, y3], axis=0),
                            preferred_element_type=F32)
        return carry

    lax.fori_loop(0, (n_ch + COMB_GRP - 1) // COMB_GRP, group, 0)

    x2 = x1_ref[...] + g2_ref[...] * acc[...]
    if final:
        x2 = x2 * lax.rsqrt(jnp.mean(x2 * x2, axis=-1, keepdims=True) + EPS) * fg_ref[...]
    o_ref[...] = x2


def _combine(l, off, cnt, ye, slot, x1, mod5, final_g, row_of_tile, n_e, final):
    n, d = x1.shape
    tb = ROUTE_TB
    assert LANES % COMB_CH == 0 and (COMB_GRP * COMB_CH) % LANES == 0 and COMB_MAXCH % COMB_GRP == 0
    assert COMB_DESC >= COMB_MAXCH >= n_e * -(-(tb + SUBLANES - 1) // COMB_CH)
    grid_spec = pltpu.PrefetchScalarGridSpec(
        num_scalar_prefetch=2,
        grid=(n // tb,),
        in_specs=[
            pl.BlockSpec(memory_space=pl.ANY),
            pl.BlockSpec((tb, LANES), lambda j, o_, c_: (j, 0)),
            pl.BlockSpec((tb, d), lambda j, o_, c_: (j, 0)),
            pl.BlockSpec((None, None, None, 1, d), lambda j, o_, c_: (l, row_of_tile(j), 5, 0, 0)),
            pl.BlockSpec((1, d), lambda j, o_, c_: (0, 0)),
        ],
        out_specs=pl.BlockSpec((tb, d), lambda j, o_, c_: (j, 0)),
        scratch_shapes=[
            pltpu.VMEM((2, COMB_MAXCH, COMB_CH, d), F32),
            pltpu.VMEM((tb, d), F32),
            pltpu.SMEM((2, 4, COMB_DESC), jnp.int32),
            pltpu.SemaphoreType.DMA((2,)),
        ],
    )
    return pl.pallas_call(
        functools.partial(_combine_kernel, n_e=n_e, final=final),
        grid_spec=grid_spec,
        out_shape=jax.ShapeDtypeStruct((n, d), F32),
        compiler_params=_cparams(("arbitrary",)),
        name="combine",
    )(off, cnt, ye, slot, x1, mod5, final_g.reshape(1, d))


def _pos_embed_2d(n_tokens, d_model):
    rows = n_tokens // GRID_W
    r = jnp.broadcast_to(jnp.arange(rows, dtype=F32)[:, None], (rows, GRID_W)).reshape(-1)
    col = jnp.broadcast_to(jnp.arange(GRID_W, dtype=F32)[None, :], (rows, GRID_W)).reshape(-1)
    quarter = d_model // 4
    freq = jnp.exp(-math.log(10000.0) * jnp.arange(quarter, dtype=F32) / quarter)
    ar = r[:, None] * freq
    ac = col[:, None] * freq
    return jnp.concatenate([jnp.sin(ar), jnp.cos(ar), jnp.sin(ac), jnp.cos(ac)], axis=-1)


def kernel(x_prompt, x_sample, state_delta, state_lru, c, c_ctx, norm1_g, w_mod, b_mod, w_in, conv_qkv, dn_a_log, dn_dt_bias, dn_norm_g, w_dn_out, conv_lru_w, conv_lru_b, lru_wa, lru_ba, lru_wx, lru_bx, lru_lambda, w_lru_out, w_o, norm2_g, w_router, w_gate, w_up, w_down, final_g):
    bp, tp, d = x_prompt.shape
    bs, ts, _ = x_sample.shape
    depth = w_in.shape[0]
    heads, dk, dv = state_delta.shape[3:]
    qk = heads * dk
    vw = heads * dv
    lru_w = state_lru.shape[-1]
    n_experts = w_router.shape[-1]
    n_p, n_s = bp * tp, bs * ts
    n = n_p + n_s
    cap_p = 2 * n_p // n_experts
    cap_s = 2 * n_s // n_experts
    assert dk == LANES and dv == LANES and tp % CHUNK == 0 and ts % CHUNK == 0

    xs = x_sample + _pos_embed_2d(ts, d)[None]
    x = jnp.concatenate([x_prompt.reshape(n_p, d), xs.reshape(n_s, d)], axis=0)

    cond8 = jnp.zeros((8, d), F32).at[0].set(c_ctx).at[1:1 + bs].set(c)
    mod = _modulation(cond8, w_mod, b_mod)
    mod = mod.reshape(depth, 8, 6, 1, d)

    n_small = 2 * N_DIR * heads
    c0 = 2 * qk + 2 * vw
    w_main, w_ba = _w_in_prep(w_in, c0, n_small)
    lx_blk = c0 // LANES
    ly_blk = lx_blk + lru_w // LANES
    ga_blk = (c0 + 2 * lru_w) // d
    gb_blk = ga_blk + 1
    w_dn_b = w_dn_out.astype(BF16)
    w_lru_b = w_lru_out.astype(BF16)
    w_o_b = w_o.astype(BF16)
    w_r_b = jnp.pad(w_router, ((0, 0), (0, 0), (0, LANES - n_experts))).astype(BF16)
    wa_b = lru_wa.astype(BF16)
    wx_b = lru_wx.astype(BF16)
    lane_pad = lambda a: jnp.pad(a.reshape(depth, 1, N_DIR * heads), ((0, 0), (0, 0), (0, LANES - N_DIR * heads)))
    al_v = lane_pad(dn_a_log)
    dt_v = lane_pad(dn_dt_bias)
    norm1_3 = norm1_g.reshape(depth, 1, d)
    norm2_3 = norm2_g.reshape(depth, 1, d)
    dn_norm_3 = dn_norm_g.reshape(depth, 1, dv)
    conv_lru_b3 = conv_lru_b.reshape(depth, 1, lru_w)

    tm_mg = 512
    tiles_p_mg = n_p // tm_mg
    per_seq = ts // tm_mg
    row_mg = lambda i: jnp.where(i < tiles_p_mg, 0, (i - tiles_p_mg) // per_seq + 1)
    tiles_p_cb = n_p // ROUTE_TB
    per_seq_cb = ts // ROUTE_TB
    row_cb = lambda i: jnp.where(i < tiles_p_cb, 0, (i - tiles_p_cb) // per_seq_cb + 1)
    assert n_p % tm_mg == 0 and ts % tm_mg == 0 and n_p % ROUTE_TB == 0 and ts % ROUTE_TB == 0
    assert n_experts <= LANES and d % LANES == 0

    sd_acc = jnp.zeros((bp, depth, N_DIR, heads, dk, dv), F32)
    sl_out = []
    for l in range(depth):
        cols, ba = _in_proj(l, x, norm1_3, mod, w_main, w_ba, row_mg, tm_mg)
        o_p, sd_acc = _gdn_seq(l, cols, ba, conv_qkv, al_v, dt_v, dn_norm_3, None, 0, bp, tp, heads, dk,
                               s_acc=sd_acc)
        o_s, _ = _gdn_seq(l, cols, ba, conv_qkv, al_v, dt_v, dn_norm_3, state_delta, n_p, bs, ts, heads, dk)
        y_p, sl_p = _lru(l, cols, conv_lru_w, conv_lru_b3, wa_b, wx_b, lru_ba, lru_bx, lru_lambda, None,
                         0, bp, tp, lx_blk, ly_blk)
        y_s, _ = _lru(l, cols, conv_lru_w, conv_lru_b3, wa_b, wx_b, lru_ba, lru_bx, lru_lambda, state_lru,
                      n_p, bs, ts, lx_blk, ly_blk)
        x1, hn_aug, afft = _merge(l, o_p, o_s, y_p, y_s, cols, x, mod, norm2_3, w_dn_b, w_lru_b, w_o_b, w_r_b,
                                  row_mg, tm_mg, ga_blk, gb_blk, n_experts)
        thr_need = _thresholds(afft, ((0, n_p, cap_p), (n_p, n, cap_s)))
        slot, off, cnt, lst = _route(hn_aug, thr_need, n_p, cap_p, n_experts)
        nblk = n // ROUTE_TB
        off = off.reshape(nblk, LANES)
        cnt = cnt.reshape(nblk, LANES)
        xe_p = _gather(off, cnt, lst, hn_aug, 0, n_p, 0, cap_p)
        xe_s = _gather(off, cnt, lst, hn_aug, n_p, n_s, cap_p, cap_s)
        ye = _ffn(l, xe_p, xe_s, w_gate, w_up, w_down, 512)
        x = _combine(l, off, cnt, ye, slot, x1, mod, final_g, row_cb, n_experts, l == depth - 1)
        sl_out.append(sl_p)

    y = x
    y_prompt = y[:n_p].reshape(bp, tp, d)
    y_sample = y[n_p:].reshape(bs, ts, d)
    return (y_prompt, y_sample, sd_acc, jnp.stack(sl_out, axis=1))
```

```python
import functools
import math

import jax
import jax.numpy as jnp
from jax import lax
from jax.experimental import pallas as pl
from jax.experimental.pallas import tpu as pltpu

F32 = jnp.float32
BF16 = jnp.bfloat16

EPS = 1e-6
CHUNK = 64
SUB = 8
GDN_BATCH = 32
CONV_LEFT = 2
CONV_W = 4
LRU_C = 8.0
LRU_STEP_ELEMS = 512 * 1024
N_DIR = 2
GRID_W = 64
LANES = 128
SUBLANES = 8
VMEM_LIMIT = 56 * 1024 * 1024


def _cparams(sem):
    return pltpu.CompilerParams(dimension_semantics=sem, vmem_limit_bytes=VMEM_LIMIT)


def _bdot(a, b):
    return jnp.dot(a.astype(BF16), b.astype(BF16), preferred_element_type=F32)


def _mod_kernel(c_ref, w_ref, b_ref, o_ref):
    c = c_ref[...]
    c = c * jax.nn.sigmoid(c)
    o_ref[...] = _bdot(c, w_ref[...]) + b_ref[...]


def _modulation(cond8, w_mod, b_mod):
    depth, d, n6 = w_mod.shape
    tn = 1536
    return pl.pallas_call(
        _mod_kernel,
        grid=(depth, n6 // tn),
        in_specs=[
            pl.BlockSpec((8, d), lambda l, j: (0, 0)),
            pl.BlockSpec((None, d, tn), lambda l, j: (l, 0, j)),
            pl.BlockSpec((None, 1, tn), lambda l, j: (l, 0, j)),
        ],
        out_specs=pl.BlockSpec((None, 8, tn), lambda l, j: (l, 0, j)),
        out_shape=jax.ShapeDtypeStruct((depth, 8, n6), F32),
        compiler_params=_cparams(("parallel", "parallel")),
        name="modulation",
    )(cond8, w_mod, b_mod.reshape(depth, 1, n6))


def _w_in_prep_kernel(w_ref, main_ref, ba_ref, *, c0, n_small):
    w = w_ref[...]
    rows = w.shape[0]
    main_ref[...] = jnp.concatenate([w[:, :c0], w[:, c0 + n_small:]], axis=1).astype(BF16)
    ba_ref[...] = jnp.concatenate(
        [w[:, c0:c0 + n_small], jnp.zeros((rows, LANES - n_small), F32)], axis=1).astype(BF16)


def _w_in_prep(w_in, c0, n_small):
    depth, d, ncol = w_in.shape
    tk = 256
    return pl.pallas_call(
        functools.partial(_w_in_prep_kernel, c0=c0, n_small=n_small),
        grid=(depth, d // tk),
        in_specs=[pl.BlockSpec((None, tk, ncol), lambda l, i: (l, i, 0))],
        out_specs=[
            pl.BlockSpec((None, tk, ncol - n_small), lambda l, i: (l, i, 0)),
            pl.BlockSpec((None, tk, LANES), lambda l, i: (l, i, 0)),
        ],
        out_shape=[
            jax.ShapeDtypeStruct((depth, d, ncol - n_small), BF16),
            jax.ShapeDtypeStruct((depth, d, LANES), BF16),
        ],
        compiler_params=_cparams(("parallel", "parallel")),
        name="w_in_prep",
    )(w_in)


IN_PROJ_TN = 1536


def _in_proj_kernel(x_ref, g_ref, sh_ref, sc_ref, w_ref, wba_ref, o_ref, oba_ref):
    x = x_ref[...]
    y = x * lax.rsqrt(jnp.mean(x * x, axis=-1, keepdims=True) + EPS) * g_ref[...]
    hn = (y * (1.0 + sc_ref[...]) + sh_ref[...]).astype(BF16)
    oba_ref[...] = jnp.dot(hn, wba_ref[...], preferred_element_type=F32)
    for c in range(0, o_ref.shape[1], IN_PROJ_TN):
        o_ref[:, c:c + IN_PROJ_TN] = jnp.dot(hn, w_ref[:, c:c + IN_PROJ_TN], preferred_element_type=F32)


def _in_proj(l, x, norm_g, mod5, w_main, w_ba, row_of_tile, tm):
    n, d = x.shape
    ncols = w_main.shape[2]
    assert ncols % IN_PROJ_TN == 0
    resident = pl.Buffered(1)
    return pl.pallas_call(
        _in_proj_kernel,
        grid=(n // tm,),
        in_specs=[
            pl.BlockSpec((tm, d), lambda i: (i, 0)),
            pl.BlockSpec((None, 1, d), lambda i: (l, 0, 0)),
            pl.BlockSpec((None, None, None, 1, d), lambda i: (l, row_of_tile(i), 0, 0, 0)),
            pl.BlockSpec((None, None, None, 1, d), lambda i: (l, row_of_tile(i), 1, 0, 0)),
            pl.BlockSpec((None, d, ncols), lambda i: (l, 0, 0), pipeline_mode=resident),
            pl.BlockSpec((None, d, LANES), lambda i: (l, 0, 0), pipeline_mode=resident),
        ],
        out_specs=[
            pl.BlockSpec((tm, ncols), lambda i: (i, 0)),
            pl.BlockSpec((tm, LANES), lambda i: (i, 0)),
        ],
        out_shape=[
            jax.ShapeDtypeStruct((n, ncols), F32),
            jax.ShapeDtypeStruct((n, LANES), F32),
        ],
        compiler_params=_cparams(("parallel",)),
        name="in_proj",
    )(x, norm_g, mod5, mod5, w_main, w_ba)


def _conv_rows(x, w, row):
    t = x.shape[0]
    acc = x * w[CONV_LEFT:CONV_LEFT + 1, :]
    for j in range(CONV_W):
        off = j - CONV_LEFT
        if off == 0:
            continue
        xs = pltpu.roll(x, (-off) % t, axis=0)
        valid = (row + off >= 0) & (row + off < t)
        acc = acc + jnp.where(valid, xs, 0.0) * w[j:j + 1, :]
    return acc


def _softplus(x):
    return jnp.maximum(x, 0.0) + jnp.log1p(jnp.exp(-jnp.abs(x)))


def _bmm(a, b):
    return jnp.einsum('nij,njk->nik', a.astype(BF16), b.astype(BF16), preferred_element_type=F32)


def _bmm_nt(a, b, precision=None):
    return jnp.einsum('nid,njd->nij', a, b, preferred_element_type=F32, precision=precision)


def _unit_tri_inverse(a, ii, jj):
    def same(b):
        return (ii // b) == (jj // b)

    eye = (ii == jj).astype(F32)
    d1 = jnp.where(same(SUB), a, 0.0)
    d2 = _bmm(d1, d1)
    d4 = _bmm(d2, d2)
    x = eye - d1
    x = x + _bmm(x, d2)
    x = x + _bmm(x, d4)
    b = SUB
    while b < CHUNK:
        o = jnp.where(same(2 * b) & jnp.logical_not(same(b)), a, 0.0)
        x = x - _bmm(_bmm(x, o), x)
        b *= 2
    return x


def _gdn_seq_kernel(*refs, t, has_s0, has_acc, heads, dk, hg):
    n_st = N_DIR * heads
    n_in = 12 + int(has_s0) + int(has_acc)
    ins, outs = refs[:n_in], refs[n_in:]
    (q_ref, k_ref, v_ref, z_ref, ba_ref, cq_ref, ck_ref, cv_ref, al_ref, dt_ref, ng_ref, oh_ref) = ins[:12]
    s0_ref = ins[12] if has_s0 else None
    o_ref, s_ref, x_scr, at_scr, kq_scr, kd_scr, vb_scr, cd_scr, o_scr = outs[:9]
    st_scrs = outs[9:9 + n_st]
    n = t // CHUNK
    row = lax.broadcasted_iota(jnp.int32, (t, LANES), 0)
    pos = row % CHUNK
    ii = lax.broadcasted_iota(jnp.int32, (CHUNK, CHUNK), 0)
    jj = lax.broadcasted_iota(jnp.int32, (CHUNK, CHUNK), 1)
    ba = ba_ref[...]
    al = al_ref[...]
    dtb = dt_ref[...]

    def conv_silu(x_ref, w_ref, h):
        y = _conv_rows(x_ref[:, h * LANES:(h + 1) * LANES], w_ref[:, h * LANES:(h + 1) * LANES], row)
        return y * jax.nn.sigmoid(y)

    def l2n(x):
        return x * lax.rsqrt(jnp.sum(x * x, axis=-1, keepdims=True) + EPS)

    nb = hg * n
    ii3 = lax.broadcasted_iota(jnp.int32, (N_DIR * nb, CHUNK, CHUNK), 1)
    jj3 = lax.broadcasted_iota(jnp.int32, (N_DIR * nb, CHUNK, CHUNK), 2)
    fwd = lax.broadcasted_iota(jnp.int32, (N_DIR * nb, CHUNK, CHUNK), 0) < nb
    ahead = jnp.where(fwd, ii3 - jj3, jj3 - ii3)
    tri = ahead >= 0
    strict = ahead > 0
    onehot0 = jnp.broadcast_to(oh_ref[...][None], (N_DIR * nb, CHUNK, LANES))

    for h0 in range(0, heads, hg):
        hs = range(h0, h0 + hg)
        as3 = lambda x: x.reshape(n, CHUNK, LANES)
        q3 = jnp.concatenate([as3(l2n(conv_silu(q_ref, cq_ref, h)) * (dk ** -0.5)) for h in hs], axis=0)
        k3 = jnp.concatenate([as3(l2n(conv_silu(k_ref, ck_ref, h))) for h in hs], axis=0)
        v3 = jnp.concatenate([as3(conv_silu(v_ref, cv_ref, h)) for h in hs], axis=0)
        k3b = k3.astype(BF16)
        gram = _bmm_nt(k3b, k3b)
        qk = _bmm_nt(q3.astype(BF16), k3b)
        decs, tots, betas = [], [], []
        for d in range(N_DIR):
            for h in hs:
                lb = d * heads + h
                la = 2 * heads + lb
                beta = jax.nn.sigmoid(ba[:, lb:lb + 1])
                g = -jnp.exp(al[:, lb:lb + 1]) * _softplus(ba[:, la:la + 1] + dtb[:, lb:lb + 1])
                dec = jnp.broadcast_to(g, (t, LANES))
                step = 1
                while step < CHUNK:
                    if d == 0:
                        sh = pltpu.roll(dec, step, axis=0)
                        dec = dec + jnp.where(pos >= step, sh, 0.0)
                    else:
                        sh = pltpu.roll(dec, t - step, axis=0)
                        dec = dec + jnp.where(pos < CHUNK - step, sh, 0.0)
                    step *= 2
                dec3 = as3(dec)
                decs.append(dec3)
                tots.append(jnp.broadcast_to(dec3[:, CHUNK - 1:CHUNK, :] if d == 0 else dec3[:, 0:1, :],
                                             (n, CHUNK, LANES)))
                betas.append(as3(jnp.broadcast_to(beta, (t, LANES))))
        dec3 = jnp.concatenate(decs, axis=0)
        tot3 = jnp.concatenate(tots, axis=0)
        beta3 = jnp.concatenate(betas, axis=0)
        both = lambda x: jnp.concatenate([x, x], axis=0)
        dec_row = _bmm_nt(onehot0, dec3, precision=lax.Precision.HIGHEST)
        dec_col = dec3[:, :, :CHUNK]
        gamma = jnp.where(tri, jnp.exp(jnp.where(tri, dec_col - dec_row, 0.0)), 0.0)
        a = jnp.where(strict, both(gram) * gamma, 0.0) * beta3[:, :, :CHUNK]
        xinv = _unit_tri_inverse(a, ii, jj).astype(BF16)
        attn = jnp.where(tri, both(qk) * gamma, 0.0).astype(BF16)
        edec = jnp.exp(dec3)
        k32, q32, v32 = both(k3), both(q3), both(v3)
        kbd = (k32 * (beta3 * edec)).astype(BF16)
        qd = (q32 * edec).astype(BF16)
        kd = jnp.swapaxes(k32 * jnp.exp(tot3 - dec3), 1, 2).astype(BF16)
        vb = v32 * beta3
        cd = jnp.exp(tot3[:, :SUBLANES, :])
        for d in range(N_DIR):
            src = slice(d * nb, (d + 1) * nb)
            dst = pl.ds((d * heads + h0) * n, nb)
            x_scr[dst] = xinv[src]
            at_scr[dst] = attn[src]
            kq_scr[dst, :CHUNK, :] = kbd[src]
            kq_scr[dst, CHUNK:, :] = qd[src]
            kd_scr[dst] = kd[src]
            vb_scr[dst] = vb[src]
            cd_scr[dst] = cd[src]

    for d in range(N_DIR):
        for h in range(heads):
            if has_s0:
                st_scrs[d * heads + h][...] = s0_ref[d, h]
            else:
                st_scrs[d * heads + h][...] = jnp.zeros((dk, LANES), F32)

    def scan(it, carry):
        chains = [(d * heads + h, (d * heads + h) * n + (it if d == 0 else n - 1 - it))
                  for h in range(heads) for d in range(N_DIR)]
        ss = [st_scrs[i][...] for i, _ in chains]
        rqs = [jnp.dot(kq_scr[b], s.astype(BF16), preferred_element_type=F32)
               for (_, b), s in zip(chains, ss)]
        vns = [jnp.dot(x_scr[b], (vb_scr[b] - rq[:CHUNK]).astype(BF16), preferred_element_type=F32).astype(BF16)
               for (_, b), rq in zip(chains, rqs)]
        for (i, b), s, rq, vn in zip(chains, ss, rqs, vns):
            o_scr[b] = rq[CHUNK:] + jnp.dot(at_scr[b], vn, preferred_element_type=F32)
            st_scrs[i][...] = s * cd_scr[b][0:1, :] + jnp.dot(kd_scr[b], vn, preferred_element_type=F32)
        return carry

    lax.fori_loop(0, n, scan, 0)

    for h in range(heads):
        for d in range(N_DIR):
            s_ref[d, h] = st_scrs[d * heads + h][...]
        of = o_scr[pl.ds(h * n, n)]
        ob = o_scr[pl.ds((heads + h) * n, n)]
        o = (of + ob).reshape(t, LANES)
        o = o * lax.rsqrt(jnp.mean(o * o, axis=-1, keepdims=True) + EPS) * ng_ref[...]
        z = z_ref[:, h * LANES:(h + 1) * LANES]
        o_ref[:, h * LANES:(h + 1) * LANES] = (o * (z * jax.nn.sigmoid(z))).astype(o_ref.dtype)


def _gdn_seq(l, cols, ba, conv_qkv, al, dtb, norm_g, s0, row0, nb, t, heads, dk, s_acc=None):
    blk0 = row0 // t
    has_s0 = s0 is not None
    has_acc = s_acc is not None
    hw = heads * LANES
    n = t // CHUNK
    col = lambda off: pl.BlockSpec((t, hw), lambda b: (blk0 + b, off))
    cw = lambda off: pl.BlockSpec((None, CONV_W, hw), lambda b: (l, 0, off))
    vec = pl.BlockSpec((None, 1, LANES), lambda b: (l, 0, 0))
    in_specs = [
        col(0), col(1), col(2), col(3),
        pl.BlockSpec((t, LANES), lambda b: (blk0 + b, 0)),
        cw(0), cw(1), cw(2),
        vec, vec, vec,
        pl.BlockSpec((CHUNK, LANES), lambda b: (0, 0)),
    ]
    onehot0 = jnp.zeros((CHUNK, LANES), F32).at[:, 0].set(1.0)
    args = [cols, cols, cols, cols, ba, conv_qkv, conv_qkv, conv_qkv, al, dtb, norm_g, onehot0]
    if has_s0:
        in_specs.append(pl.BlockSpec((None, None, N_DIR, heads, dk, LANES), lambda b: (b, l, 0, 0, 0, 0)))
        args.append(s0)
    if has_acc:
        in_specs.append(pl.BlockSpec(memory_space=pl.ANY))
        args.append(s_acc)
        state_spec = pl.BlockSpec((None, None, N_DIR, heads, dk, LANES), lambda b: (b, l, 0, 0, 0, 0))
        state_shape = jax.ShapeDtypeStruct(s_acc.shape, F32)
        aliases = {len(args) - 1: 1}
    else:
        state_spec = pl.BlockSpec((None, N_DIR, heads, dk, LANES), lambda b: (b, 0, 0, 0, 0))
        state_shape = jax.ShapeDtypeStruct((nb, N_DIR, heads, dk, LANES), F32)
        aliases = {}
    per = (N_DIR * heads * n,)
    hg = max(1, min(heads, GDN_BATCH // (N_DIR * n)))
    assert heads % hg == 0
    return pl.pallas_call(
        functools.partial(_gdn_seq_kernel, t=t, has_s0=has_s0, has_acc=has_acc, heads=heads, dk=dk, hg=hg),
        grid=(nb,),
        in_specs=in_specs,
        out_specs=[pl.BlockSpec((t, hw), lambda b: (b, 0)), state_spec],
        out_shape=[jax.ShapeDtypeStruct((nb * t, hw), BF16), state_shape],
        input_output_aliases=aliases,
        scratch_shapes=[
            pltpu.VMEM(per + (CHUNK, CHUNK), BF16),
            pltpu.VMEM(per + (CHUNK, CHUNK), BF16),
            pltpu.VMEM(per + (2 * CHUNK, LANES), BF16),
            pltpu.VMEM(per + (dk, CHUNK), BF16),
            pltpu.VMEM(per + (CHUNK, LANES), F32),
            pltpu.VMEM(per + (SUBLANES, LANES), F32),
            pltpu.VMEM(per + (CHUNK, LANES), F32),
        ] + [pltpu.VMEM((dk, LANES), F32)] * (N_DIR * heads),
        compiler_params=_cparams(("parallel",)),
        name=f"gdn_t{t}",
    )(*args)


def _lru_kernel(*refs, t, has_h0):
    if has_h0:
        (lx_ref, ly_ref, cw_ref, cb_ref, wa_ref, wx_ref, ba_ref, bx_ref, lam_ref, h0_ref,
         y_ref, last_ref, a_scr, b_scr, h_scr) = refs
    else:
        (lx_ref, ly_ref, cw_ref, cb_ref, wa_ref, wx_ref, ba_ref, bx_ref, lam_ref,
         y_ref, last_ref, a_scr, b_scr, h_scr) = refs
        h0_ref = None
    nblk = t // SUBLANES
    wl = lx_ref.shape[1]
    row = lax.broadcasted_iota(jnp.int32, (t, wl), 0)
    sub = lax.broadcasted_iota(jnp.int32, (nblk, SUBLANES, wl), 1)
    u = _conv_rows(lx_ref[...], cw_ref[...], row) + cb_ref[...]
    ub = u.astype(BF16)

    def block_diag(w_ref, d):
        return jnp.concatenate(
            [jnp.dot(ub[:, c * LANES:(c + 1) * LANES], w_ref[d, c], preferred_element_type=F32)
             for c in range(wl // LANES)], axis=1)

    for d in range(N_DIR):
        r = jax.nn.sigmoid(block_diag(wa_ref, d) + ba_ref[d:d + 1, :])
        i = jax.nn.sigmoid(block_diag(wx_ref, d) + bx_ref[d:d + 1, :])
        log_a = (LRU_C * r) * (-_softplus(-lam_ref[d:d + 1, :]))
        a = jnp.exp(log_a)
        b = jnp.sqrt(1.0 - a * a) * (i * u)
        a3 = a.reshape(nblk, SUBLANES, wl)
        b3 = b.reshape(nblk, SUBLANES, wl)
        step = 1
        while step < SUBLANES:
            if d == 0:
                a_s = pltpu.roll(a3, step, axis=1)
                b_s = pltpu.roll(b3, step, axis=1)
                m = sub >= step
            else:
                a_s = pltpu.roll(a3, SUBLANES - step, axis=1)
                b_s = pltpu.roll(b3, SUBLANES - step, axis=1)
                m = sub < SUBLANES - step
            b3 = b3 + a3 * jnp.where(m, b_s, 0.0)
            a3 = a3 * jnp.where(m, a_s, 1.0)
            step *= 2
        a_scr[d] = a3
        b_scr[d] = b3

    if has_h0:
        h_init = (jnp.broadcast_to(h0_ref[0:1, :], (SUBLANES, wl)),
                  jnp.broadcast_to(h0_ref[1:2, :], (SUBLANES, wl)))
    else:
        h_init = (jnp.zeros((SUBLANES, wl), F32), jnp.zeros((SUBLANES, wl), F32))

    def body(it, carry):
        hf, hb = carry
        kf = it
        kb = nblk - 1 - it
        new_f = b_scr[0, kf] + a_scr[0, kf] * hf
        new_b = b_scr[1, kb] + a_scr[1, kb] * hb
        h_scr[0, kf] = new_f
        h_scr[1, kb] = new_b
        hf = jnp.broadcast_to(new_f[SUBLANES - 1:SUBLANES, :], (SUBLANES, wl))
        hb = jnp.broadcast_to(new_b[0:1, :], (SUBLANES, wl))
        return hf, hb

    hf, hb = lax.fori_loop(0, nblk, body, h_init, unroll=4)
    last_ref[0:1, :] = hf[0:1, :]
    last_ref[1:2, :] = hb[0:1, :]
    rec = (h_scr[0] + h_scr[1]).reshape(t, wl)
    y_ref[...] = (jax.nn.gelu(ly_ref[...]) * rec).astype(y_ref.dtype)


def _lru(l, cols, conv_w, conv_b, wa, wx, ba, bx, lam, h0, row0, nb, t, lx_blk, ly_blk):
    nblocks = wa.shape[2]
    w = nblocks * LANES
    cg = max(1, min(nblocks, LRU_STEP_ELEMS // (t * LANES)))
    wl = cg * LANES
    blk0 = row0 // t
    has_h0 = h0 is not None
    in_specs = [
        pl.BlockSpec((t, wl), lambda b, c: (blk0 + b, lx_blk // cg + c)),
        pl.BlockSpec((t, wl), lambda b, c: (blk0 + b, ly_blk // cg + c)),
        pl.BlockSpec((None, CONV_W, wl), lambda b, c: (l, 0, c)),
        pl.BlockSpec((None, 1, wl), lambda b, c: (l, 0, c)),
        pl.BlockSpec((None, N_DIR, cg, LANES, LANES), lambda b, c: (l, 0, c, 0, 0)),
        pl.BlockSpec((None, N_DIR, cg, LANES, LANES), lambda b, c: (l, 0, c, 0, 0)),
        pl.BlockSpec((None, N_DIR, wl), lambda b, c: (l, 0, c)),
        pl.BlockSpec((None, N_DIR, wl), lambda b, c: (l, 0, c)),
        pl.BlockSpec((None, N_DIR, wl), lambda b, c: (l, 0, c)),
    ]
    args = [cols, cols, conv_w, conv_b, wa, wx, ba, bx, lam]
    if has_h0:
        in_specs.append(pl.BlockSpec((None, None, N_DIR, wl), lambda b, c: (b, l, 0, c)))
        args.append(h0)
    nblk = t // SUBLANES
    assert lx_blk % cg == 0 and ly_blk % cg == 0 and nblocks % cg == 0
    return pl.pallas_call(
        functools.partial(_lru_kernel, t=t, has_h0=has_h0),
        grid=(nb, nblocks // cg),
        in_specs=in_specs,
        out_specs=[
            pl.BlockSpec((t, wl), lambda b, c: (b, c)),
            pl.BlockSpec((None, N_DIR, wl), lambda b, c: (b, 0, c)),
        ],
        out_shape=[
            jax.ShapeDtypeStruct((nb * t, w), BF16),
            jax.ShapeDtypeStruct((nb, N_DIR, w), F32),
        ],
        scratch_shapes=[pltpu.VMEM((N_DIR, nblk, SUBLANES, wl), F32)] * 3,
        compiler_params=_cparams(("parallel", "parallel")),
        name=f"lru_t{t}",
    )(*args)


MERGE_PARTS = 2


def _merge_kernel(oap_ref, oas_ref, obp_ref, obs_ref, ga_ref, gb_ref, x_ref, g1_ref, sh2_ref, sc2_ref,
                  n2_ref, wdn_ref, wlru_ref, wo_ref, wr_ref, x1_ref, hn_ref, afft_ref, *, n_experts, tiles_p):
    d = x_ref.shape[1]
    tm = x_ref.shape[0]
    is_p = pl.program_id(0) < tiles_p
    parts = [slice(r, r + tm // MERGE_PARTS) for r in range(0, tm, tm // MERGE_PARTS)]
    yas = [jnp.dot(jnp.where(is_p, oap_ref[p, :], oas_ref[p, :]), wdn_ref[...], preferred_element_type=F32)
           for p in parts]
    ybs = [jnp.dot(jnp.where(is_p, obp_ref[p, :], obs_ref[p, :]), wlru_ref[...], preferred_element_type=F32)
           for p in parts]
    mixes = [(jax.nn.sigmoid(ga_ref[p, :]) * ya + jax.nn.sigmoid(gb_ref[p, :]) * yb).astype(BF16)
             for p, ya, yb in zip(parts, yas, ybs)]
    outs = [jnp.dot(m, wo_ref[...], preferred_element_type=F32) for m in mixes]
    hns = []
    for p, mo in zip(parts, outs):
        x1 = x_ref[p, :] + g1_ref[...] * mo
        x1_ref[p, :] = x1
        y = x1 * lax.rsqrt(jnp.mean(x1 * x1, axis=-1, keepdims=True) + EPS) * n2_ref[...]
        hns.append(y * (1.0 + sc2_ref[...]) + sh2_ref[...])
    logit_parts = [jnp.dot(hn.astype(BF16), wr_ref[...], preferred_element_type=F32) for hn in hns]
    for p, hn, logits in zip(parts, hns, logit_parts):
        lane = lax.broadcasted_iota(jnp.int32, logits.shape, 1)
        logits = jnp.where(lane < n_experts, logits, -jnp.inf)
        e = jnp.exp(logits - jnp.max(logits, axis=-1, keepdims=True))
        aff = e / jnp.sum(e, axis=-1, keepdims=True)
        hn_ref[p, :d] = hn
        hn_ref[p, d:] = aff
        afft_ref[:, p] = aff.T[:n_experts, :]


def _merge(l, o_p, o_s, y_p, y_s, cols, x, mod5, norm2_g, w_dn, w_lru, w_o, w_r, row_of_tile, tm,
           ga_blk, gb_blk, n_experts):
    n, d = x.shape
    dv = o_p.shape[1]
    w = y_p.shape[1]
    tiles_p = o_p.shape[0] // tm
    modspec = lambda k: pl.BlockSpec((None, None, None, 1, d), lambda i: (l, row_of_tile(i), k, 0, 0))
    layer = lambda a: pl.BlockSpec((None,) + a.shape[1:], lambda i: (l,) + (0,) * (a.ndim - 1))
    p_tile = lambda width: pl.BlockSpec((tm, width), lambda i: (jnp.minimum(i, tiles_p - 1), 0))
    s_tile = lambda width: pl.BlockSpec((tm, width), lambda i: (jnp.maximum(i - tiles_p, 0), 0))
    return pl.pallas_call(
        functools.partial(_merge_kernel, n_experts=n_experts, tiles_p=tiles_p),
        grid=(n // tm,),
        in_specs=[
            p_tile(dv), s_tile(dv), p_tile(w), s_tile(w),
            pl.BlockSpec((tm, d), lambda i: (i, ga_blk)),
            pl.BlockSpec((tm, d), lambda i: (i, gb_blk)),
            pl.BlockSpec((tm, d), lambda i: (i, 0)),
            modspec(2), modspec(3), modspec(4),
            layer(norm2_g), layer(w_dn), layer(w_lru), layer(w_o), layer(w_r),
        ],
        out_specs=[
            pl.BlockSpec((tm, d), lambda i: (i, 0)),
            pl.BlockSpec((tm, d + LANES), lambda i: (i, 0)),
            pl.BlockSpec((n_experts, tm), lambda i: (0, i)),
        ],
        out_shape=[
            jax.ShapeDtypeStruct((n, d), F32),
            jax.ShapeDtypeStruct((n, d + LANES), F32),
            jax.ShapeDtypeStruct((n_experts, n), F32),
        ],
        compiler_params=_cparams(("parallel",)),
        name="merge",
    )(o_p, o_s, y_p, y_s, cols, cols, x, mod5, mod5, mod5, norm2_g, w_dn, w_lru, w_o, w_r)


def _ffn_kernel(xp_ref, xs_ref, wg_ref, wu_ref, wd_ref, o_ref, xb_scr):
    f = pl.program_id(1)
    cap_p = xp_ref.shape[0]
    d = xb_scr.shape[1]

    @pl.when(f == 0)
    def _():
        xb_scr[:cap_p, :] = xp_ref[:, :d].astype(BF16)
        xb_scr[cap_p:, :] = xs_ref[:, :d].astype(BF16)
        o_ref[...] = jnp.zeros_like(o_ref)

    x = xb_scr[...]
    g = jnp.dot(x, wg_ref[...].astype(BF16), preferred_element_type=F32)
    u = jnp.dot(x, wu_ref[...].astype(BF16), preferred_element_type=F32)
    hid = ((g * jax.nn.sigmoid(g)) * u).astype(BF16)
    o_ref[...] += jnp.dot(hid, wd_ref[...].astype(BF16), preferred_element_type=F32)

    @pl.when(f == pl.num_programs(1) - 1)
    def _():
        e = pl.program_id(0)
        for ref, r0 in ((xp_ref, 0), (xs_ref, cap_p)):
            aff = ref[:, d:]
            lane = lax.broadcasted_iota(jnp.int32, aff.shape, 1)
            gv = jnp.sum(jnp.where(lane == e, aff, 0.0), axis=1, keepdims=True)
            rows = ref.shape[0]
            o_ref[r0:r0 + rows, :] = o_ref[r0:r0 + rows, :] * gv


def _ffn(l, xe_p, xe_s, w_gate, w_up, w_down, tf):
    e, cap_p, da = xe_p.shape
    d = da - LANES
    cap_s = xe_s.shape[1]
    r = cap_p + cap_s
    ff = w_gate.shape[3]
    return pl.pallas_call(
        _ffn_kernel,
        grid=(e, ff // tf),
        in_specs=[
            pl.BlockSpec((None, cap_p, da), lambda i, f: (i, 0, 0)),
            pl.BlockSpec((None, cap_s, da), lambda i, f: (i, 0, 0)),
            pl.BlockSpec((None, None, d, tf), lambda i, f: (l, i, 0, f)),
            pl.BlockSpec((None, None, d, tf), lambda i, f: (l, i, 0, f)),
            pl.BlockSpec((None, None, tf, d), lambda i, f: (l, i, f, 0)),
        ],
        out_specs=pl.BlockSpec((None, r, d), lambda i, f: (i, 0, 0)),
        out_shape=jax.ShapeDtypeStruct((e, r, d), F32),
        scratch_shapes=[pltpu.VMEM((r, d), BF16)],
        compiler_params=_cparams(("parallel", "arbitrary")),
        name="expert_ffn",
    )(xe_p, xe_s, w_gate, w_up, w_down)


ROUTE_TB = 128


def _thr_kernel(afft_ref, o_ref, *, groups):
    n_e = afft_ref.shape[0]
    sub = lax.broadcasted_iota(jnp.int32, (n_e, LANES), 0)
    lane = lax.broadcasted_iota(jnp.int32, (n_e, LANES), 1)
    rows = []
    for lo, hi, cap in groups:
        a = afft_ref[:, lo:hi]
        above_all = 4.0

        def count_ge(v, a=a):
            return jnp.sum((a >= v).astype(F32), axis=1, keepdims=True)

        def count_gt(v, a=a):
            return jnp.sum((a > v).astype(F32), axis=1, keepdims=True)

        def bisect(i, lh, cap=cap, count_ge=count_ge):
            lo_v, hi_v = lh
            mid = 0.5 * (lo_v + hi_v)
            ok = count_ge(mid) >= cap
            return jnp.where(ok, mid, lo_v), jnp.where(ok, hi_v, mid)

        lo_v, _ = lax.fori_loop(0, 48, bisect, (jnp.zeros((n_e, 1), F32), jnp.full((n_e, 1), 2.0, F32)))
        thr = jnp.min(jnp.where(a >= lo_v, a, above_all), axis=1, keepdims=True)

        def not_done(thr, cap=cap, count_gt=count_gt):
            return jnp.max(count_gt(thr)) >= cap

        def step_up(thr, a=a, cap=cap, count_gt=count_gt):
            nxt = jnp.min(jnp.where(a > thr, a, above_all), axis=1, keepdims=True)
            return jnp.where(count_gt(thr) >= cap, nxt, thr)

        thr = lax.while_loop(not_done, step_up, thr)
        need = cap - count_gt(thr)
        for col in (thr, need):
            m = jnp.where(sub == lane, jnp.broadcast_to(col, (n_e, LANES)), 0.0)
            rows.append(jnp.sum(m, axis=0, keepdims=True))
    rows.append(jnp.zeros((SUBLANES - len(rows), LANES), F32))
    o_ref[...] = jnp.concatenate(rows, axis=0)


def _thresholds(afft, groups):
    n_e, n = afft.shape
    return pl.pallas_call(
        functools.partial(_thr_kernel, groups=groups),
        grid=(1,),
        in_specs=[pl.BlockSpec((n_e, n), lambda i: (0, 0))],
        out_specs=pl.BlockSpec((SUBLANES, LANES), lambda i: (0, 0)),
        out_shape=jax.ShapeDtypeStruct((SUBLANES, LANES), F32),
        compiler_params=_cparams(("arbitrary",)),
        name="route_thresholds",
    )(afft)


def _route_kernel(aff_ref, tn_ref, ls_ref, slot_ref, off_ref, cnt_ref, lst_ref,
                  run_eq, run_sel, *, nblk_p, cap_p, n_e):
    tb = ROUTE_TB
    j = pl.program_id(0)

    @pl.when(j == 0)
    def _():
        run_eq[...] = jnp.zeros_like(run_eq)
        run_sel[...] = jnp.zeros_like(run_sel)

    @pl.when(j == nblk_p)
    def _():
        run_eq[...] = jnp.zeros_like(run_eq)
        run_sel[...] = jnp.full_like(run_sel, float(cap_p))

    g = (j >= nblk_p).astype(jnp.int32)
    thr = tn_ref[pl.ds(2 * g, 1), :]
    need = tn_ref[pl.ds(2 * g + 1, 1), :]
    a = aff_ref[...]
    lane = lax.broadcasted_iota(jnp.int32, (tb, LANES), 1)
    valid = lane < n_e
    gt = (a > thr) & valid
    eq = (a == thr) & valid
    eqf = eq.astype(F32)
    ls = ls_ref[...]
    eq_rank = run_eq[...] + jnp.dot(ls, eqf.astype(BF16), preferred_element_type=F32)
    sel = gt | (eq & (eq_rank < need))
    self_ = sel.astype(F32)
    pos = jnp.dot(ls, self_.astype(BF16), preferred_element_type=F32)
    cnt = jnp.sum(self_, axis=0, keepdims=True)
    off = run_sel[...]
    run_eq[...] = run_eq[...] + jnp.sum(eqf, axis=0, keepdims=True)
    run_sel[...] = off + cnt
    slot_ref[...] = jnp.where(sel, off + pos, -1.0)
    off_ref[...] = off.astype(jnp.int32)
    cnt_ref[...] = cnt.astype(jnp.int32)

    tok = (lax.broadcasted_iota(jnp.int32, (tb, LANES), 0) + j * tb).astype(F32)
    lanef = lane.astype(F32)
    local = jnp.where(sel, pos, -1.0)
    for e in range(n_e):
        hit = jnp.broadcast_to(local[:, e:e + 1], (tb, LANES)) == lanef
        lst_ref[e] = jnp.sum(jnp.where(hit, tok, 0.0), axis=0, keepdims=True).astype(jnp.int32)


def _route(hn_aug, thr_need, n_p, cap_p, n_e):
    n = hn_aug.shape[0]
    aff_blk = hn_aug.shape[1] // LANES - 1
    tb = ROUTE_TB
    nblk = n // tb
    ls = jnp.tril(jnp.ones((tb, tb), F32), -1).astype(BF16)
    blk_row = pl.BlockSpec((None, 1, LANES), lambda j: (j, 0, 0))
    return pl.pallas_call(
        functools.partial(_route_kernel, nblk_p=n_p // tb, cap_p=cap_p, n_e=n_e),
        grid=(nblk,),
        in_specs=[
            pl.BlockSpec((tb, LANES), lambda j: (j, aff_blk)),
            pl.BlockSpec((SUBLANES, LANES), lambda j: (0, 0)),
            pl.BlockSpec((tb, tb), lambda j: (0, 0)),
        ],
        out_specs=[
            pl.BlockSpec((tb, LANES), lambda j: (j, 0)),
            blk_row,
            blk_row,
            pl.BlockSpec((n_e, None, 1, LANES), lambda j: (0, j, 0, 0)),
        ],
        out_shape=[
            jax.ShapeDtypeStruct((n, LANES), F32),
            jax.ShapeDtypeStruct((nblk, 1, LANES), jnp.int32),
            jax.ShapeDtypeStruct((nblk, 1, LANES), jnp.int32),
            jax.ShapeDtypeStruct((n_e, nblk, 1, LANES), jnp.int32),
        ],
        scratch_shapes=[pltpu.VMEM((1, LANES), F32), pltpu.VMEM((1, LANES), F32)],
        compiler_params=_cparams(("arbitrary",)),
        name="route_slots",
    )(hn_aug, thr_need, ls)


GATHER_UNROLL = 4


def _gather_kernel(off_ref, cnt_ref, lst_hbm, hn_hbm, o_ref, hn_scr, lst_smem, sem_h, sem_l,
                   *, row0, slot0, blk0, nblk_g):
    e = pl.program_id(0)
    load_lst = pltpu.make_async_copy(lst_hbm.at[e, pl.ds(blk0, nblk_g)], lst_smem, sem_l)
    load_lst.start()

    @pl.when(e == 0)
    def _():
        rows = hn_scr.shape[0]
        load = pltpu.make_async_copy(hn_hbm.at[pl.ds(row0, rows)], hn_scr, sem_h)
        load.start()
        load.wait()

    load_lst.wait()

    def block(jb, carry):
        c = cnt_ref[blk0 + jb, e]
        o = off_ref[blk0 + jb, e] - slot0

        def copy_row(q):
            t = lst_smem[jb, q] - row0
            o_ref[pl.ds(o + q, 1), :] = hn_scr[pl.ds(t, 1), :]

        def rows(qq, carry2):
            for u in range(GATHER_UNROLL):
                copy_row(qq * GATHER_UNROLL + u)
            return carry2

        def row(q, carry2):
            copy_row(q)
            return carry2

        full = c // GATHER_UNROLL
        lax.fori_loop(0, full, rows, 0)
        lax.fori_loop(full * GATHER_UNROLL, c, row, 0)
        return carry

    lax.fori_loop(0, nblk_g, block, 0)


def _gather(off, cnt, lst, hn, row0, rows, slot0, cap):
    n_e = lst.shape[0]
    width = hn.shape[1]
    blk0 = row0 // ROUTE_TB
    nblk_g = rows // ROUTE_TB
    grid_spec = pltpu.PrefetchScalarGridSpec(
        num_scalar_prefetch=2,
        grid=(n_e,),
        in_specs=[pl.BlockSpec(memory_space=pl.ANY), pl.BlockSpec(memory_space=pl.ANY)],
        out_specs=pl.BlockSpec((None, cap, width), lambda e, o_, c_: (e, 0, 0)),
        scratch_shapes=[
            pltpu.VMEM((rows, width), F32),
            pltpu.SMEM((nblk_g, LANES), jnp.int32),
            pltpu.SemaphoreType.DMA(()),
            pltpu.SemaphoreType.DMA(()),
        ],
    )
    return pl.pallas_call(
        functools.partial(_gather_kernel, row0=row0, slot0=slot0, blk0=blk0, nblk_g=nblk_g),
        grid_spec=grid_spec,
        out_shape=jax.ShapeDtypeStruct((n_e, cap, width), F32),
        compiler_params=_cparams(("arbitrary",)),
        name=f"gather_rows{rows}",
    )(off, cnt, lst.reshape(n_e, -1, LANES), hn)


COMB_CH = 32
COMB_GRP = 8
COMB_MAXCH = 80
COMB_DESC = 128


def _combine_kernel(off_ref, cnt_ref, y_hbm, slot_ref, x1_ref, g2_ref, fg_ref, o_ref,
                    buf, acc, desc, sems, *, n_e, final):
    tb = ROUTE_TB
    j = pl.program_id(0)
    nblk = pl.num_programs(0)
    r_total = y_hbm.shape[1]
    par = j % 2

    def issue(jj, p):
        s = jnp.int32(0)
        for e in range(n_e):
            o = off_ref[jj, e]
            c = cnt_ref[jj, e]
            st8 = (o // SUBLANES) * SUBLANES
            nch = jnp.where(c > 0, (o - st8 + c + COMB_CH - 1) // COMB_CH, 0)

            def one(k, s, e=e, st8=st8):
                lo_row = st8 + k * COMB_CH
                base = jnp.minimum(lo_row, r_total - COMB_CH)
                pltpu.make_async_copy(y_hbm.at[e, pl.ds(base, COMB_CH)], buf.at[p, s], sems.at[p]).start()
                desc[p, 0, s] = e
                desc[p, 1, s] = lo_row
                desc[p, 2, s] = base
                return s + 1

            s = lax.fori_loop(0, nch, one, s)
        desc[p, 3, 0] = s

    @pl.when(j == 0)
    def _():
        buf[...] = jnp.zeros_like(buf)

        def clear(i, carry):
            for p in range(2):
                for row in range(4):
                    desc[p, row, i] = 0
            return carry

        lax.fori_loop(0, COMB_DESC, clear, 0)
        issue(j, par)

    @pl.when(j + 1 < nblk)
    def _():
        issue(j + 1, 1 - par)

    n_ch = desc[par, 3, 0]

    def drain(k, carry):
        pltpu.make_async_copy(y_hbm.at[0, pl.ds(0, COMB_CH)], buf.at[par, k], sems.at[par]).wait()
        return carry

    lax.fori_loop(0, n_ch, drain, 0)

    slot = slot_ref[...]
    lane = lax.broadcasted_iota(jnp.int32, (tb, LANES), 1)
    lanef = lane.astype(F32)
    acc[...] = jnp.zeros_like(acc)

    per_lane = LANES // COMB_CH
    n_sub = COMB_GRP // per_lane

    def group(gi, carry):
        ebs = []
        for sub in range(n_sub):
            hit = jnp.zeros((tb, LANES), jnp.bool_)
            for c4 in range(per_lane):
                s = gi * COMB_GRP + sub * per_lane + c4
                e = desc[par, 0, s]
                lo_row = desc[par, 1, s].astype(F32)
                base = desc[par, 2, s].astype(F32)
                col = jnp.sum(jnp.where(lane == e, slot, 0.0), axis=1, keepdims=True)
                colb = jnp.broadcast_to(col, (tb, LANES))
                in_chunk = (lane >= c4 * COMB_CH) & (lane < (c4 + 1) * COMB_CH)
                match = (colb == base + (lanef - float(c4 * COMB_CH))) & (colb >= lo_row)
                hit = hit | (in_chunk & match & (s < n_ch))
            ebs.append(jnp.where(hit, 1.0, 0.0).astype(BF16))
        eb = jnp.concatenate(ebs, axis=1)
        yg = buf[par, pl.ds(gi * COMB_GRP, COMB_GRP)].reshape(COMB_GRP * COMB_CH, -1)
        y1 = yg.astype(BF16)
        r1 = yg - y1.astype(F32)
        y2 = r1.astype(BF16)
        y3 = (r1 - y2.astype(F32)).astype(BF16)
        acc[...] += jnp.dot(jnp.concatenate([eb, eb, eb], axis=1), jnp.concatenate([y1, y2, y3], axis=0),
                            preferred_element_type=F32)
        return carry

    lax.fori_loop(0, (n_ch + COMB_GRP - 1) // COMB_GRP, group, 0)

    x2 = x1_ref[...] + g2_ref[...] * acc[...]
    if final:
        x2 = x2 * lax.rsqrt(jnp.mean(x2 * x2, axis=-1, keepdims=True) + EPS) * fg_ref[...]
    o_ref[...] = x2


def _combine(l, off, cnt, ye, slot, x1, mod5, final_g, row_of_tile, n_e, final):
    n, d = x1.shape
    tb = ROUTE_TB
    assert LANES % COMB_CH == 0 and (COMB_GRP * COMB_CH) % LANES == 0 and COMB_MAXCH % COMB_GRP == 0
    assert COMB_DESC >= COMB_MAXCH >= n_e * -(-(tb + SUBLANES - 1) // COMB_CH)
    grid_spec = pltpu.PrefetchScalarGridSpec(
        num_scalar_prefetch=2,
        grid=(n // tb,),
        in_specs=[
            pl.BlockSpec(memory_space=pl.ANY),
            pl.BlockSpec((tb, LANES), lambda j, o_, c_: (j, 0)),
            pl.BlockSpec((tb, d), lambda j, o_, c_: (j, 0)),
            pl.BlockSpec((None, None, None, 1, d), lambda j, o_, c_: (l, row_of_tile(j), 5, 0, 0)),
            pl.BlockSpec((1, d), lambda j, o_, c_: (0, 0)),
        ],
        out_specs=pl.BlockSpec((tb, d), lambda j, o_, c_: (j, 0)),
        scratch_shapes=[
            pltpu.VMEM((2, COMB_MAXCH, COMB_CH, d), F32),
            pltpu.VMEM((tb, d), F32),
            pltpu.SMEM((2, 4, COMB_DESC), jnp.int32),
            pltpu.SemaphoreType.DMA((2,)),
        ],
    )
    return pl.pallas_call(
        functools.partial(_combine_kernel, n_e=n_e, final=final),
        grid_spec=grid_spec,
        out_shape=jax.ShapeDtypeStruct((n, d), F32),
        compiler_params=_cparams(("arbitrary",)),
        name="combine",
    )(off, cnt, ye, slot, x1, mod5, final_g.reshape(1, d))


def _pos_embed_2d(n_tokens, d_model):
    rows = n_tokens // GRID_W
    r = jnp.broadcast_to(jnp.arange(rows, dtype=F32)[:, None], (rows, GRID_W)).reshape(-1)
    col = jnp.broadcast_to(jnp.arange(GRID_W, dtype=F32)[None, :], (rows, GRID_W)).reshape(-1)
    quarter = d_model // 4
    freq = jnp.exp(-math.log(10000.0) * jnp.arange(quarter, dtype=F32) / quarter)
    ar = r[:, None] * freq
    ac = col[:, None] * freq
    return jnp.concatenate([jnp.sin(ar), jnp.cos(ar), jnp.sin(ac), jnp.cos(ac)], axis=-1)


def kernel(x_prompt, x_sample, state_delta, state_lru, c, c_ctx, norm1_g, w_mod, b_mod, w_in, conv_qkv, dn_a_log, dn_dt_bias, dn_norm_g, w_dn_out, conv_lru_w, conv_lru_b, lru_wa, lru_ba, lru_wx, lru_bx, lru_lambda, w_lru_out, w_o, norm2_g, w_router, w_gate, w_up, w_down, final_g):
    bp, tp, d = x_prompt.shape
    bs, ts, _ = x_sample.shape
    depth = w_in.shape[0]
    heads, dk, dv = state_delta.shape[3:]
    qk = heads * dk
    vw = heads * dv
    lru_w = state_lru.shape[-1]
    n_experts = w_router.shape[-1]
    n_p, n_s = bp * tp, bs * ts
    n = n_p + n_s
    cap_p = 2 * n_p // n_experts
    cap_s = 2 * n_s // n_experts
    assert dk == LANES and dv == LANES and tp % CHUNK == 0 and ts % CHUNK == 0

    xs = x_sample + _pos_embed_2d(ts, d)[None]
    x = jnp.concatenate([x_prompt.reshape(n_p, d), xs.reshape(n_s, d)], axis=0)

    cond8 = jnp.zeros((8, d), F32).at[0].set(c_ctx).at[1:1 + bs].set(c)
    mod = _modulation(cond8, w_mod, b_mod)
    mod = mod.reshape(depth, 8, 6, 1, d)

    n_small = 2 * N_DIR * heads
    c0 = 2 * qk + 2 * vw
    w_main, w_ba = _w_in_prep(w_in, c0, n_small)
    lx_blk = c0 // LANES
    ly_blk = lx_blk + lru_w // LANES
    ga_blk = (c0 + 2 * lru_w) // d
    gb_blk = ga_blk + 1
    w_dn_b = w_dn_out.astype(BF16)
    w_lru_b = w_lru_out.astype(BF16)
    w_o_b = w_o.astype(BF16)
    w_r_b = jnp.pad(w_router, ((0, 0), (0, 0), (0, LANES - n_experts))).astype(BF16)
    wa_b = lru_wa.astype(BF16)
    wx_b = lru_wx.astype(BF16)
    lane_pad = lambda a: jnp.pad(a.reshape(depth, 1, N_DIR * heads), ((0, 0), (0, 0), (0, LANES - N_DIR * heads)))
    al_v = lane_pad(dn_a_log)
    dt_v = lane_pad(dn_dt_bias)
    norm1_3 = norm1_g.reshape(depth, 1, d)
    norm2_3 = norm2_g.reshape(depth, 1, d)
    dn_norm_3 = dn_norm_g.reshape(depth, 1, dv)
    conv_lru_b3 = conv_lru_b.reshape(depth, 1, lru_w)

    tm_mg = 512
    tiles_p_mg = n_p // tm_mg
    per_seq = ts // tm_mg
    row_mg = lambda i: jnp.where(i < tiles_p_mg, 0, (i - tiles_p_mg) // per_seq + 1)
    tiles_p_cb = n_p // ROUTE_TB
    per_seq_cb = ts // ROUTE_TB
    row_cb = lambda i: jnp.where(i < tiles_p_cb, 0, (i - tiles_p_cb) // per_seq_cb + 1)
    assert n_p % tm_mg == 0 and ts % tm_mg == 0 and n_p % ROUTE_TB == 0 and ts % ROUTE_TB == 0
    assert n_experts <= LANES and d % LANES == 0

    sd_acc = jnp.zeros((bp, depth, N_DIR, heads, dk, dv), F32)
    sl_out = []
    for l in range(depth):
        cols, ba = _in_proj(l, x, norm1_3, mod, w_main, w_ba, row_mg, tm_mg)
        o_p, sd_acc = _gdn_seq(l, cols, ba, conv_qkv, al_v, dt_v, dn_norm_3, None, 0, bp, tp, heads, dk,
                               s_acc=sd_acc)
        o_s, _ = _gdn_seq(l, cols, ba, conv_qkv, al_v, dt_v, dn_norm_3, state_delta, n_p, bs, ts, heads, dk)
        y_p, sl_p = _lru(l, cols, conv_lru_w, conv_lru_b3, wa_b, wx_b, lru_ba, lru_bx, lru_lambda, None,
                         0, bp, tp, lx_blk, ly_blk)
        y_s, _ = _lru(l, cols, conv_lru_w, conv_lru_b3, wa_b, wx_b, lru_ba, lru_bx, lru_lambda, state_lru,
                      n_p, bs, ts, lx_blk, ly_blk)
        x1, hn_aug, afft = _merge(l, o_p, o_s, y_p, y_s, cols, x, mod, norm2_3, w_dn_b, w_lru_b, w_o_b, w_r_b,
                                  row_mg, tm_mg, ga_blk, gb_blk, n_experts)
        thr_need = _thresholds(afft, ((0, n_p, cap_p), (n_p, n, cap_s)))
        slot, off, cnt, lst = _route(hn_aug, thr_need, n_p, cap_p, n_experts)
        nblk = n // ROUTE_TB
        off = off.reshape(nblk, LANES)
        cnt = cnt.reshape(nblk, LANES)
        xe_p = _gather(off, cnt, lst, hn_aug, 0, n_p, 0, cap_p)
        xe_s = _gather(off, cnt, lst, hn_aug, n_p, n_s, cap_p, cap_s)
        ye = _ffn(l, xe_p, xe_s, w_gate, w_up, w_down, 512)
        x = _combine(l, off, cnt, ye, slot, x1, mod, final_g, row_cb, n_experts, l == depth - 1)
        sl_out.append(sl_p)

    y = x
    y_prompt = y[:n_p].reshape(bp, tp, d)
    y_sample = y[n_p:].reshape(bs, ts, d)
    return (y_prompt, y_sample, sd_acc, jnp.stack(sl_out, axis=1))
```

```python
import functools
import math

import jax
import jax.numpy as jnp
from jax import lax
from jax.experimental import pallas as pl
from jax.experimental.pallas import tpu as pltpu

F32 = jnp.float32
BF16 = jnp.bfloat16

EPS = 1e-6
CHUNK = 64
SUB = 8
GDN_BATCH = 32
CONV_LEFT = 2
CONV_W = 4
LRU_C = 8.0
LRU_STEP_ELEMS = 512 * 1024
N_DIR = 2
GRID_W = 64
LANES = 128
SUBLANES = 8
VMEM_LIMIT = 56 * 1024 * 1024


def _cparams(sem):
    return pltpu.CompilerParams(dimension_semantics=sem, vmem_limit_bytes=VMEM_LIMIT)


def _bdot(a, b):
    return jnp.dot(a.astype(BF16), b.astype(BF16), preferred_element_type=F32)


def _mod_kernel(c_ref, w_ref, b_ref, o_ref):
    c = c_ref[...]
    c = c * jax.nn.sigmoid(c)
    o_ref[...] = _bdot(c, w_ref[...]) + b_ref[...]


def _modulation(cond8, w_mod, b_mod):
    depth, d, n6 = w_mod.shape
    tn = 1536
    return pl.pallas_call(
        _mod_kernel,
        grid=(depth, n6 // tn),
        in_specs=[
            pl.BlockSpec((8, d), lambda l, j: (0, 0)),
            pl.BlockSpec((None, d, tn), lambda l, j: (l, 0, j)),
            pl.BlockSpec((None, 1, tn), lambda l, j: (l, 0, j)),
        ],
        out_specs=pl.BlockSpec((None, 8, tn), lambda l, j: (l, 0, j)),
        out_shape=jax.ShapeDtypeStruct((depth, 8, n6), F32),
        compiler_params=_cparams(("parallel", "parallel")),
        name="modulation",
    )(cond8, w_mod, b_mod.reshape(depth, 1, n6))


def _w_in_prep_kernel(w_ref, main_ref, ba_ref, *, c0, n_small):
    w = w_ref[...]
    rows = w.shape[0]
    main_ref[...] = jnp.concatenate([w[:, :c0], w[:, c0 + n_small:]], axis=1).astype(BF16)
    ba_ref[...] = jnp.concatenate(
        [w[:, c0:c0 + n_small], jnp.zeros((rows, LANES - n_small), F32)], axis=1).astype(BF16)


def _w_in_prep(w_in, c0, n_small):
    depth, d, ncol = w_in.shape
    tk = 256
    return pl.pallas_call(
        functools.partial(_w_in_prep_kernel, c0=c0, n_small=n_small),
        grid=(depth, d // tk),
        in_specs=[pl.BlockSpec((None, tk, ncol), lambda l, i: (l, i, 0))],
        out_specs=[
            pl.BlockSpec((None, tk, ncol - n_small), lambda l, i: (l, i, 0)),
            pl.BlockSpec((None, tk, LANES), lambda l, i: (l, i, 0)),
        ],
        out_shape=[
            jax.ShapeDtypeStruct((depth, d, ncol - n_small), BF16),
            jax.ShapeDtypeStruct((depth, d, LANES), BF16),
        ],
        compiler_params=_cparams(("parallel", "parallel")),
        name="w_in_prep",
    )(w_in)


IN_PROJ_TN = 1536


def _in_proj_kernel(x_ref, g_ref, sh_ref, sc_ref, w_ref, wba_ref, o_ref, oba_ref):
    x = x_ref[...]
    y = x * lax.rsqrt(jnp.mean(x * x, axis=-1, keepdims=True) + EPS) * g_ref[...]
    hn = (y * (1.0 + sc_ref[...]) + sh_ref[...]).astype(BF16)
    oba_ref[...] = jnp.dot(hn, wba_ref[...], preferred_element_type=F32)
    for c in range(0, o_ref.shape[1], IN_PROJ_TN):
        o_ref[:, c:c + IN_PROJ_TN] = jnp.dot(hn, w_ref[:, c:c + IN_PROJ_TN], preferred_element_type=F32)


def _in_proj(l, x, norm_g, mod5, w_main, w_ba, row_of_tile, tm):
    n, d = x.shape
    ncols = w_main.shape[2]
    assert ncols % IN_PROJ_TN == 0
    resident = pl.Buffered(1)
    return pl.pallas_call(
        _in_proj_kernel,
        grid=(n // tm,),
        in_specs=[
            pl.BlockSpec((tm, d), lambda i: (i, 0)),
            pl.BlockSpec((None, 1, d), lambda i: (l, 0, 0)),
            pl.BlockSpec((None, None, None, 1, d), lambda i: (l, row_of_tile(i), 0, 0, 0)),
            pl.BlockSpec((None, None, None, 1, d), lambda i: (l, row_of_tile(i), 1, 0, 0)),
            pl.BlockSpec((None, d, ncols), lambda i: (l, 0, 0), pipeline_mode=resident),
            pl.BlockSpec((None, d, LANES), lambda i: (l, 0, 0), pipeline_mode=resident),
        ],
        out_specs=[
            pl.BlockSpec((tm, ncols), lambda i: (i, 0)),
            pl.BlockSpec((tm, LANES), lambda i: (i, 0)),
        ],
        out_shape=[
            jax.ShapeDtypeStruct((n, ncols), F32),
            jax.ShapeDtypeStruct((n, LANES), F32),
        ],
        compiler_params=_cparams(("parallel",)),
        name="in_proj",
    )(x, norm_g, mod5, mod5, w_main, w_ba)


def _conv_rows(x, w, row):
    t = x.shape[0]
    acc = x * w[CONV_LEFT:CONV_LEFT + 1, :]
    for j in range(CONV_W):
        off = j - CONV_LEFT
        if off == 0:
            continue
        xs = pltpu.roll(x, (-off) % t, axis=0)
        valid = (row + off >= 0) & (row + off < t)
        acc = acc + jnp.where(valid, xs, 0.0) * w[j:j + 1, :]
    return acc


def _softplus(x):
    return jnp.maximum(x, 0.0) + jnp.log1p(jnp.exp(-jnp.abs(x)))


def _bmm(a, b):
    return jnp.einsum('nij,njk->nik', a.astype(BF16), b.astype(BF16), preferred_element_type=F32)


def _bmm_nt(a, b, precision=None):
    return jnp.einsum('nid,njd->nij', a, b, preferred_element_type=F32, precision=precision)


def _unit_tri_inverse(a, ii, jj):
    def same(b):
        return (ii // b) == (jj // b)

    eye = (ii == jj).astype(F32)
    d1 = jnp.where(same(SUB), a, 0.0)
    d2 = _bmm(d1, d1)
    d4 = _bmm(d2, d2)
    x = eye - d1
    x = x + _bmm(x, d2)
    x = x + _bmm(x, d4)
    b = SUB
    while b < CHUNK:
        o = jnp.where(same(2 * b) & jnp.logical_not(same(b)), a, 0.0)
        x = x - _bmm(_bmm(x, o), x)
        b *= 2
    return x


def _gdn_seq_kernel(*refs, t, has_s0, has_acc, heads, dk, hg):
    n_st = N_DIR * heads
    n_in = 12 + int(has_s0) + int(has_acc)
    ins, outs = refs[:n_in], refs[n_in:]
    (q_ref, k_ref, v_ref, z_ref, ba_ref, cq_ref, ck_ref, cv_ref, al_ref, dt_ref, ng_ref, oh_ref) = ins[:12]
    s0_ref = ins[12] if has_s0 else None
    o_ref, s_ref, x_scr, at_scr, kq_scr, kd_scr, vb_scr, cd_scr, o_scr = outs[:9]
    st_scrs = outs[9:9 + n_st]
    n = t // CHUNK
    row = lax.broadcasted_iota(jnp.int32, (t, LANES), 0)
    pos = row % CHUNK
    ii = lax.broadcasted_iota(jnp.int32, (CHUNK, CHUNK), 0)
    jj = lax.broadcasted_iota(jnp.int32, (CHUNK, CHUNK), 1)
    ba = ba_ref[...]
    al = al_ref[...]
    dtb = dt_ref[...]

    def conv_silu(x_ref, w_ref, h):
        y = _conv_rows(x_ref[:, h * LANES:(h + 1) * LANES], w_ref[:, h * LANES:(h + 1) * LANES], row)
        return y * jax.nn.sigmoid(y)

    def l2n(x):
        return x * lax.rsqrt(jnp.sum(x * x, axis=-1, keepdims=True) + EPS)

    nb = hg * n
    ii3 = lax.broadcasted_iota(jnp.int32, (N_DIR * nb, CHUNK, CHUNK), 1)
    jj3 = lax.broadcasted_iota(jnp.int32, (N_DIR * nb, CHUNK, CHUNK), 2)
    fwd = lax.broadcasted_iota(jnp.int32, (N_DIR * nb, CHUNK, CHUNK), 0) < nb
    ahead = jnp.where(fwd, ii3 - jj3, jj3 - ii3)
    tri = ahead >= 0
    strict = ahead > 0
    onehot0 = jnp.broadcast_to(oh_ref[...][None], (N_DIR * nb, CHUNK, LANES))

    for h0 in range(0, heads, hg):
        hs = range(h0, h0 + hg)
        as3 = lambda x: x.reshape(n, CHUNK, LANES)
        q3 = jnp.concatenate([as3(l2n(conv_silu(q_ref, cq_ref, h)) * (dk ** -0.5)) for h in hs], axis=0)
        k3 = jnp.concatenate([as3(l2n(conv_silu(k_ref, ck_ref, h))) for h in hs], axis=0)
        v3 = jnp.concatenate([as3(conv_silu(v_ref, cv_ref, h)) for h in hs], axis=0)
        k3b = k3.astype(BF16)
        gram = _bmm_nt(k3b, k3b)
        qk = _bmm_nt(q3.astype(BF16), k3b)
        decs, tots, betas = [], [], []
        for d in range(N_DIR):
            for h in hs:
                lb = d * heads + h
                la = 2 * heads + lb
                beta = jax.nn.sigmoid(ba[:, lb:lb + 1])
                g = -jnp.exp(al[:, lb:lb + 1]) * _softplus(ba[:, la:la + 1] + dtb[:, lb:lb + 1])
                dec = jnp.broadcast_to(g, (t, LANES))
                step = 1
                while step < CHUNK:
                    if d == 0:
                        sh = pltpu.roll(dec, step, axis=0)
                        dec = dec + jnp.where(pos >= step, sh, 0.0)
                    else:
                        sh = pltpu.roll(dec, t - step, axis=0)
                        dec = dec + jnp.where(pos < CHUNK - step, sh, 0.0)
                    step *= 2
                dec3 = as3(dec)
                decs.append(dec3)
                tots.append(jnp.broadcast_to(dec3[:, CHUNK - 1:CHUNK, :] if d == 0 else dec3[:, 0:1, :],
                                             (n, CHUNK, LANES)))
                betas.append(as3(jnp.broadcast_to(beta, (t, LANES))))
        dec3 = jnp.concatenate(decs, axis=0)
        tot3 = jnp.concatenate(tots, axis=0)
        beta3 = jnp.concatenate(betas, axis=0)
        both = lambda x: jnp.concatenate([x, x], axis=0)
        dec_row = _bmm_nt(onehot0, dec3, precision=lax.Precision.HIGHEST)
        dec_col = dec3[:, :, :CHUNK]
        gamma = jnp.where(tri, jnp.exp(jnp.where(tri, dec_col - dec_row, 0.0)), 0.0)
        a = jnp.where(strict, both(gram) * gamma, 0.0) * beta3[:, :, :CHUNK]
        xinv = _unit_tri_inverse(a, ii, jj).astype(BF16)
        attn = jnp.where(tri, both(qk) * gamma, 0.0).astype(BF16)
        edec = jnp.exp(dec3)
        k32, q32, v32 = both(k3), both(q3), both(v3)
        kbd = (k32 * (beta3 * edec)).astype(BF16)
        qd = (q32 * edec).astype(BF16)
        kd = jnp.swapaxes(k32 * jnp.exp(tot3 - dec3), 1, 2).astype(BF16)
        vb = v32 * beta3
        cd = jnp.exp(tot3[:, :SUBLANES, :])
        for d in range(N_DIR):
            src = slice(d * nb, (d + 1) * nb)
            dst = pl.ds((d * heads + h0) * n, nb)
            x_scr[dst] = xinv[src]
            at_scr[dst] = attn[src]
            kq_scr[dst, :CHUNK, :] = kbd[src]
            kq_scr[dst, CHUNK:, :] = qd[src]
            kd_scr[dst] = kd[src]
            vb_scr[dst] = vb[src]
            cd_scr[dst] = cd[src]

    for d in range(N_DIR):
        for h in range(heads):
            if has_s0:
                st_scrs[d * heads + h][...] = s0_ref[d, h]
            else:
                st_scrs[d * heads + h][...] = jnp.zeros((dk, LANES), F32)

    def scan(it, carry):
        chains = [(d * heads + h, (d * heads + h) * n + (it if d == 0 else n - 1 - it))
                  for h in range(heads) for d in range(N_DIR)]
        ss = [st_scrs[i][...] for i, _ in chains]
        rqs = [jnp.dot(kq_scr[b], s.astype(BF16), preferred_element_type=F32)
               for (_, b), s in zip(chains, ss)]
        vns = [jnp.dot(x_scr[b], (vb_scr[b] - rq[:CHUNK]).astype(BF16), preferred_element_type=F32).astype(BF16)
               for (_, b), rq in zip(chains, rqs)]
        for (i, b), s, rq, vn in zip(chains, ss, rqs, vns):
            o_scr[b] = rq[CHUNK:] + jnp.dot(at_scr[b], vn, preferred_element_type=F32)
            st_scrs[i][...] = s * cd_scr[b][0:1, :] + jnp.dot(kd_scr[b], vn, preferred_element_type=F32)
        return carry

    lax.fori_loop(0, n, scan, 0)

    for h in range(heads):
        for d in range(N_DIR):
            s_ref[d, h] = st_scrs[d * heads + h][...]
        of = o_scr[pl.ds(h * n, n)]
        ob = o_scr[pl.ds((heads + h) * n, n)]
        o = (of + ob).reshape(t, LANES)
        o = o * lax.rsqrt(jnp.mean(o * o, axis=-1, keepdims=True) + EPS) * ng_ref[...]
        z = z_ref[:, h * LANES:(h + 1) * LANES]
        o_ref[:, h * LANES:(h + 1) * LANES] = (o * (z * jax.nn.sigmoid(z))).astype(o_ref.dtype)


def _gdn_seq(l, cols, ba, conv_qkv, al, dtb, norm_g, s0, row0, nb, t, heads, dk, s_acc=None):
    blk0 = row0 // t
    has_s0 = s0 is not None
    has_acc = s_acc is not None
    hw = heads * LANES
    n = t // CHUNK
    col = lambda off: pl.BlockSpec((t, hw), lambda b: (blk0 + b, off))
    cw = lambda off: pl.BlockSpec((None, CONV_W, hw), lambda b: (l, 0, off))
    vec = pl.BlockSpec((None, 1, LANES), lambda b: (l, 0, 0))
    in_specs = [
        col(0), col(1), col(2), col(3),
        pl.BlockSpec((t, LANES), lambda b: (blk0 + b, 0)),
        cw(0), cw(1), cw(2),
        vec, vec, vec,
        pl.BlockSpec((CHUNK, LANES), lambda b: (0, 0)),
    ]
    onehot0 = jnp.zeros((CHUNK, LANES), F32).at[:, 0].set(1.0)
    args = [cols, cols, cols, cols, ba, conv_qkv, conv_qkv, conv_qkv, al, dtb, norm_g, onehot0]
    if has_s0:
        in_specs.append(pl.BlockSpec((None, None, N_DIR, heads, dk, LANES), lambda b: (b, l, 0, 0, 0, 0)))
        args.append(s0)
    if has_acc:
        in_specs.append(pl.BlockSpec(memory_space=pl.ANY))
        args.append(s_acc)
        state_spec = pl.BlockSpec((None, None, N_DIR, heads, dk, LANES), lambda b: (b, l, 0, 0, 0, 0))
        state_shape = jax.ShapeDtypeStruct(s_acc.shape, F32)
        aliases = {len(args) - 1: 1}
    else:
        state_spec = pl.BlockSpec((None, N_DIR, heads, dk, LANES), lambda b: (b, 0, 0, 0, 0))
        state_shape = jax.ShapeDtypeStruct((nb, N_DIR, heads, dk, LANES), F32)
        aliases = {}
    per = (N_DIR * heads * n,)
    hg = max(1, min(heads, GDN_BATCH // (N_DIR * n)))
    assert heads % hg == 0
    return pl.pallas_call(
        functools.partial(_gdn_seq_kernel, t=t, has_s0=has_s0, has_acc=has_acc, heads=heads, dk=dk, hg=hg),
        grid=(nb,),
        in_specs=in_specs,
        out_specs=[pl.BlockSpec((t, hw), lambda b: (b, 0)), state_spec],
        out_shape=[jax.ShapeDtypeStruct((nb * t, hw), BF16), state_shape],
        input_output_aliases=aliases,
        scratch_shapes=[
            pltpu.VMEM(per + (CHUNK, CHUNK), BF16),
            pltpu.VMEM(per + (CHUNK, CHUNK), BF16),
            pltpu.VMEM(per + (2 * CHUNK, LANES), BF16),
            pltpu.VMEM(per + (dk, CHUNK), BF16),
            pltpu.VMEM(per + (CHUNK, LANES), F32),
            pltpu.VMEM(per + (SUBLANES, LANES), F32),
            pltpu.VMEM(per + (CHUNK, LANES), F32),
        ] + [pltpu.VMEM((dk, LANES), F32)] * (N_DIR * heads),
        compiler_params=_cparams(("parallel",)),
        name=f"gdn_t{t}",
    )(*args)


def _lru_kernel(*refs, t, has_h0):
    if has_h0:
        (lx_ref, ly_ref, cw_ref, cb_ref, wa_ref, wx_ref, ba_ref, bx_ref, lam_ref, h0_ref,
         y_ref, last_ref, a_scr, b_scr, h_scr) = refs
    else:
        (lx_ref, ly_ref, cw_ref, cb_ref, wa_ref, wx_ref, ba_ref, bx_ref, lam_ref,
         y_ref, last_ref, a_scr, b_scr, h_scr) = refs
        h0_ref = None
    nblk = t // SUBLANES
    wl = lx_ref.shape[1]
    row = lax.broadcasted_iota(jnp.int32, (t, wl), 0)
    sub = lax.broadcasted_iota(jnp.int32, (nblk, SUBLANES, wl), 1)
    u = _conv_rows(lx_ref[...], cw_ref[...], row) + cb_ref[...]
    ub = u.astype(BF16)

    def block_diag(w_ref, d):
        return jnp.concatenate(
            [jnp.dot(ub[:, c * LANES:(c + 1) * LANES], w_ref[d, c], preferred_element_type=F32)
             for c in range(wl // LANES)], axis=1)

    for d in range(N_DIR):
        r = jax.nn.sigmoid(block_diag(wa_ref, d) + ba_ref[d:d + 1, :])
        i = jax.nn.sigmoid(block_diag(wx_ref, d) + bx_ref[d:d + 1, :])
        log_a = (LRU_C * r) * (-_softplus(-lam_ref[d:d + 1, :]))
        a = jnp.exp(log_a)
        b = jnp.sqrt(1.0 - a * a) * (i * u)
        a3 = a.reshape(nblk, SUBLANES, wl)
        b3 = b.reshape(nblk, SUBLANES, wl)
        step = 1
        while step < SUBLANES:
            if d == 0:
                a_s = pltpu.roll(a3, step, axis=1)
                b_s = pltpu.roll(b3, step, axis=1)
                m = sub >= step
            else:
                a_s = pltpu.roll(a3, SUBLANES - step, axis=1)
                b_s = pltpu.roll(b3, SUBLANES - step, axis=1)
                m = sub < SUBLANES - step
            b3 = b3 + a3 * jnp.where(m, b_s, 0.0)
            a3 = a3 * jnp.where(m, a_s, 1.0)
            step *= 2
        a_scr[d] = a3
        b_scr[d] = b3

    if has_h0:
        h_init = (jnp.broadcast_to(h0_ref[0:1, :], (SUBLANES, wl)),
                  jnp.broadcast_to(h0_ref[1:2, :], (SUBLANES, wl)))
    else:
        h_init = (jnp.zeros((SUBLANES, wl), F32), jnp.zeros((SUBLANES, wl), F32))

    def body(it, carry):
        hf, hb = carry
        kf = it
        kb = nblk - 1 - it
        new_f = b_scr[0, kf] + a_scr[0, kf] * hf
        new_b = b_scr[1, kb] + a_scr[1, kb] * hb
        h_scr[0, kf] = new_f
        h_scr[1, kb] = new_b
        hf = jnp.broadcast_to(new_f[SUBLANES - 1:SUBLANES, :], (SUBLANES, wl))
        hb = jnp.broadcast_to(new_b[0:1, :], (SUBLANES, wl))
        return hf, hb

    hf, hb = lax.fori_loop(0, nblk, body, h_init, unroll=4)
    last_ref[0:1, :] = hf[0:1, :]
    last_ref[1:2, :] = hb[0:1, :]
    rec = (h_scr[0] + h_scr[1]).reshape(t, wl)
    y_ref[...] = (jax.nn.gelu(ly_ref[...]) * rec).astype(y_ref.dtype)


def _lru(l, cols, conv_w, conv_b, wa, wx, ba, bx, lam, h0, row0, nb, t, lx_blk, ly_blk):
    nblocks = wa.shape[2]
    w = nblocks * LANES
    cg = max(1, min(nblocks, LRU_STEP_ELEMS // (t * LANES)))
    wl = cg * LANES
    blk0 = row0 // t
    has_h0 = h0 is not None
    in_specs = [
        pl.BlockSpec((t, wl), lambda b, c: (blk0 + b, lx_blk // cg + c)),
        pl.BlockSpec((t, wl), lambda b, c: (blk0 + b, ly_blk // cg + c)),
        pl.BlockSpec((None, CONV_W, wl), lambda b, c: (l, 0, c)),
        pl.BlockSpec((None, 1, wl), lambda b, c: (l, 0, c)),
        pl.BlockSpec((None, N_DIR, cg, LANES, LANES), lambda b, c: (l, 0, c, 0, 0)),
        pl.BlockSpec((None, N_DIR, cg, LANES, LANES), lambda b, c: (l, 0, c, 0, 0)),
        pl.BlockSpec((None, N_DIR, wl), lambda b, c: (l, 0, c)),
        pl.BlockSpec((None, N_DIR, wl), lambda b, c: (l, 0, c)),
        pl.BlockSpec((None, N_DIR, wl), lambda b, c: (l, 0, c)),
    ]
    args = [cols, cols, conv_w, conv_b, wa, wx, ba, bx, lam]
    if has_h0:
        in_specs.append(pl.BlockSpec((None, None, N_DIR, wl), lambda b, c: (b, l, 0, c)))
        args.append(h0)
    nblk = t // SUBLANES
    assert lx_blk % cg == 0 and ly_blk % cg == 0 and nblocks % cg == 0
    return pl.pallas_call(
        functools.partial(_lru_kernel, t=t, has_h0=has_h0),
        grid=(nb, nblocks // cg),
        in_specs=in_specs,
        out_specs=[
            pl.BlockSpec((t, wl), lambda b, c: (b, c)),
            pl.BlockSpec((None, N_DIR, wl), lambda b, c: (b, 0, c)),
        ],
        out_shape=[
            jax.ShapeDtypeStruct((nb * t, w), BF16),
            jax.ShapeDtypeStruct((nb, N_DIR, w), F32),
        ],
        scratch_shapes=[pltpu.VMEM((N_DIR, nblk, SUBLANES, wl), F32)] * 3,
        compiler_params=_cparams(("parallel", "parallel")),
        name=f"lru_t{t}",
    )(*args)


MERGE_PARTS = 2


def _merge_kernel(oap_ref, oas_ref, obp_ref, obs_ref, ga_ref, gb_ref, x_ref, g1_ref, sh2_ref, sc2_ref,
                  n2_ref, wdn_ref, wlru_ref, wo_ref, wr_ref, x1_ref, hn_ref, afft_ref, *, n_experts, tiles_p):
    d = x_ref.shape[1]
    tm = x_ref.shape[0]
    is_p = pl.program_id(0) < tiles_p
    parts = [slice(r, r + tm // MERGE_PARTS) for r in range(0, tm, tm // MERGE_PARTS)]
    yas = [jnp.dot(jnp.where(is_p, oap_ref[p, :], oas_ref[p, :]), wdn_ref[...], preferred_element_type=F32)
           for p in parts]
    ybs = [jnp.dot(jnp.where(is_p, obp_ref[p, :], obs_ref[p, :]), wlru_ref[...], preferred_element_type=F32)
           for p in parts]
    mixes = [(jax.nn.sigmoid(ga_ref[p, :]) * ya + jax.nn.sigmoid(gb_ref[p, :]) * yb).astype(BF16)
             for p, ya, yb in zip(parts, yas, ybs)]
    outs = [jnp.dot(m, wo_ref[...], preferred_element_type=F32) for m in mixes]
    hns = []
    for p, mo in zip(parts, outs):
        x1 = x_ref[p, :] + g1_ref[...] * mo
        x1_ref[p, :] = x1
        y = x1 * lax.rsqrt(jnp.mean(x1 * x1, axis=-1, keepdims=True) + EPS) * n2_ref[...]
        hns.append(y * (1.0 + sc2_ref[...]) + sh2_ref[...])
    logit_parts = [jnp.dot(hn.astype(BF16), wr_ref[...], preferred_element_type=F32) for hn in hns]
    for p, hn, logits in zip(parts, hns, logit_parts):
        lane = lax.broadcasted_iota(jnp.int32, logits.shape, 1)
        logits = jnp.where(lane < n_experts, logits, -jnp.inf)
        e = jnp.exp(logits - jnp.max(logits, axis=-1, keepdims=True))
        aff = e / jnp.sum(e, axis=-1, keepdims=True)
        hn_ref[p, :d] = hn
        hn_ref[p, d:] = aff
        afft_ref[:, p] = aff.T[:n_experts, :]


def _merge(l, o_p, o_s, y_p, y_s, cols, x, mod5, norm2_g, w_dn, w_lru, w_o, w_r, row_of_tile, tm,
           ga_blk, gb_blk, n_experts):
    n, d = x.shape
    dv = o_p.shape[1]
    w = y_p.shape[1]
    tiles_p = o_p.shape[0] // tm
    modspec = lambda k: pl.BlockSpec((None, None, None, 1, d), lambda i: (l, row_of_tile(i), k, 0, 0))
    layer = lambda a: pl.BlockSpec((None,) + a.shape[1:], lambda i: (l,) + (0,) * (a.ndim - 1))
    p_tile = lambda width: pl.BlockSpec((tm, width), lambda i: (jnp.minimum(i, tiles_p - 1), 0))
    s_tile = lambda width: pl.BlockSpec((tm, width), lambda i: (jnp.maximum(i - tiles_p, 0), 0))
    return pl.pallas_call(
        functools.partial(_merge_kernel, n_experts=n_experts, tiles_p=tiles_p),
        grid=(n // tm,),
        in_specs=[
            p_tile(dv), s_tile(dv), p_tile(w), s_tile(w),
            pl.BlockSpec((tm, d), lambda i: (i, ga_blk)),
            pl.BlockSpec((tm, d), lambda i: (i, gb_blk)),
            pl.BlockSpec((tm, d), lambda i: (i, 0)),
            modspec(2), modspec(3), modspec(4),
            layer(norm2_g), layer(w_dn), layer(w_lru), layer(w_o), layer(w_r),
        ],
        out_specs=[
            pl.BlockSpec((tm, d), lambda i: (i, 0)),
            pl.BlockSpec((tm, d + LANES), lambda i: (i, 0)),
            pl.BlockSpec((n_experts, tm), lambda i: (0, i)),
        ],
        out_shape=[
            jax.ShapeDtypeStruct((n, d), F32),
            jax.ShapeDtypeStruct((n, d + LANES), F32),
            jax.ShapeDtypeStruct((n_experts, n), F32),
        ],
        compiler_params=_cparams(("parallel",)),
        name="merge",
    )(o_p, o_s, y_p, y_s, cols, cols, x, mod5, mod5, mod5, norm2_g, w_dn, w_lru, w_o, w_r)


FFN_PARTS = 2


def _ffn_kernel(xp_ref, xs_ref, wg_ref, wu_ref, wd_ref, o_ref, xb_scr):
    f = pl.program_id(1)
    cap_p = xp_ref.shape[0]
    d = xb_scr.shape[1]

    @pl.when(f == 0)
    def _():
        xb_scr[:cap_p, :] = xp_ref[:, :d].astype(BF16)
        xb_scr[cap_p:, :] = xs_ref[:, :d].astype(BF16)
        o_ref[...] = jnp.zeros_like(o_ref)

    wg = wg_ref[...].astype(BF16)
    wu = wu_ref[...].astype(BF16)
    wd = wd_ref[...].astype(BF16)
    rows = xb_scr.shape[0]
    parts = [slice(r, r + rows // FFN_PARTS) for r in range(0, rows, rows // FFN_PARTS)]
    gs = [jnp.dot(xb_scr[p, :], wg, preferred_element_type=F32) for p in parts]
    us = [jnp.dot(xb_scr[p, :], wu, preferred_element_type=F32) for p in parts]
    for p, g, u in zip(parts, gs, us):
        hid = ((g * jax.nn.sigmoid(g)) * u).astype(BF16)
        o_ref[p, :] += jnp.dot(hid, wd, preferred_element_type=F32)

    @pl.when(f == pl.num_programs(1) - 1)
    def _():
        e = pl.program_id(0)
        for ref, r0 in ((xp_ref, 0), (xs_ref, cap_p)):
            aff = ref[:, d:]
            lane = lax.broadcasted_iota(jnp.int32, aff.shape, 1)
            gv = jnp.sum(jnp.where(lane == e, aff, 0.0), axis=1, keepdims=True)
            rows = ref.shape[0]
            o_ref[r0:r0 + rows, :] = o_ref[r0:r0 + rows, :] * gv


def _ffn(l, xe_p, xe_s, w_gate, w_up, w_down, tf):
    e, cap_p, da = xe_p.shape
    d = da - LANES
    cap_s = xe_s.shape[1]
    r = cap_p + cap_s
    ff = w_gate.shape[3]
    return pl.pallas_call(
        _ffn_kernel,
        grid=(e, ff // tf),
        in_specs=[
            pl.BlockSpec((None, cap_p, da), lambda i, f: (i, 0, 0)),
            pl.BlockSpec((None, cap_s, da), lambda i, f: (i, 0, 0)),
            pl.BlockSpec((None, None, d, tf), lambda i, f: (l, i, 0, f)),
            pl.BlockSpec((None, None, d, tf), lambda i, f: (l, i, 0, f)),
            pl.BlockSpec((None, None, tf, d), lambda i, f: (l, i, f, 0)),
        ],
        out_specs=pl.BlockSpec((None, r, d), lambda i, f: (i, 0, 0)),
        out_shape=jax.ShapeDtypeStruct((e, r, d), F32),
        scratch_shapes=[pltpu.VMEM((r, d), BF16)],
        compiler_params=_cparams(("parallel", "arbitrary")),
        name="expert_ffn",
    )(xe_p, xe_s, w_gate, w_up, w_down)


ROUTE_TB = 128


def _thr_kernel(afft_ref, o_ref, *, groups):
    n_e = afft_ref.shape[0]
    sub = lax.broadcasted_iota(jnp.int32, (n_e, LANES), 0)
    lane = lax.broadcasted_iota(jnp.int32, (n_e, LANES), 1)
    rows = []
    for lo, hi, cap in groups:
        a = afft_ref[:, lo:hi]
        above_all = 4.0

        def count_ge(v, a=a):
            return jnp.sum((a >= v).astype(F32), axis=1, keepdims=True)

        def count_gt(v, a=a):
            return jnp.sum((a > v).astype(F32), axis=1, keepdims=True)

        def bisect(i, lh, cap=cap, count_ge=count_ge):
            lo_v, hi_v = lh
            mid = 0.5 * (lo_v + hi_v)
            ok = count_ge(mid) >= cap
            return jnp.where(ok, mid, lo_v), jnp.where(ok, hi_v, mid)

        lo_v, _ = lax.fori_loop(0, 48, bisect, (jnp.zeros((n_e, 1), F32), jnp.full((n_e, 1), 2.0, F32)))
        thr = jnp.min(jnp.where(a >= lo_v, a, above_all), axis=1, keepdims=True)

        def not_done(thr, cap=cap, count_gt=count_gt):
            return jnp.max(count_gt(thr)) >= cap

        def step_up(thr, a=a, cap=cap, count_gt=count_gt):
            nxt = jnp.min(jnp.where(a > thr, a, above_all), axis=1, keepdims=True)
            return jnp.where(count_gt(thr) >= cap, nxt, thr)

        thr = lax.while_loop(not_done, step_up, thr)
        need = cap - count_gt(thr)
        for col in (thr, need):
            m = jnp.where(sub == lane, jnp.broadcast_to(col, (n_e, LANES)), 0.0)
            rows.append(jnp.sum(m, axis=0, keepdims=True))
    rows.append(jnp.zeros((SUBLANES - len(rows), LANES), F32))
    o_ref[...] = jnp.concatenate(rows, axis=0)


def _thresholds(afft, groups):
    n_e, n = afft.shape
    return pl.pallas_call(
        functools.partial(_thr_kernel, groups=groups),
        grid=(1,),
        in_specs=[pl.BlockSpec((n_e, n), lambda i: (0, 0))],
        out_specs=pl.BlockSpec((SUBLANES, LANES), lambda i: (0, 0)),
        out_shape=jax.ShapeDtypeStruct((SUBLANES, LANES), F32),
        compiler_params=_cparams(("arbitrary",)),
        name="route_thresholds",
    )(afft)


def _route_kernel(aff_ref, tn_ref, ls_ref, slot_ref, off_ref, cnt_ref, lst_ref,
                  run_eq, run_sel, *, nblk_p, cap_p, n_e):
    tb = ROUTE_TB
    j = pl.program_id(0)

    @pl.when(j == 0)
    def _():
        run_eq[...] = jnp.zeros_like(run_eq)
        run_sel[...] = jnp.zeros_like(run_sel)

    @pl.when(j == nblk_p)
    def _():
        run_eq[...] = jnp.zeros_like(run_eq)
        run_sel[...] = jnp.full_like(run_sel, float(cap_p))

    g = (j >= nblk_p).astype(jnp.int32)
    thr = tn_ref[pl.ds(2 * g, 1), :]
    need = tn_ref[pl.ds(2 * g + 1, 1), :]
    a = aff_ref[...]
    lane = lax.broadcasted_iota(jnp.int32, (tb, LANES), 1)
    valid = lane < n_e
    gt = (a > thr) & valid
    eq = (a == thr) & valid
    eqf = eq.astype(F32)
    ls = ls_ref[...]
    eq_rank = run_eq[...] + jnp.dot(ls, eqf.astype(BF16), preferred_element_type=F32)
    sel = gt | (eq & (eq_rank < need))
    self_ = sel.astype(F32)
    pos = jnp.dot(ls, self_.astype(BF16), preferred_element_type=F32)
    cnt = jnp.sum(self_, axis=0, keepdims=True)
    off = run_sel[...]
    run_eq[...] = run_eq[...] + jnp.sum(eqf, axis=0, keepdims=True)
    run_sel[...] = off + cnt
    slot_ref[...] = jnp.where(sel, off + pos, -1.0)
    off_ref[...] = off.astype(jnp.int32)
    cnt_ref[...] = cnt.astype(jnp.int32)

    tok = (lax.broadcasted_iota(jnp.int32, (tb, LANES), 0) + j * tb).astype(F32)
    lanef = lane.astype(F32)
    local = jnp.where(sel, pos, -1.0)
    for e in range(n_e):
        hit = jnp.broadcast_to(local[:, e:e + 1], (tb, LANES)) == lanef
        lst_ref[e] = jnp.sum(jnp.where(hit, tok, 0.0), axis=0, keepdims=True).astype(jnp.int32)


def _route(hn_aug, thr_need, n_p, cap_p, n_e):
    n = hn_aug.shape[0]
    aff_blk = hn_aug.shape[1] // LANES - 1
    tb = ROUTE_TB
    nblk = n // tb
    ls = jnp.tril(jnp.ones((tb, tb), F32), -1).astype(BF16)
    blk_row = pl.BlockSpec((None, 1, LANES), lambda j: (j, 0, 0))
    return pl.pallas_call(
        functools.partial(_route_kernel, nblk_p=n_p // tb, cap_p=cap_p, n_e=n_e),
        grid=(nblk,),
        in_specs=[
            pl.BlockSpec((tb, LANES), lambda j: (j, aff_blk)),
            pl.BlockSpec((SUBLANES, LANES), lambda j: (0, 0)),
            pl.BlockSpec((tb, tb), lambda j: (0, 0)),
        ],
        out_specs=[
            pl.BlockSpec((tb, LANES), lambda j: (j, 0)),
            blk_row,
            blk_row,
            pl.BlockSpec((n_e, None, 1, LANES), lambda j: (0, j, 0, 0)),
        ],
        out_shape=[
            jax.ShapeDtypeStruct((n, LANES), F32),
            jax.ShapeDtypeStruct((nblk, 1, LANES), jnp.int32),
            jax.ShapeDtypeStruct((nblk, 1, LANES), jnp.int32),
            jax.ShapeDtypeStruct((n_e, nblk, 1, LANES), jnp.int32),
        ],
        scratch_shapes=[pltpu.VMEM((1, LANES), F32), pltpu.VMEM((1, LANES), F32)],
        compiler_params=_cparams(("arbitrary",)),
        name="route_slots",
    )(hn_aug, thr_need, ls)


GATHER_UNROLL = 4


def _gather_kernel(off_ref, cnt_ref, lst_hbm, hn_hbm, o_ref, hn_scr, lst_smem, sem_h, sem_l,
                   *, row0, slot0, blk0, nblk_g):
    e = pl.program_id(0)
    load_lst = pltpu.make_async_copy(lst_hbm.at[e, pl.ds(blk0, nblk_g)], lst_smem, sem_l)
    load_lst.start()

    @pl.when(e == 0)
    def _():
        rows = hn_scr.shape[0]
        load = pltpu.make_async_copy(hn_hbm.at[pl.ds(row0, rows)], hn_scr, sem_h)
        load.start()
        load.wait()

    load_lst.wait()

    def block(jb, carry):
        c = cnt_ref[blk0 + jb, e]
        o = off_ref[blk0 + jb, e] - slot0

        def copy_row(q):
            t = lst_smem[jb, q] - row0
            o_ref[pl.ds(o + q, 1), :] = hn_scr[pl.ds(t, 1), :]

        def rows(qq, carry2):
            for u in range(GATHER_UNROLL):
                copy_row(qq * GATHER_UNROLL + u)
            return carry2

        def row(q, carry2):
            copy_row(q)
            return carry2

        full = c // GATHER_UNROLL
        lax.fori_loop(0, full, rows, 0)
        lax.fori_loop(full * GATHER_UNROLL, c, row, 0)
        return carry

    lax.fori_loop(0, nblk_g, block, 0)


def _gather(off, cnt, lst, hn, row0, rows, slot0, cap):
    n_e = lst.shape[0]
    width = hn.shape[1]
    blk0 = row0 // ROUTE_TB
    nblk_g = rows // ROUTE_TB
    grid_spec = pltpu.PrefetchScalarGridSpec(
        num_scalar_prefetch=2,
        grid=(n_e,),
        in_specs=[pl.BlockSpec(memory_space=pl.ANY), pl.BlockSpec(memory_space=pl.ANY)],
        out_specs=pl.BlockSpec((None, cap, width), lambda e, o_, c_: (e, 0, 0)),
        scratch_shapes=[
            pltpu.VMEM((rows, width), F32),
            pltpu.SMEM((nblk_g, LANES), jnp.int32),
            pltpu.SemaphoreType.DMA(()),
            pltpu.SemaphoreType.DMA(()),
        ],
    )
    return pl.pallas_call(
        functools.partial(_gather_kernel, row0=row0, slot0=slot0, blk0=blk0, nblk_g=nblk_g),
        grid_spec=grid_spec,
        out_shape=jax.ShapeDtypeStruct((n_e, cap, width), F32),
        compiler_params=_cparams(("arbitrary",)),
        name=f"gather_rows{rows}",
    )(off, cnt, lst.reshape(n_e, -1, LANES), hn)


COMB_CH = 32
COMB_GRP = 8
COMB_MAXCH = 80
COMB_DESC = 128


def _combine_kernel(off_ref, cnt_ref, y_hbm, slot_ref, x1_ref, g2_ref, fg_ref, o_ref,
                    buf, acc, desc, sems, *, n_e, final):
    tb = ROUTE_TB
    j = pl.program_id(0)
    nblk = pl.num_programs(0)
    r_total = y_hbm.shape[1]
    par = j % 2

    def issue(jj, p):
        s = jnp.int32(0)
        for e in range(n_e):
            o = off_ref[jj, e]
            c = cnt_ref[jj, e]
            st8 = (o // SUBLANES) * SUBLANES
            nch = jnp.where(c > 0, (o - st8 + c + COMB_CH - 1) // COMB_CH, 0)

            def one(k, s, e=e, st8=st8):
                lo_row = st8 + k * COMB_CH
                base = jnp.minimum(lo_row, r_total - COMB_CH)
                pltpu.make_async_copy(y_hbm.at[e, pl.ds(base, COMB_CH)], buf.at[p, s], sems.at[p]).start()
                desc[p, 0, s] = e
                desc[p, 1, s] = lo_row
                desc[p, 2, s] = base
                return s + 1

            s = lax.fori_loop(0, nch, one, s)
        desc[p, 3, 0] = s

    @pl.when(j == 0)
    def _():
        buf[...] = jnp.zeros_like(buf)

        def clear(i, carry):
            for p in range(2):
                for row in range(4):
                    desc[p, row, i] = 0
            return carry

        lax.fori_loop(0, COMB_DESC, clear, 0)
        issue(j, par)

    @pl.when(j + 1 < nblk)
    def _():
        issue(j + 1, 1 - par)

    n_ch = desc[par, 3, 0]

    def drain(k, carry):
        pltpu.make_async_copy(y_hbm.at[0, pl.ds(0, COMB_CH)], buf.at[par, k], sems.at[par]).wait()
        return carry

    lax.fori_loop(0, n_ch, drain, 0)

    slot = slot_ref[...]
    lane = lax.broadcasted_iota(jnp.int32, (tb, LANES), 1)
    lanef = lane.astype(F32)
    acc[...] = jnp.zeros_like(acc)

    per_lane = LANES // COMB_CH
    n_sub = COMB_GRP // per_lane

    def group(gi, carry):
        ebs = []
        for sub in range(n_sub):
            hit = jnp.zeros((tb, LANES), jnp.bool_)
            for c4 in range(per_lane):
                s = gi * COMB_GRP + sub * per_lane + c4
                e = desc[par, 0, s]
                lo_row = desc[par, 1, s].astype(F32)
                base = desc[par, 2, s].astype(F32)
                col = jnp.sum(jnp.where(lane == e, slot, 0.0), axis=1, keepdims=True)
                colb = jnp.broadcast_to(col, (tb, LANES))
                in_chunk = (lane >= c4 * COMB_CH) & (lane < (c4 + 1) * COMB_CH)
                match = (colb == base + (lanef - float(c4 * COMB_CH))) & (colb >= lo_row)
                hit = hit | (in_chunk & match & (s < n_ch))
            ebs.append(jnp.where(hit, 1.0, 0.0).astype(BF16))
        eb = jnp.concatenate(ebs, axis=1)
        yg = buf[par, pl.ds(gi * COMB_GRP, COMB_GRP)].reshape(COMB_GRP * COMB_CH, -1)
        y1 = yg.astype(BF16)
        r1 = yg - y1.astype(F32)
        y2 = r1.astype(BF16)
        y3 = (r1 - y2.astype(F32)).astype(BF16)
        acc[...] += jnp.dot(jnp.concatenate([eb, eb, eb], axis=1), jnp.concatenate([y1, y2, y3], axis=0),
                            preferred_element_type=F32)
        return carry

    lax.fori_loop(0, (n_ch + COMB_GRP - 1) // COMB_GRP, group, 0)

    x2 = x1_ref[...] + g2_ref[...] * acc[...]
    if final:
        x2 = x2 * lax.rsqrt(jnp.mean(x2 * x2, axis=-1, keepdims=True) + EPS) * fg_ref[...]
    o_ref[...] = x2


def _combine(l, off, cnt, ye, slot, x1, mod5, final_g, row_of_tile, n_e, final):
    n, d = x1.shape
    tb = ROUTE_TB
    assert LANES % COMB_CH == 0 and (COMB_GRP * COMB_CH) % LANES == 0 and COMB_MAXCH % COMB_GRP == 0
    assert COMB_DESC >= COMB_MAXCH >= n_e * -(-(tb + SUBLANES - 1) // COMB_CH)
    grid_spec = pltpu.PrefetchScalarGridSpec(
        num_scalar_prefetch=2,
        grid=(n // tb,),
        in_specs=[
            pl.BlockSpec(memory_space=pl.ANY),
            pl.BlockSpec((tb, LANES), lambda j, o_, c_: (j, 0)),
            pl.BlockSpec((tb, d), lambda j, o_, c_: (j, 0)),
            pl.BlockSpec((None, None, None, 1, d), lambda j, o_, c_: (l, row_of_tile(j), 5, 0, 0)),
            pl.BlockSpec((1, d), lambda j, o_, c_: (0, 0)),
        ],
        out_specs=pl.BlockSpec((tb, d), lambda j, o_, c_: (j, 0)),
        scratch_shapes=[
            pltpu.VMEM((2, COMB_MAXCH, COMB_CH, d), F32),
            pltpu.VMEM((tb, d), F32),
            pltpu.SMEM((2, 4, COMB_DESC), jnp.int32),
            pltpu.SemaphoreType.DMA((2,)),
        ],
    )
    return pl.pallas_call(
        functools.partial(_combine_kernel, n_e=n_e, final=final),
        grid_spec=grid_spec,
        out_shape=jax.ShapeDtypeStruct((n, d), F32),
        compiler_params=_cparams(("arbitrary",)),
        name="combine",
    )(off, cnt, ye, slot, x1, mod5, final_g.reshape(1, d))


def _pos_embed_2d(n_tokens, d_model):
    rows = n_tokens // GRID_W
    r = jnp.broadcast_to(jnp.arange(rows, dtype=F32)[:, None], (rows, GRID_W)).reshape(-1)
    col = jnp.broadcast_to(jnp.arange(GRID_W, dtype=F32)[None, :], (rows, GRID_W)).reshape(-1)
    quarter = d_model // 4
    freq = jnp.exp(-math.log(10000.0) * jnp.arange(quarter, dtype=F32) / quarter)
    ar = r[:, None] * freq
    ac = col[:, None] * freq
    return jnp.concatenate([jnp.sin(ar), jnp.cos(ar), jnp.sin(ac), jnp.cos(ac)], axis=-1)


def kernel(x_prompt, x_sample, state_delta, state_lru, c, c_ctx, norm1_g, w_mod, b_mod, w_in, conv_qkv, dn_a_log, dn_dt_bias, dn_norm_g, w_dn_out, conv_lru_w, conv_lru_b, lru_wa, lru_ba, lru_wx, lru_bx, lru_lambda, w_lru_out, w_o, norm2_g, w_router, w_gate, w_up, w_down, final_g):
    bp, tp, d = x_prompt.shape
    bs, ts, _ = x_sample.shape
    depth = w_in.shape[0]
    heads, dk, dv = state_delta.shape[3:]
    qk = heads * dk
    vw = heads * dv
    lru_w = state_lru.shape[-1]
    n_experts = w_router.shape[-1]
    n_p, n_s = bp * tp, bs * ts
    n = n_p + n_s
    cap_p = 2 * n_p // n_experts
    cap_s = 2 * n_s // n_experts
    assert dk == LANES and dv == LANES and tp % CHUNK == 0 and ts % CHUNK == 0

    xs = x_sample + _pos_embed_2d(ts, d)[None]
    x = jnp.concatenate([x_prompt.reshape(n_p, d), xs.reshape(n_s, d)], axis=0)

    cond8 = jnp.zeros((8, d), F32).at[0].set(c_ctx).at[1:1 + bs].set(c)
    mod = _modulation(cond8, w_mod, b_mod)
    mod = mod.reshape(depth, 8, 6, 1, d)

    n_small = 2 * N_DIR * heads
    c0 = 2 * qk + 2 * vw
    w_main, w_ba = _w_in_prep(w_in, c0, n_small)
    lx_blk = c0 // LANES
    ly_blk = lx_blk + lru_w // LANES
    ga_blk = (c0 + 2 * lru_w) // d
    gb_blk = ga_blk + 1
    w_dn_b = w_dn_out.astype(BF16)
    w_lru_b = w_lru_out.astype(BF16)
    w_o_b = w_o.astype(BF16)
    w_r_b = jnp.pad(w_router, ((0, 0), (0, 0), (0, LANES - n_experts))).astype(BF16)
    wa_b = lru_wa.astype(BF16)
    wx_b = lru_wx.astype(BF16)
    lane_pad = lambda a: jnp.pad(a.reshape(depth, 1, N_DIR * heads), ((0, 0), (0, 0), (0, LANES - N_DIR * heads)))
    al_v = lane_pad(dn_a_log)
    dt_v = lane_pad(dn_dt_bias)
    norm1_3 = norm1_g.reshape(depth, 1, d)
    norm2_3 = norm2_g.reshape(depth, 1, d)
    dn_norm_3 = dn_norm_g.reshape(depth, 1, dv)
    conv_lru_b3 = conv_lru_b.reshape(depth, 1, lru_w)

    tm_mg = 512
    tiles_p_mg = n_p // tm_mg
    per_seq = ts // tm_mg
    row_mg = lambda i: jnp.where(i < tiles_p_mg, 0, (i - tiles_p_mg) // per_seq + 1)
    tiles_p_cb = n_p // ROUTE_TB
    per_seq_cb = ts // ROUTE_TB
    row_cb = lambda i: jnp.where(i < tiles_p_cb, 0, (i - tiles_p_cb) // per_seq_cb + 1)
    assert n_p % tm_mg == 0 and ts % tm_mg == 0 and n_p % ROUTE_TB == 0 and ts % ROUTE_TB == 0
    assert n_experts <= LANES and d % LANES == 0

    sd_acc = jnp.zeros((bp, depth, N_DIR, heads, dk, dv), F32)
    sl_out = []
    for l in range(depth):
        cols, ba = _in_proj(l, x, norm1_3, mod, w_main, w_ba, row_mg, tm_mg)
        o_p, sd_acc = _gdn_seq(l, cols, ba, conv_qkv, al_v, dt_v, dn_norm_3, None, 0, bp, tp, heads, dk,
                               s_acc=sd_acc)
        o_s, _ = _gdn_seq(l, cols, ba, conv_qkv, al_v, dt_v, dn_norm_3, state_delta, n_p, bs, ts, heads, dk)
        y_p, sl_p = _lru(l, cols, conv_lru_w, conv_lru_b3, wa_b, wx_b, lru_ba, lru_bx, lru_lambda, None,
                         0, bp, tp, lx_blk, ly_blk)
        y_s, _ = _lru(l, cols, conv_lru_w, conv_lru_b3, wa_b, wx_b, lru_ba, lru_bx, lru_lambda, state_lru,
                      n_p, bs, ts, lx_blk, ly_blk)
        x1, hn_aug, afft = _merge(l, o_p, o_s, y_p, y_s, cols, x, mod, norm2_3, w_dn_b, w_lru_b, w_o_b, w_r_b,
                                  row_mg, tm_mg, ga_blk, gb_blk, n_experts)
        thr_need = _thresholds(afft, ((0, n_p, cap_p), (n_p, n, cap_s)))
        slot, off, cnt, lst = _route(hn_aug, thr_need, n_p, cap_p, n_experts)
        nblk = n // ROUTE_TB
        off = off.reshape(nblk, LANES)
        cnt = cnt.reshape(nblk, LANES)
        xe_p = _gather(off, cnt, lst, hn_aug, 0, n_p, 0, cap_p)
        xe_s = _gather(off, cnt, lst, hn_aug, n_p, n_s, cap_p, cap_s)
        ye = _ffn(l, xe_p, xe_s, w_gate, w_up, w_down, 512)
        x = _combine(l, off, cnt, ye, slot, x1, mod, final_g, row_cb, n_experts, l == depth - 1)
        sl_out.append(sl_p)

    y = x
    y_prompt = y[:n_p].reshape(bp, tp, d)
    y_sample = y[n_p:].reshape(bs, ts, d)
    return (y_prompt, y_sample, sd_acc, jnp.stack(sl_out, axis=1))
```
